```python
import math
import jax, jax.numpy as jnp
from jax import lax
import numpy as np

D_MODEL = 1024
BATCH = 2
SEQ = 8192
DEPTH = 4

N_A_LAYERS = DEPTH // 2
N_B_LAYERS = DEPTH - N_A_LAYERS

DN_HEADS = 8
DN_HEAD_DIM = D_MODEL // DN_HEADS
DN_WIDTH = DN_HEADS * DN_HEAD_DIM
CONV_K = 4
DN_CHUNK = 64

DA_HEADS = 8
DA_HEAD_DIM = D_MODEL // (2 * DA_HEADS)
DA_V_DIM = 2 * DA_HEAD_DIM
DA_QK_WIDTH = 2 * DA_HEADS * DA_HEAD_DIM
DA_WIDTH = DA_HEADS * DA_V_DIM
Q_BLOCK = 128

N_GROUPS = 4
EXPERTS_PER_GROUP = 8
N_EXPERTS = N_GROUPS * EXPERTS_PER_GROUP
TOP_K = 2
EXPERT_FF = D_MODEL // 2
MOE_BLOCK = 128

ALPHA = (2 * DEPTH) ** 0.25
BETA_INIT = (8 * DEPTH) ** -0.25
LN_EPS = 1e-5
RMS_EPS = 1e-6

kernel_name = 'yoco_deltanet_diffattn_hier_moe'


def layer_norm(x, g, b):
    xf = x.astype(jnp.float32)
    mu = jnp.mean(xf, axis=-1, keepdims=True)
    var = jnp.mean(jnp.square(xf - mu), axis=-1, keepdims=True)
    return ((xf - mu) * lax.rsqrt(var + LN_EPS) * g + b).astype(x.dtype)


def rms_norm(x, w):
    xf = x.astype(jnp.float32)
    return xf * lax.rsqrt(jnp.mean(jnp.square(xf), axis=-1, keepdims=True) + RMS_EPS) * w.astype(jnp.float32)


def l2_normalize(x):
    xf = x.astype(jnp.float32)
    return xf * lax.rsqrt(jnp.sum(jnp.square(xf), axis=-1, keepdims=True) + RMS_EPS)


def causal_depthwise_conv(x, w):
    k_width, t = w.shape[0], x.shape[1]
    xp = jnp.pad(x, ((0, 0), (k_width - 1, 0), (0, 0)))
    y = xp[:, 0:t] * w[0]
    for j in range(1, k_width):
        y = y + xp[:, j:j + t] * w[j]
    return y


def chunked_gated_delta_rule(q, k, v, g, beta):
    b, t, h, dk = q.shape
    dv = v.shape[-1]
    n, c = t // DN_CHUNK, DN_CHUNK

    def to_chunks(a):
        return jnp.moveaxis(a.reshape((b, n, c, h) + a.shape[3:]), 3, 1)

    q = to_chunks(q * dk ** -0.5)
    k = to_chunks(k)
    v = to_chunks(v.astype(jnp.float32))
    gc = jnp.cumsum(to_chunks(g), axis=-1)
    beta = to_chunks(beta)
    idx = jnp.arange(c)
    causal = idx[:, None] >= idx[None, :]
    strict = idx[:, None] > idx[None, :]
    decay = jnp.exp(jnp.where(causal, gc[..., :, None] - gc[..., None, :], -jnp.inf))
    k_beta = k * beta[..., None]
    a_mat = jnp.where(strict, jnp.einsum('bhnid,bhnjd->bhnij', k_beta, k) * decay, 0.0)
    rhs = jnp.concatenate([v * beta[..., None], k_beta * jnp.exp(gc)[..., None]], axis=-1)
    sol = lax.linalg.triangular_solve(a_mat + jnp.eye(c, dtype=jnp.float32), rhs,
                                      left_side=True, lower=True, unit_diagonal=True)
    u, w = sol[..., :dv], sol[..., dv:]
    attn = jnp.where(causal, jnp.einsum('bhnid,bhnjd->bhnij', q, k) * decay, 0.0)
    q_dec = q * jnp.exp(gc)[..., None]
    k_dec = k * jnp.exp(gc[..., -1:] - gc)[..., None]
    g_last = jnp.exp(gc[..., -1])

    def step(state, xs):
        u_i, w_i, q_i, k_i, attn_i, gl_i = xs
        v_new = u_i - jnp.einsum('bhcd,bhdv->bhcv', w_i, state)
        o_i = jnp.einsum('bhcd,bhdv->bhcv', q_i, state) + jnp.einsum('bhij,bhjv->bhiv', attn_i, v_new)
        state = state * gl_i[..., None, None] + jnp.einsum('bhcd,bhcv->bhdv', k_i, v_new)
        return state, o_i

    xs = tuple(jnp.moveaxis(a, 2, 0) for a in (u, w, q_dec, k_dec, attn, g_last))
    state0 = jnp.zeros((b, h, dk, dv), jnp.float32)
    _, o = lax.scan(step, state0, xs)
    return o.transpose(1, 0, 3, 2, 4).reshape(b, t, h, dv)


def gated_deltanet(x, w_in, conv_w, a_log, dt_bias, norm_w, w_out):
    b, t, _ = x.shape
    proj = x @ w_in
    qkv = jax.nn.silu(causal_depthwise_conv(proj[..., :3 * DN_WIDTH], conv_w))
    qkv = qkv.reshape(b, t, 3, DN_HEADS, DN_HEAD_DIM)
    q = l2_normalize(qkv[:, :, 0])
    k = l2_normalize(qkv[:, :, 1])
    v = qkv[:, :, 2]
    z = proj[..., 3 * DN_WIDTH:4 * DN_WIDTH].reshape(b, t, DN_HEADS, DN_HEAD_DIM)
    beta = jax.nn.sigmoid(proj[..., 4 * DN_WIDTH:4 * DN_WIDTH + DN_HEADS].astype(jnp.float32))
    alpha_logit = proj[..., 4 * DN_WIDTH + DN_HEADS:].astype(jnp.float32)
    g = -jnp.exp(a_log.astype(jnp.float32)) * jax.nn.softplus(alpha_logit + dt_bias.astype(jnp.float32))
    o = chunked_gated_delta_rule(q, k, v, g, beta)
    o = rms_norm(o, norm_w) * jax.nn.silu(z.astype(jnp.float32))
    return o.reshape(b, t, DN_WIDTH).astype(x.dtype) @ w_out


def shared_kv(x, kv_w):
    b, t, _ = x.shape
    kv = x @ kv_w
    keys = kv[..., :DA_QK_WIDTH].reshape(b, t, DA_HEADS, 2, DA_HEAD_DIM).transpose(3, 0, 2, 1, 4)
    vals = kv[..., DA_QK_WIDTH:].reshape(b, t, DA_HEADS, DA_V_DIM).transpose(0, 2, 1, 3)
    return keys, vals


def differential_attention(x, keys, vals, w_q, lam_p, subln_w, w_out, lam_init):
    b, t, _ = x.shape
    nb = t // Q_BLOCK
    q = (x @ w_q).reshape(b, nb, Q_BLOCK, DA_HEADS, 2, DA_HEAD_DIM).transpose(1, 4, 0, 3, 2, 5)
    lp = lam_p.astype(jnp.float32)
    lam = jnp.exp(jnp.sum(lp[0] * lp[1])) - jnp.exp(jnp.sum(lp[2] * lp[3])) + lam_init
    key_pos = jnp.arange(t)
    scale = DA_HEAD_DIM ** -0.5

    def block(args):
        q_blk, blk = args
        query_pos = blk * Q_BLOCK + jnp.arange(Q_BLOCK)
        mask = key_pos[None, :] <= query_pos[:, None]
        s = jnp.einsum('sbhqd,sbhkd->sbhqk', q_blk, keys).astype(jnp.float32) * scale
        p = jax.nn.softmax(jnp.where(mask, s, -jnp.inf), axis=-1)
        a = p[0] - lam * p[1]
        return jnp.einsum('bhqk,bhkv->bhqv', a.astype(vals.dtype), vals)

    o = lax.map(block, (q, jnp.arange(nb)))
    o = o.transpose(1, 0, 3, 2, 4).reshape(b, t, DA_HEADS, DA_V_DIM)
    o = rms_norm(o, subln_w) * (1.0 - lam_init)
    return o.reshape(b, t, DA_WIDTH).astype(x.dtype) @ w_out


def hierarchical_moe(x, w_group, w_expert, w13, w2):
    b, t, d = x.shape
    n = b * t
    m = n * TOP_K
    xf = x.reshape(n, d)
    group_prob = jax.nn.softmax((xf @ w_group).astype(jnp.float32), axis=-1)
    p_group, group_idx = lax.top_k(group_prob, 1)
    expert_logits = (xf @ w_expert).astype(jnp.float32).reshape(n, N_GROUPS, EXPERTS_PER_GROUP)
    in_group = expert_logits[jnp.arange(n), group_idx[:, 0]]
    top_p, top_i = lax.top_k(jax.nn.softmax(in_group, axis=-1), TOP_K)
    gate = (p_group * top_p / jnp.sum(top_p, axis=-1, keepdims=True)).reshape(m)
    expert_id = (group_idx * EXPERTS_PER_GROUP + top_i).reshape(m)
    token_id = jnp.repeat(jnp.arange(n), TOP_K)
    order = jnp.argsort(expert_id)
    eid_s, tok_s, gate_s = expert_id[order], token_id[order], gate[order]
    counts = jnp.bincount(expert_id, length=N_EXPERTS)
    start = jnp.cumsum(counts) - counts
    padded = (counts + MOE_BLOCK - 1) // MOE_BLOCK * MOE_BLOCK
    pad_end = jnp.cumsum(padded)
    pad_start = pad_end - padded
    dest = pad_start[eid_s] + jnp.arange(m) - start[eid_s]
    n_rows = (m + N_EXPERTS * (MOE_BLOCK - 1) + MOE_BLOCK - 1) // MOE_BLOCK * MOE_BLOCK
    n_blocks = n_rows // MOE_BLOCK
    row_tok = jnp.zeros((n_rows,), jnp.int32).at[dest].set(tok_s)
    row_gate = jnp.zeros((n_rows,), x.dtype).at[dest].set(gate_s.astype(x.dtype))
    block_eid = jnp.minimum(jnp.searchsorted(pad_end, jnp.arange(n_blocks) * MOE_BLOCK, side='right'),
                            N_EXPERTS - 1)
    x_rows = xf[row_tok].reshape(n_blocks, MOE_BLOCK, d)

    def expert_block(args):
        xb, e = args
        hcat = xb @ w13[e]
        hid = jax.nn.silu(hcat[:, :EXPERT_FF]) * hcat[:, EXPERT_FF:]
        return hid @ w2[e]

    y_rows = lax.map(expert_block, (x_rows, block_eid)).reshape(n_rows, d)
    y = jax.ops.segment_sum(y_rows * row_gate[:, None], row_tok, num_segments=n)
    return y.reshape(b, t, d)


def lambda_init(layer_idx):
    return 0.8 - 0.6 * math.exp(-0.3 * layer_idx)


def setup_inputs(seed: int = 0) -> dict:
    key = jax.random.key(seed)
    ks = jax.random.split(key, 24)
    f32 = jnp.float32
    D = D_MODEL

    def nrm(k, shape, scale):
        return jax.random.normal(k, shape, f32) * scale

    in_a = 4 * DN_WIDTH + 2 * DN_HEADS
    x = nrm(ks[0], (BATCH, SEQ, D), 1.0)
    a_w_in = nrm(ks[1], (N_A_LAYERS, D, in_a), D ** -0.5)
    a_conv_w = nrm(ks[2], (N_A_LAYERS, CONV_K, 3 * DN_WIDTH), CONV_K ** -0.5)
    a_a_log = jnp.log(jax.random.uniform(ks[3], (N_A_LAYERS, DN_HEADS), f32, 1.0, 16.0))
    dt = jnp.exp(jax.random.uniform(ks[4], (N_A_LAYERS, DN_HEADS), f32, math.log(1e-3), math.log(1e-1)))
    a_dt_bias = dt + jnp.log(-jnp.expm1(-dt))
    a_norm_w = 1.0 + nrm(ks[5], (N_A_LAYERS, DN_HEAD_DIM), 0.02)
    a_w_out = nrm(ks[6], (N_A_LAYERS, DN_WIDTH, D), DN_WIDTH ** -0.5 * BETA_INIT)
    kv_w = nrm(ks[7], (D, DA_QK_WIDTH + DA_WIDTH), D ** -0.5)
    b_w_q = nrm(ks[8], (N_B_LAYERS, D, DA_QK_WIDTH), D ** -0.5)
    b_lambda = nrm(ks[9], (N_B_LAYERS, 4, DA_HEAD_DIM), 0.1)
    b_subln_w = 1.0 + nrm(ks[10], (N_B_LAYERS, DA_V_DIM), 0.02)
    b_w_out = nrm(ks[11], (N_B_LAYERS, DA_WIDTH, D), DA_WIDTH ** -0.5 * BETA_INIT)
    ln_mix_g = 1.0 + nrm(ks[12], (DEPTH, D), 0.02)
    ln_mix_b = nrm(ks[13], (DEPTH, D), 0.02)
    ln_ffn_g = 1.0 + nrm(ks[14], (DEPTH, D), 0.02)
    ln_ffn_b = nrm(ks[15], (DEPTH, D), 0.02)
    moe_w_group = nrm(ks[16], (DEPTH, D, N_GROUPS), D ** -0.5)
    moe_w_expert = nrm(ks[17], (DEPTH, D, N_EXPERTS), D ** -0.5)
    moe_w13 = nrm(ks[18], (DEPTH, N_EXPERTS, D, 2 * EXPERT_FF), D ** -0.5)
    moe_w2 = nrm(ks[19], (DEPTH, N_EXPERTS, EXPERT_FF, D), EXPERT_FF ** -0.5 * BETA_INIT)
    return {'x': x, 'a_w_in': a_w_in, 'a_conv_w': a_conv_w, 'a_a_log': a_a_log, 'a_dt_bias': a_dt_bias,
            'a_norm_w': a_norm_w, 'a_w_out': a_w_out, 'kv_w': kv_w, 'b_w_q': b_w_q, 'b_lambda': b_lambda,
            'b_subln_w': b_subln_w, 'b_w_out': b_w_out, 'ln_mix_g': ln_mix_g, 'ln_mix_b': ln_mix_b,
            'ln_ffn_g': ln_ffn_g, 'ln_ffn_b': ln_ffn_b, 'moe_w_group': moe_w_group,
            'moe_w_expert': moe_w_expert, 'moe_w13': moe_w13, 'moe_w2': moe_w2}


def reference(x, a_w_in, a_conv_w, a_a_log, a_dt_bias, a_norm_w, a_w_out, kv_w, b_w_q, b_lambda,
              b_subln_w, b_w_out, ln_mix_g, ln_mix_b, ln_ffn_g, ln_ffn_b, moe_w_group, moe_w_expert,
              moe_w13, moe_w2):
    h = x
    keys, vals = None, None
    for layer in range(DEPTH):
        if layer < N_A_LAYERS:
            mix = gated_deltanet(h, a_w_in[layer], a_conv_w[layer], a_a_log[layer], a_dt_bias[layer],
                                 a_norm_w[layer], a_w_out[layer])
        else:
            j = layer - N_A_LAYERS
            if j == 0:
                keys, vals = shared_kv(h, kv_w)
            mix = differential_attention(h, keys, vals, b_w_q[j], b_lambda[j], b_subln_w[j], b_w_out[j],
                                         lambda_init(layer))
        h = layer_norm(ALPHA * h + mix, ln_mix_g[layer], ln_mix_b[layer])
        ffn = hierarchical_moe(h, moe_w_group[layer], moe_w_expert[layer], moe_w13[layer], moe_w2[layer])
        h = layer_norm(ALPHA * h + ffn, ln_ffn_g[layer], ln_ffn_b[layer])
    return h
```

```python
import functools
import math

import jax
import jax.numpy as jnp
from jax import lax
from jax.experimental import pallas as pl
from jax.experimental.pallas import tpu as pltpu

F32 = jnp.float32
I32 = jnp.int32

DEPTH = 4
N_A_LAYERS = DEPTH // 2
DN_HEADS = 8
DN_HEAD_DIM = 128
CONV_K = 4
DN_CHUNK = 64
DA_HEADS = 8
DA_HEAD_DIM = 64
DA_V_DIM = 2 * DA_HEAD_DIM
N_GROUPS = 4
EXPERTS_PER_GROUP = 8
N_EXPERTS = N_GROUPS * EXPERTS_PER_GROUP
TOP_K = 2
ALPHA = (2 * DEPTH) ** 0.25
LN_EPS = 1e-5
RMS_EPS = 1e-6

LANES = 128
SUBLANES = 8
VMEM_LIMIT_BYTES = 56 * 1024 * 1024

MM_TM = 1024
MM_TN = 512
LN_TM = 512
CONV_TT = 256
DN_ROWS = 256
ATT_T = 512
ROUTE_TM = 512
MOE_TB = 256
DISP_TM = 512
COMB_TM = 256


def _cparams(*sem):
    return pltpu.CompilerParams(dimension_semantics=sem, vmem_limit_bytes=VMEM_LIMIT_BYTES)


def _silu(x):
    return x * (1.0 / (1.0 + jnp.exp(-x)))


def _layer_norm(x, g, b):
    mu = jnp.mean(x, axis=-1, keepdims=True)
    xc = x - mu
    var = jnp.mean(xc * xc, axis=-1, keepdims=True)
    return xc * lax.rsqrt(var + LN_EPS) * g + b


def _mm_body(x_ref, w_ref, o_ref):
    o_ref[...] = jnp.dot(x_ref[...], w_ref[...], preferred_element_type=F32)


def _matmul(x, w, layer, n_out, name):
    m, k = x.shape
    return pl.pallas_call(
        _mm_body,
        grid=(m // MM_TM, n_out // MM_TN),
        in_specs=[pl.BlockSpec((MM_TM, k), lambda i, j: (i, 0)),
                  pl.BlockSpec((None, k, MM_TN), lambda i, j: (layer, 0, j))],
        out_specs=pl.BlockSpec((MM_TM, MM_TN), lambda i, j: (i, j)),
        out_shape=jax.ShapeDtypeStruct((m, n_out), F32),
        compiler_params=_cparams("parallel", "parallel"),
        name=name,
    )(x, w)


def _gates_body(x_ref, w_ref, alog_ref, dtb_ref, o_ref):
    logit = jnp.dot(x_ref[...], w_ref[...], preferred_element_type=F32)
    lane = lax.broadcasted_iota(I32, logit.shape, 1)
    beta = 1.0 / (1.0 + jnp.exp(-logit))
    sp_in = logit + dtb_ref[...]
    softplus = jnp.maximum(sp_in, 0.0) + jnp.log1p(jnp.exp(-jnp.abs(sp_in)))
    g = -jnp.exp(alog_ref[...]) * softplus
    o_ref[...] = jnp.where(lane < DN_HEADS, beta, g)


def _dn_gates(x, w_small, a_log, dt_bias):
    m, k = x.shape
    pad = LANES - 2 * DN_HEADS
    w = jnp.pad(w_small, ((0, 0), (0, pad)))
    alog = jnp.pad(a_log, (DN_HEADS, pad)).reshape(1, LANES)
    dtb = jnp.pad(dt_bias, (DN_HEADS, pad)).reshape(1, LANES)
    return pl.pallas_call(
        _gates_body,
        grid=(m // MM_TM,),
        in_specs=[pl.BlockSpec((MM_TM, k), lambda i: (i, 0)),
                  pl.BlockSpec((k, LANES), lambda i: (0, 0)),
                  pl.BlockSpec((1, LANES), lambda i: (0, 0)),
                  pl.BlockSpec((1, LANES), lambda i: (0, 0))],
        out_specs=pl.BlockSpec((MM_TM, LANES), lambda i: (i, 0)),
        out_shape=jax.ShapeDtypeStruct((m, LANES), F32),
        compiler_params=_cparams("parallel"),
        name="dn_gates",
    )(x, w, alog, dtb)


def _conv_body(xq_ref, xk_ref, xv_ref, hq_ref, hk_ref, hv_ref, w_ref, q_ref, k_ref, v_ref):
    first = pl.program_id(1) == 0
    width = xq_ref.shape[-1]

    def conv_silu(x_ref, halo_ref, col0):
        halo = jnp.where(first, 0.0, halo_ref[0])
        xe = jnp.concatenate([halo, x_ref[0]], axis=0)
        tt = x_ref.shape[1]
        acc = xe[SUBLANES:SUBLANES + tt] * w_ref[CONV_K - 1:CONV_K, col0:col0 + width]
        for j in range(CONV_K - 1):
            shifted = pltpu.roll(xe, CONV_K - 1 - j, axis=0)[SUBLANES:SUBLANES + tt]
            acc = acc + shifted * w_ref[j:j + 1, col0:col0 + width]
        return _silu(acc)

    def l2n(y, scale):
        outs = []
        for h in range(DN_HEADS):
            yh = y[:, h * DN_HEAD_DIM:(h + 1) * DN_HEAD_DIM]
            ss = jnp.sum(yh * yh, axis=-1, keepdims=True)
            outs.append(yh * (lax.rsqrt(ss + RMS_EPS) * scale))
        return jnp.concatenate(outs, axis=-1)

    q_ref[0] = l2n(conv_silu(xq_ref, hq_ref, 0), DN_HEAD_DIM ** -0.5)
    k_ref[0] = l2n(conv_silu(xk_ref, hk_ref, width), 1.0)
    v_ref[0] = conv_silu(xv_ref, hv_ref, 2 * width)


def _dn_conv(proj, conv_w, layer):
    b, t, _ = proj.shape
    w = DN_HEADS * DN_HEAD_DIM
    tt = CONV_TT
    hb = tt // SUBLANES

    def xspec(c):
        return pl.BlockSpec((1, tt, w), lambda bi, i: (bi, i, c))

    def hspec(c):
        return pl.BlockSpec((1, SUBLANES, w), lambda bi, i: (bi, jnp.maximum(i * hb - 1, 0), c))

    out = jax.ShapeDtypeStruct((b, t, w), F32)
    ospec = pl.BlockSpec((1, tt, w), lambda bi, i: (bi, i, 0))
    return pl.pallas_call(
        _conv_body,
        grid=(b, t // tt),
        in_specs=[xspec(0), xspec(1), xspec(2), hspec(0), hspec(1), hspec(2),
                  pl.BlockSpec((None, CONV_K, 3 * w), lambda bi, i: (layer, 0, 0))],
        out_specs=[ospec, ospec, ospec],
        out_shape=[out, out, out],
        compiler_params=_cparams("parallel", "parallel"),
        name="dn_conv",
    )(proj, proj, proj, proj, proj, proj, conv_w)


def _split3(x):
    hi = x.astype(jnp.bfloat16).astype(F32)
    r = x - hi
    mid = r.astype(jnp.bfloat16).astype(F32)
    return hi, mid, r - mid


def _nt(a, b):
    return lax.dot_general(a, b, (((1,), (1,)), ((), ())), preferred_element_type=F32)


def _tn(a, b):
    return lax.dot_general(a, b, (((0,), (0,)), ((), ())), preferred_element_type=F32)


def _delta_body(q_ref, k_ref, v_ref, z_ref, gb_ref, nw_ref, o_ref, state_ref):
    c = DN_CHUNK
    d = DN_HEAD_DIM

    @pl.when(pl.program_id(1) == 0)
    def _():
        state_ref[...] = jnp.zeros_like(state_ref)

    ri = lax.broadcasted_iota(I32, (c, c), 0)
    ci = lax.broadcasted_iota(I32, (c, c), 1)
    causal = ri >= ci
    strict = ri > ci
    tri = causal.astype(F32)
    nw = nw_ref[...]

    def chunk(ic, carry):
        rows = pl.ds(pl.multiple_of(ic * c, c), c)
        gbt = gb_ref[0, rows, :]
        g_hi, g_mid, g_lo = _split3(gbt)
        gcol = (jnp.dot(tri, g_hi, preferred_element_type=F32)
                + jnp.dot(tri, g_mid, preferred_element_type=F32)
                + jnp.dot(tri, g_lo, preferred_element_type=F32))
        grow = gcol.T
        for h in range(DN_HEADS):
            cols = slice(h * d, (h + 1) * d)
            qh = q_ref[0, rows, cols]
            kh = k_ref[0, rows, cols]
            vh = v_ref[0, rows, cols]
            beta_c = gbt[:, h:h + 1]
            gc_c = gcol[:, DN_HEADS + h:DN_HEADS + h + 1]
            gc_r = grow[DN_HEADS + h:DN_HEADS + h + 1, :]
            g_last = gc_r[:, c - 1:c]
            decay = jnp.exp(jnp.where(causal, gc_c - gc_r, -jnp.inf))
            eg_c = jnp.exp(gc_c)
            k_beta = kh * beta_c
            a_mat = jnp.where(strict, _nt(k_beta, kh) * decay, 0.0)
            x = jnp.concatenate([vh * beta_c, k_beta * eg_c], axis=-1)
            x = x - jnp.dot(a_mat, x, preferred_element_type=F32)
            p = a_mat
            for _ in range(int(math.log2(c)) - 1):
                p = jnp.dot(p, p, preferred_element_type=F32)
                x = x + jnp.dot(p, x, preferred_element_type=F32)
            u = x[:, :d]
            w = x[:, d:]
            attn = jnp.where(causal, _nt(qh, kh) * decay, 0.0)
            s = state_ref[h]
            v_new = u - jnp.dot(w, s, preferred_element_type=F32)
            o = (jnp.dot(qh * eg_c, s, preferred_element_type=F32)
                 + jnp.dot(attn, v_new, preferred_element_type=F32))
            k_dec = kh * jnp.exp(g_last - gc_c)
            state_ref[h] = s * jnp.exp(g_last) + _tn(k_dec, v_new)
            zh = z_ref[0, rows, cols]
            ms = jnp.mean(o * o, axis=-1, keepdims=True)
            o_ref[0, rows, cols] = o * lax.rsqrt(ms + RMS_EPS) * nw * _silu(zh)
        return carry

    lax.fori_loop(0, q_ref.shape[1] // c, chunk, 0)


def _dn_delta(q, k, v, proj, gb, norm_w, layer):
    b, t, w = q.shape
    rs = DN_ROWS
    spec = pl.BlockSpec((1, rs, w), lambda bi, i: (bi, i, 0))
    return pl.pallas_call(
        _delta_body,
        grid=(b, t // rs),
        in_specs=[spec, spec, spec,
                  pl.BlockSpec((1, rs, w), lambda bi, i: (bi, i, 3)),
                  pl.BlockSpec((1, rs, LANES), lambda bi, i: (bi, i, 0)),
                  pl.BlockSpec((None, 1, DN_HEAD_DIM), lambda bi, i: (layer, 0, 0))],
        out_specs=spec,
        out_shape=jax.ShapeDtypeStruct((b, t, w), F32),
        scratch_shapes=[pltpu.VMEM((DN_HEADS, DN_HEAD_DIM, DN_HEAD_DIM), F32)],
        compiler_params=_cparams("parallel", "arbitrary"),
        name="dn_delta",
    )(q, k, v, proj, gb, norm_w)


def _proj_ln_body(a_ref, w_ref, h_ref, g_ref, b_ref, o_ref):
    y = jnp.dot(a_ref[...], w_ref[...], preferred_element_type=F32)
    o_ref[...] = _layer_norm(ALPHA * h_ref[...] + y, g_ref[...], b_ref[...])


def _proj_res_ln(a, w, h, g, b, w_layer, ln_layer, name):
    m, k = a.shape
    n = h.shape[1]
    row = pl.BlockSpec((None, 1, n), lambda i: (ln_layer, 0, 0))
    return pl.pallas_call(
        _proj_ln_body,
        grid=(m // LN_TM,),
        in_specs=[pl.BlockSpec((LN_TM, k), lambda i: (i, 0)),
                  pl.BlockSpec((None, k, n), lambda i: (w_layer, 0, 0)),
                  pl.BlockSpec((LN_TM, n), lambda i: (i, 0)), row, row],
        out_specs=pl.BlockSpec((LN_TM, n), lambda i: (i, 0)),
        out_shape=jax.ShapeDtypeStruct((m, n), F32),
        compiler_params=_cparams("parallel"),
        name=name,
    )(a, w, h, g, b)


def _attn_body(q_ref, k_ref, v_ref, lam_ref, sw_ref, o_ref, m_ref, l_ref, acc_ref, *, lam_init):
    i = pl.program_id(2)
    t = ATT_T
    q = q_ref[0] * (DA_HEAD_DIM ** -0.5)
    lane = lax.broadcasted_iota(I32, q.shape, 1)
    q_maps = (jnp.where(lane < DA_HEAD_DIM, q, 0.0), jnp.where(lane >= DA_HEAD_DIM, q, 0.0))
    m_ref[...] = jnp.full_like(m_ref, -jnp.inf)
    l_ref[...] = jnp.zeros_like(l_ref)
    acc_ref[...] = jnp.zeros_like(acc_ref)

    def tile(j, masked):
        rows = pl.ds(pl.multiple_of(j * t, t), t)
        k = k_ref[0, rows, :]
        v = v_ref[0, rows, :]
        for s_idx in range(2):
            s = _nt(q_maps[s_idx], k)
            if masked:
                ri = lax.broadcasted_iota(I32, s.shape, 0)
                ci = lax.broadcasted_iota(I32, s.shape, 1)
                s = jnp.where(ci <= ri, s, -jnp.inf)
            m_old = m_ref[s_idx]
            m_new = jnp.maximum(m_old, jnp.max(s, axis=-1, keepdims=True))
            p = jnp.exp(s - m_new)
            corr = jnp.exp(m_old - m_new)
            l_ref[s_idx] = corr * l_ref[s_idx] + jnp.sum(p, axis=-1, keepdims=True)
            acc_ref[s_idx] = corr * acc_ref[s_idx] + jnp.dot(p, v, preferred_element_type=F32)
            m_ref[s_idx] = m_new

    def full_tile(j, carry):
        tile(j, False)
        return carry

    lax.fori_loop(0, i, full_tile, 0)
    tile(i, True)

    lp = lam_ref[...]
    lam = (jnp.exp(jnp.sum(lp[0:1] * lp[1:2], axis=-1, keepdims=True))
           - jnp.exp(jnp.sum(lp[2:3] * lp[3:4], axis=-1, keepdims=True)) + lam_init)
    o = acc_ref[0] / l_ref[0] - lam * (acc_ref[1] / l_ref[1])
    ms = jnp.mean(o * o, axis=-1, keepdims=True)
    o_ref[0] = o * lax.rsqrt(ms + RMS_EPS) * sw_ref[...] * (1.0 - lam_init)


def _diff_attention(qp, kv, lam_p, subln_w, layer, lam_init):
    b, t, w = qp.shape
    tq = ATT_T
    return pl.pallas_call(
        functools.partial(_attn_body, lam_init=lam_init),
        grid=(b, DA_HEADS, t // tq),
        in_specs=[pl.BlockSpec((1, tq, DA_V_DIM), lambda bi, h, i: (bi, i, h)),
                  pl.BlockSpec((1, t, DA_V_DIM), lambda bi, h, i: (bi, 0, h)),
                  pl.BlockSpec((1, t, DA_V_DIM), lambda bi, h, i: (bi, 0, DA_HEADS + h)),
                  pl.BlockSpec((None, 4, DA_HEAD_DIM), lambda bi, h, i: (layer, 0, 0)),
                  pl.BlockSpec((None, 1, DA_V_DIM), lambda bi, h, i: (layer, 0, 0))],
        out_specs=pl.BlockSpec((1, tq, DA_V_DIM), lambda bi, h, i: (bi, i, h)),
        out_shape=jax.ShapeDtypeStruct((b, t, w), F32),
        scratch_shapes=[pltpu.VMEM((2, tq, 1), F32), pltpu.VMEM((2, tq, 1), F32),
                        pltpu.VMEM((2, tq, DA_V_DIM), F32)],
        compiler_params=_cparams("parallel", "parallel", "parallel"),
        name="diff_attn",
    )(qp, kv, kv, lam_p, subln_w)


def _router_body(h_ref, w_ref, info_ref, gate_ref, cnt_ref, carry_ref):
    tm = h_ref.shape[0]

    @pl.when(pl.program_id(0) == 0)
    def _():
        carry_ref[...] = jnp.zeros_like(carry_ref)

    logits = jnp.dot(h_ref[...], w_ref[...], preferred_element_type=F32, precision=lax.Precision.HIGHEST)
    lane = lax.broadcasted_iota(I32, logits.shape, 1)
    big = jnp.int32(LANES)

    def first_max(vals):
        mx = jnp.max(vals, axis=-1, keepdims=True)
        idx = jnp.min(jnp.where(vals == mx, lane, big), axis=-1, keepdims=True)
        return mx, idx

    gl = jnp.where(lane < N_GROUPS, logits, -jnp.inf)
    gmax, gidx = first_max(gl)
    p_group = 1.0 / jnp.sum(jnp.exp(gl - gmax), axis=-1, keepdims=True)
    lo = N_GROUPS + gidx * EXPERTS_PER_GROUP
    el = jnp.where((lane >= lo) & (lane < lo + EXPERTS_PER_GROUP), logits, -jnp.inf)
    m1, i1 = first_max(el)
    m2, i2 = first_max(jnp.where(lane == i1, -jnp.inf, el))
    e2 = jnp.exp(m2 - m1)
    gate1 = p_group / (1.0 + e2)
    gate2 = p_group * e2 / (1.0 + e2)

    oh = ((lane == i1) | (lane == i2)).astype(F32)
    ri = lax.broadcasted_iota(I32, (tm, tm), 0)
    ci = lax.broadcasted_iota(I32, (tm, tm), 1)
    before = jnp.dot((ci < ri).astype(F32), oh, preferred_element_type=F32) + carry_ref[0:1, :]
    rank1 = jnp.sum(jnp.where(lane == i1, before, 0.0), axis=-1, keepdims=True).astype(I32)
    rank2 = jnp.sum(jnp.where(lane == i2, before, 0.0), axis=-1, keepdims=True).astype(I32)
    total = carry_ref[0:1, :] + jnp.sum(oh, axis=0, keepdims=True)
    carry_ref[0:1, :] = total

    info_ref[...] = jnp.where(lane == 0, i1 - N_GROUPS,
                              jnp.where(lane == 1, i2 - N_GROUPS,
                                        jnp.where(lane == 2, rank1, jnp.where(lane == 3, rank2, 0))))
    gate_ref[...] = jnp.where(lane == 0, gate1, jnp.where(lane == 1, gate2, 0.0))
    cnt_ref[...] = jnp.broadcast_to(total, cnt_ref.shape).astype(I32)


def _moe_router(h, w_route):
    m, k = h.shape
    tm = ROUTE_TM
    return pl.pallas_call(
        _router_body,
        grid=(m // tm,),
        in_specs=[pl.BlockSpec((tm, k), lambda i: (i, 0)),
                  pl.BlockSpec((k, LANES), lambda i: (0, 0))],
        out_specs=[pl.BlockSpec((tm, LANES), lambda i: (i, 0)),
                   pl.BlockSpec((tm, LANES), lambda i: (i, 0)),
                   pl.BlockSpec((SUBLANES, LANES), lambda i: (0, 0))],
        out_shape=[jax.ShapeDtypeStruct((m, LANES), I32),
                   jax.ShapeDtypeStruct((m, LANES), F32),
                   jax.ShapeDtypeStruct((SUBLANES, LANES), I32)],
        scratch_shapes=[pltpu.VMEM((SUBLANES, LANES), F32)],
        compiler_params=_cparams("arbitrary"),
        name="moe_router",
    )(h, w_route)


ZERO_ROWS = MOE_TB // 2


def _dispatch_body(pos0_ref, pos1_ref, zlo_ref, zhi_ref, x_ref, xs_hbm, zero_ref, sem, zsem):
    i = pl.program_id(0)
    tm = x_ref.shape[0]

    def zero_pieces(act):
        def per_expert(e, carry):
            lo = zlo_ref[e]
            hi = zhi_ref[e]
            n = hi - lo
            size = ZERO_ROWS
            end = hi
            while size >= SUBLANES:
                take = (n & size) != 0

                @pl.when(take)
                def _(end=end, size=size):
                    off = pl.multiple_of(end - size, SUBLANES)
                    act(pltpu.make_async_copy(zero_ref.at[pl.ds(0, size), :], xs_hbm.at[pl.ds(off, size), :], zsem))

                end = end - jnp.where(take, size, 0)
                size //= 2
            for r in range(SUBLANES - 1):
                @pl.when(r < (n & (SUBLANES - 1)))
                def _(r=r):
                    act(pltpu.make_async_copy(zero_ref.at[pl.ds(0, 1), :], xs_hbm.at[pl.ds(lo + r, 1), :], zsem))
            return carry

        lax.fori_loop(0, N_EXPERTS, per_expert, 0)

        def trailing(p, carry):
            off = pl.multiple_of(p * ZERO_ROWS, ZERO_ROWS)
            act(pltpu.make_async_copy(zero_ref, xs_hbm.at[pl.ds(off, ZERO_ROWS), :], zsem))
            return carry

        lax.fori_loop(zlo_ref[N_EXPERTS] // ZERO_ROWS, xs_hbm.shape[0] // ZERO_ROWS, trailing, 0)

    @pl.when(i == 0)
    def _():
        zero_ref[...] = jnp.zeros_like(zero_ref)
        zero_pieces(lambda cp: cp.start())

    def issue(r, carry):
        t = i * tm + r
        src = x_ref.at[pl.ds(r, 1), :]
        pltpu.make_async_copy(src, xs_hbm.at[pl.ds(pos0_ref[t], 1), :], sem).start()
        pltpu.make_async_copy(src, xs_hbm.at[pl.ds(pos1_ref[t], 1), :], sem).start()
        return carry

    lax.fori_loop(0, tm, issue, 0, unroll=8)
    for _ in range(TOP_K):
        pltpu.make_async_copy(x_ref, xs_hbm.at[pl.ds(0, tm), :], sem).wait()

    @pl.when(i == 0)
    def _():
        zero_pieces(lambda cp: cp.wait())


def _moe_dispatch(x, pos0, pos1, zlo, zhi, n_rows):
    m, d = x.shape
    tm = DISP_TM
    return pl.pallas_call(
        _dispatch_body,
        grid_spec=pltpu.PrefetchScalarGridSpec(
            num_scalar_prefetch=4, grid=(m // tm,),
            in_specs=[pl.BlockSpec((tm, d), lambda i, *_: (i, 0))],
            out_specs=pl.BlockSpec(memory_space=pl.ANY),
            scratch_shapes=[pltpu.VMEM((ZERO_ROWS, d), F32), pltpu.SemaphoreType.DMA(()),
                            pltpu.SemaphoreType.DMA(())]),
        out_shape=jax.ShapeDtypeStruct((n_rows, d), F32),
        compiler_params=_cparams("arbitrary"),
        name="moe_dispatch",
    )(pos0, pos1, zlo, zhi, x)


def _expert_body(eid_ref, nused_ref, x_ref, w13_ref, w2_ref, o_ref):
    ff = w2_ref.shape[0]

    used = pl.program_id(0) < nused_ref[0]

    @pl.when(used)
    def _():
        hcat = jnp.dot(x_ref[...], w13_ref[...], preferred_element_type=F32)
        hid = _silu(hcat[:, :ff]) * hcat[:, ff:]
        o_ref[...] = jnp.dot(hid, w2_ref[...], preferred_element_type=F32)

    @pl.when(jnp.logical_not(used))
    def _():
        o_ref[...] = jnp.zeros_like(o_ref)


def _moe_experts(xs, block_eid, n_used, w13, w2, layer):
    n_rows, d = xs.shape
    tb = MOE_TB
    ff = w2.shape[2]

    def row_block(b, eid, nused):
        return (jnp.minimum(b, nused[0] - 1), 0)

    return pl.pallas_call(
        _expert_body,
        grid_spec=pltpu.PrefetchScalarGridSpec(
            num_scalar_prefetch=2, grid=(n_rows // tb,),
            in_specs=[pl.BlockSpec((tb, d), row_block),
                      pl.BlockSpec((None, None, d, 2 * ff), lambda b, eid, nused: (layer, eid[b], 0, 0)),
                      pl.BlockSpec((None, None, ff, d), lambda b, eid, nused: (layer, eid[b], 0, 0))],
            out_specs=pl.BlockSpec((tb, d), lambda b, eid, nused: (b, 0))),
        out_shape=jax.ShapeDtypeStruct((n_rows, d), F32),
        compiler_params=_cparams("arbitrary"),
        name="moe_experts",
    )(block_eid, n_used, xs, w13, w2)


def _combine_body(pos0_ref, pos1_ref, ys_hbm, h_ref, gate_ref, g_ref, b_ref, o_ref, y0_ref, y1_ref, sem0, sem1):
    i = pl.program_id(0)
    tm = h_ref.shape[0]

    def issue(r, carry):
        t = i * tm + r
        pltpu.make_async_copy(ys_hbm.at[pl.ds(pos0_ref[t], 1), :], y0_ref.at[pl.ds(r, 1), :], sem0).start()
        pltpu.make_async_copy(ys_hbm.at[pl.ds(pos1_ref[t], 1), :], y1_ref.at[pl.ds(r, 1), :], sem1).start()
        return carry

    lax.fori_loop(0, tm, issue, 0, unroll=8)
    pltpu.make_async_copy(ys_hbm.at[pl.ds(0, tm), :], y0_ref, sem0).wait()
    pltpu.make_async_copy(ys_hbm.at[pl.ds(0, tm), :], y1_ref, sem1).wait()
    gates = gate_ref[...]
    ffn = gates[:, 0:1] * y0_ref[...] + gates[:, 1:2] * y1_ref[...]
    o_ref[...] = _layer_norm(ALPHA * h_ref[...] + ffn, g_ref[...], b_ref[...])


def _moe_combine(ys, pos0, pos1, h, gates, g, b, layer):
    m, d = h.shape
    tm = COMB_TM
    row = pl.BlockSpec((None, 1, d), lambda i, *_: (layer, 0, 0))
    return pl.pallas_call(
        _combine_body,
        grid_spec=pltpu.PrefetchScalarGridSpec(
            num_scalar_prefetch=2, grid=(m // tm,),
            in_specs=[pl.BlockSpec(memory_space=pl.ANY),
                      pl.BlockSpec((tm, d), lambda i, *_: (i, 0)),
                      pl.BlockSpec((tm, LANES), lambda i, *_: (i, 0)), row, row],
            out_specs=pl.BlockSpec((tm, d), lambda i, *_: (i, 0)),
            scratch_shapes=[pltpu.VMEM((tm, d), F32), pltpu.VMEM((tm, d), F32),
                            pltpu.SemaphoreType.DMA(()), pltpu.SemaphoreType.DMA(())]),
        out_shape=jax.ShapeDtypeStruct((m, d), F32),
        compiler_params=_cparams("arbitrary"),
        name="moe_combine",
    )(pos0, pos1, ys, h, gates, g, b)


def _moe_layer(h, w_group, w_expert, w13, w2, ln_g, ln_b, layer):
    n, d = h.shape
    tb = MOE_TB
    w_route = jnp.pad(jnp.concatenate([w_group, w_expert], axis=1), ((0, 0), (0, LANES - N_GROUPS - N_EXPERTS)))
    info, gates, cnt = _moe_router(h, w_route)
    counts = cnt[0, N_GROUPS:N_GROUPS + N_EXPERTS]
    padded = (counts + tb - 1) // tb * tb
    pad_end = jnp.cumsum(padded)
    pad_start = pad_end - padded
    pos0 = pad_start[info[:, 0]] + info[:, 2]
    pos1 = pad_start[info[:, 1]] + info[:, 3]
    n_blocks = (n * TOP_K + N_EXPERTS * (tb - 1) + tb - 1) // tb
    n_used = (pad_end[-1] // tb).astype(I32)
    blk = jnp.minimum(jnp.arange(n_blocks, dtype=I32), n_used - 1)
    block_eid = jnp.minimum(jnp.sum(pad_end[None, :] <= (blk * tb)[:, None], axis=1), N_EXPERTS - 1).astype(I32)
    zlo = jnp.concatenate([pad_start + counts, pad_end[-1:]])
    xs = _moe_dispatch(h, pos0, pos1, zlo, pad_end, n_blocks * tb)
    ys = _moe_experts(xs, block_eid, n_used.reshape(1), w13, w2, layer)
    return _moe_combine(ys, pos0, pos1, h, gates, ln_g, ln_b, layer)


def _lambda_init(layer_idx):
    return 0.8 - 0.6 * math.exp(-0.3 * layer_idx)


def kernel(x, a_w_in, a_conv_w, a_a_log, a_dt_bias, a_norm_w, a_w_out, kv_w, b_w_q, b_lambda, b_subln_w, b_w_out,
           ln_mix_g, ln_mix_b, ln_ffn_g, ln_ffn_b, moe_w_group, moe_w_expert, moe_w13, moe_w2):
    b, t, d = x.shape
    n = b * t
    dn_w = DN_HEADS * DN_HEAD_DIM
    h = x.reshape(n, d)

    def per_layer_rows(p):
        return p.reshape(p.shape[0], 1, p.shape[1])

    a_norm_w, b_subln_w = per_layer_rows(a_norm_w), per_layer_rows(b_subln_w)
    ln_mix_g, ln_mix_b = per_layer_rows(ln_mix_g), per_layer_rows(ln_mix_b)
    ln_ffn_g, ln_ffn_b = per_layer_rows(ln_ffn_g), per_layer_rows(ln_ffn_b)
    kv = None
    for layer in range(DEPTH):
        if layer < N_A_LAYERS:
            proj = _matmul(h, a_w_in, layer, 4 * dn_w, "dn_in_proj").reshape(b, t, 4 * dn_w)
            gb = _dn_gates(h, a_w_in[layer, :, 4 * dn_w:], a_a_log[layer], a_dt_bias[layer]).reshape(b, t, LANES)
            q, k, v = _dn_conv(proj, a_conv_w, layer)
            mix_in = _dn_delta(q, k, v, proj, gb, a_norm_w, layer).reshape(n, dn_w)
            h = _proj_res_ln(mix_in, a_w_out, h, ln_mix_g, ln_mix_b, layer, layer, "dn_out_ln")
        else:
            j = layer - N_A_LAYERS
            if j == 0:
                kv = _matmul(h, kv_w.reshape((1,) + kv_w.shape), 0, kv_w.shape[1], "kv_proj").reshape(b, t, -1)
            qp = _matmul(h, b_w_q, j, b_w_q.shape[2], "q_proj").reshape(b, t, -1)
            mix_in = _diff_attention(qp, kv, b_lambda, b_subln_w, j, _lambda_init(layer)).reshape(n, -1)
            h = _proj_res_ln(mix_in, b_w_out, h, ln_mix_g, ln_mix_b, j, layer, "da_out_ln")
        h = _moe_layer(h, moe_w_group[layer], moe_w_expert[layer], moe_w13, moe_w2, ln_ffn_g, ln_ffn_b, layer)
    return h.reshape(b, t, d)
```

```python
import functools
import math

import jax
import jax.numpy as jnp
from jax import lax
from jax.experimental import pallas as pl
from jax.experimental.pallas import tpu as pltpu

F32 = jnp.float32
I32 = jnp.int32

DEPTH = 4
N_A_LAYERS = DEPTH // 2
DN_HEADS = 8
DN_HEAD_DIM = 128
CONV_K = 4
DN_CHUNK = 64
DA_HEADS = 8
DA_HEAD_DIM = 64
DA_V_DIM = 2 * DA_HEAD_DIM
N_GROUPS = 4
EXPERTS_PER_GROUP = 8
N_EXPERTS = N_GROUPS * EXPERTS_PER_GROUP
TOP_K = 2
ALPHA = (2 * DEPTH) ** 0.25
LN_EPS = 1e-5
RMS_EPS = 1e-6

LANES = 128
SUBLANES = 8
VMEM_LIMIT_BYTES = 56 * 1024 * 1024

MM_TM = 1024
MM_TN = 512
LN_TM = 512
CONV_TT = 256
DN_ROWS = 256
ATT_T = 512
ROUTE_TM = 512
MOE_TB = 256
DISP_TM = 512
COMB_TM = 256


def _cparams(*sem):
    return pltpu.CompilerParams(dimension_semantics=sem, vmem_limit_bytes=VMEM_LIMIT_BYTES)


def _silu(x):
    return x * (1.0 / (1.0 + jnp.exp(-x)))


def _layer_norm(x, g, b):
    mu = jnp.mean(x, axis=-1, keepdims=True)
    xc = x - mu
    var = jnp.mean(xc * xc, axis=-1, keepdims=True)
    return xc * lax.rsqrt(var + LN_EPS) * g + b


def _mm_body(x_ref, w_ref, o_ref):
    o_ref[...] = jnp.dot(x_ref[...], w_ref[...], preferred_element_type=F32)


def _matmul(x, w, layer, n_out, name):
    m, k = x.shape
    return pl.pallas_call(
        _mm_body,
        grid=(m // MM_TM, n_out // MM_TN),
        in_specs=[pl.BlockSpec((MM_TM, k), lambda i, j: (i, 0)),
                  pl.BlockSpec((None, k, MM_TN), lambda i, j: (layer, 0, j))],
        out_specs=pl.BlockSpec((MM_TM, MM_TN), lambda i, j: (i, j)),
        out_shape=jax.ShapeDtypeStruct((m, n_out), F32),
        compiler_params=_cparams("parallel", "parallel"),
        name=name,
    )(x, w)


def _gates_body(x_ref, w_ref, alog_ref, dtb_ref, o_ref):
    logit = jnp.dot(x_ref[...], w_ref[...], preferred_element_type=F32)
    lane = lax.broadcasted_iota(I32, logit.shape, 1)
    beta = 1.0 / (1.0 + jnp.exp(-logit))
    sp_in = logit + dtb_ref[...]
    softplus = jnp.maximum(sp_in, 0.0) + jnp.log1p(jnp.exp(-jnp.abs(sp_in)))
    g = -jnp.exp(alog_ref[...]) * softplus
    o_ref[...] = jnp.where(lane < DN_HEADS, beta, g)


def _dn_gates(x, w_small, a_log, dt_bias):
    m, k = x.shape
    pad = LANES - 2 * DN_HEADS
    w = jnp.pad(w_small, ((0, 0), (0, pad)))
    alog = jnp.pad(a_log, (DN_HEADS, pad)).reshape(1, LANES)
    dtb = jnp.pad(dt_bias, (DN_HEADS, pad)).reshape(1, LANES)
    return pl.pallas_call(
        _gates_body,
        grid=(m // MM_TM,),
        in_specs=[pl.BlockSpec((MM_TM, k), lambda i: (i, 0)),
                  pl.BlockSpec((k, LANES), lambda i: (0, 0)),
                  pl.BlockSpec((1, LANES), lambda i: (0, 0)),
                  pl.BlockSpec((1, LANES), lambda i: (0, 0))],
        out_specs=pl.BlockSpec((MM_TM, LANES), lambda i: (i, 0)),
        out_shape=jax.ShapeDtypeStruct((m, LANES), F32),
        compiler_params=_cparams("parallel"),
        name="dn_gates",
    )(x, w, alog, dtb)


def _conv_body(xq_ref, xk_ref, xv_ref, hq_ref, hk_ref, hv_ref, w_ref, q_ref, k_ref, v_ref):
    first = pl.program_id(1) == 0
    width = xq_ref.shape[-1]

    def conv_silu(x_ref, halo_ref, col0):
        halo = jnp.where(first, 0.0, halo_ref[0])
        xe = jnp.concatenate([halo, x_ref[0]], axis=0)
        tt = x_ref.shape[1]
        acc = xe[SUBLANES:SUBLANES + tt] * w_ref[CONV_K - 1:CONV_K, col0:col0 + width]
        for j in range(CONV_K - 1):
            shifted = pltpu.roll(xe, CONV_K - 1 - j, axis=0)[SUBLANES:SUBLANES + tt]
            acc = acc + shifted * w_ref[j:j + 1, col0:col0 + width]
        return _silu(acc)

    def l2n(y, scale):
        outs = []
        for h in range(DN_HEADS):
            yh = y[:, h * DN_HEAD_DIM:(h + 1) * DN_HEAD_DIM]
            ss = jnp.sum(yh * yh, axis=-1, keepdims=True)
            outs.append(yh * (lax.rsqrt(ss + RMS_EPS) * scale))
        return jnp.concatenate(outs, axis=-1)

    q_ref[0] = l2n(conv_silu(xq_ref, hq_ref, 0), DN_HEAD_DIM ** -0.5)
    k_ref[0] = l2n(conv_silu(xk_ref, hk_ref, width), 1.0)
    v_ref[0] = conv_silu(xv_ref, hv_ref, 2 * width)


def _dn_conv(proj, conv_w, layer):
    b, t, _ = proj.shape
    w = DN_HEADS * DN_HEAD_DIM
    tt = CONV_TT
    hb = tt // SUBLANES

    def xspec(c):
        return pl.BlockSpec((1, tt, w), lambda bi, i: (bi, i, c))

    def hspec(c):
        return pl.BlockSpec((1, SUBLANES, w), lambda bi, i: (bi, jnp.maximum(i * hb - 1, 0), c))

    out = jax.ShapeDtypeStruct((b, t, w), F32)
    ospec = pl.BlockSpec((1, tt, w), lambda bi, i: (bi, i, 0))
    return pl.pallas_call(
        _conv_body,
        grid=(b, t // tt),
        in_specs=[xspec(0), xspec(1), xspec(2), hspec(0), hspec(1), hspec(2),
                  pl.BlockSpec((None, CONV_K, 3 * w), lambda bi, i: (layer, 0, 0))],
        out_specs=[ospec, ospec, ospec],
        out_shape=[out, out, out],
        compiler_params=_cparams("parallel", "parallel"),
        name="dn_conv",
    )(proj, proj, proj, proj, proj, proj, conv_w)


def _split3(x):
    hi = x.astype(jnp.bfloat16).astype(F32)
    r = x - hi
    mid = r.astype(jnp.bfloat16).astype(F32)
    return hi, mid, r - mid


def _nt(a, b):
    return lax.dot_general(a, b, (((1,), (1,)), ((), ())), preferred_element_type=F32)


def _tn(a, b):
    return lax.dot_general(a, b, (((0,), (0,)), ((), ())), preferred_element_type=F32)


def _delta_body(u_ref, w_ref, qd_ref, kd_ref, attn_ref, egl_ref, z_ref, nw_ref, o_ref, state_ref):
    c = DN_CHUNK
    d = DN_HEAD_DIM

    @pl.when(pl.program_id(0) == 0)
    def _():
        state_ref[...] = jnp.zeros_like(state_ref)

    nw = nw_ref[...]
    nb = u_ref.shape[0]
    chains = [(bi, h) for bi in range(nb) for h in range(DN_HEADS)]

    def chunk(ic, carry):
        rows = pl.ds(pl.multiple_of(ic * c, c), c)
        s_old, ws_qs = {}, {}
        for bi, h in chains:
            cols = slice(h * d, (h + 1) * d)
            s_old[bi, h] = state_ref[bi * DN_HEADS + h]
            lhs = jnp.concatenate([w_ref[bi, rows, cols], qd_ref[bi, rows, cols]], axis=0)
            ws_qs[bi, h] = jnp.dot(lhs, s_old[bi, h], preferred_element_type=F32)
        v_new = {}
        for bi, h in chains:
            cols = slice(h * d, (h + 1) * d)
            v_new[bi, h] = u_ref[bi, rows, cols] - ws_qs[bi, h][:c]
        for bi, h in chains:
            cols = slice(h * d, (h + 1) * d)
            acols = slice(h * c, (h + 1) * c)
            vn = v_new[bi, h]
            o = ws_qs[bi, h][c:] + jnp.dot(attn_ref[bi, rows, acols], vn, preferred_element_type=F32)
            gl = egl_ref[bi, pl.ds(ic * DN_HEADS + h, 1), :]
            state_ref[bi * DN_HEADS + h] = s_old[bi, h] * gl + _tn(kd_ref[bi, rows, cols], vn)
            ms = jnp.mean(o * o, axis=-1, keepdims=True)
            o_ref[bi, rows, cols] = o * lax.rsqrt(ms + RMS_EPS) * nw * _silu(z_ref[bi, rows, cols])
        return carry

    lax.fori_loop(0, u_ref.shape[1] // c, chunk, 0)


def _dn_prep_body(q_ref, k_ref, v_ref, gb_ref, u_ref, w_ref, qd_ref, kd_ref, attn_ref, egl_ref):
    c = DN_CHUNK
    d = DN_HEAD_DIM
    ri = lax.broadcasted_iota(I32, (c, c), 0)
    ci = lax.broadcasted_iota(I32, (c, c), 1)
    causal = ri >= ci
    strict = ri > ci
    tri = causal.astype(F32)
    heads = range(DN_HEADS)

    def chunk(ic, carry):
        rows = pl.ds(pl.multiple_of(ic * c, c), c)
        gbt = gb_ref[0, rows, :]
        g_hi, g_mid, g_lo = _split3(gbt)
        gcol = (jnp.dot(tri, g_hi, preferred_element_type=F32)
                + jnp.dot(tri, g_mid, preferred_element_type=F32)
                + jnp.dot(tri, g_lo, preferred_element_type=F32))
        grow = gcol.T
        g_last_col = grow[DN_HEADS:2 * DN_HEADS, c - 1:c]
        egl_ref[0, pl.ds(pl.multiple_of(ic * DN_HEADS, DN_HEADS), DN_HEADS), :] = jnp.broadcast_to(
            jnp.exp(g_last_col), (DN_HEADS, d))
        a_mat, x, p = {}, {}, {}
        for h in heads:
            cols = slice(h * d, (h + 1) * d)
            qh = q_ref[0, rows, cols]
            kh = k_ref[0, rows, cols]
            beta_c = gbt[:, h:h + 1]
            gc_c = gcol[:, DN_HEADS + h:DN_HEADS + h + 1]
            gc_r = grow[DN_HEADS + h:DN_HEADS + h + 1, :]
            decay = jnp.exp(jnp.where(causal, gc_c - gc_r, -jnp.inf))
            eg_c = jnp.exp(gc_c)
            k_beta = kh * beta_c
            a_mat[h] = jnp.where(strict, _nt(k_beta, kh) * decay, 0.0)
            attn_ref[0, rows, h * c:(h + 1) * c] = jnp.where(causal, _nt(qh, kh) * decay, 0.0)
            qd_ref[0, rows, cols] = qh * eg_c
            kd_ref[0, rows, cols] = kh * jnp.exp(gc_r[:, c - 1:c] - gc_c)
            x[h] = jnp.concatenate([v_ref[0, rows, cols] * beta_c, k_beta * eg_c], axis=-1)
        for h in heads:
            x[h] = x[h] - jnp.dot(a_mat[h], x[h], preferred_element_type=F32)
            p[h] = a_mat[h]
        for _ in range(int(math.log2(c)) - 1):
            for h in heads:
                p[h] = jnp.dot(p[h], p[h], preferred_element_type=F32)
            for h in heads:
                x[h] = x[h] + jnp.dot(p[h], x[h], preferred_element_type=F32)
        for h in heads:
            cols = slice(h * d, (h + 1) * d)
            u_ref[0, rows, cols] = x[h][:, :d]
            w_ref[0, rows, cols] = x[h][:, d:]
        return carry

    lax.fori_loop(0, q_ref.shape[1] // c, chunk, 0)


def _dn_delta(q, k, v, proj, gb, norm_w, layer):
    b, t, w = q.shape
    rs = DN_ROWS
    cps = rs // DN_CHUNK
    aw = DN_HEADS * DN_CHUNK
    spec = pl.BlockSpec((1, rs, w), lambda bi, i: (bi, i, 0))
    full = jax.ShapeDtypeStruct((b, t, w), F32)
    u, wy, qd, kd, attn, egl = pl.pallas_call(
        _dn_prep_body,
        grid=(b, t // rs),
        in_specs=[spec, spec, spec, pl.BlockSpec((1, rs, LANES), lambda bi, i: (bi, i, 0))],
        out_specs=[spec, spec, spec, spec,
                   pl.BlockSpec((1, rs, aw), lambda bi, i: (bi, i, 0)),
                   pl.BlockSpec((1, cps * DN_HEADS, DN_HEAD_DIM), lambda bi, i: (bi, i, 0))],
        out_shape=[full, full, full, full,
                   jax.ShapeDtypeStruct((b, t, aw), F32),
                   jax.ShapeDtypeStruct((b, t // DN_CHUNK * DN_HEADS, DN_HEAD_DIM), F32)],
        compiler_params=_cparams("parallel", "parallel"),
        name="dn_prep",
    )(q, k, v, gb)
    bspec = pl.BlockSpec((b, rs, w), lambda i: (0, i, 0))
    return pl.pallas_call(
        _delta_body,
        grid=(t // rs,),
        in_specs=[bspec, bspec, bspec, bspec,
                  pl.BlockSpec((b, rs, aw), lambda i: (0, i, 0)),
                  pl.BlockSpec((b, cps * DN_HEADS, DN_HEAD_DIM), lambda i: (0, i, 0)),
                  pl.BlockSpec((b, rs, w), lambda i: (0, i, 3)),
                  pl.BlockSpec((None, 1, DN_HEAD_DIM), lambda i: (layer, 0, 0))],
        out_specs=bspec,
        out_shape=full,
        scratch_shapes=[pltpu.VMEM((b * DN_HEADS, DN_HEAD_DIM, DN_HEAD_DIM), F32)],
        compiler_params=_cparams("arbitrary"),
        name="dn_scan",
    )(u, wy, qd, kd, attn, egl, proj, norm_w)


def _proj_ln_body(a_ref, w_ref, h_ref, g_ref, b_ref, o_ref):
    y = jnp.dot(a_ref[...], w_ref[...], preferred_element_type=F32)
    o_ref[...] = _layer_norm(ALPHA * h_ref[...] + y, g_ref[...], b_ref[...])


def _proj_res_ln(a, w, h, g, b, w_layer, ln_layer, name):
    m, k = a.shape
    n = h.shape[1]
    row = pl.BlockSpec((None, 1, n), lambda i: (ln_layer, 0, 0))
    return pl.pallas_call(
        _proj_ln_body,
        grid=(m // LN_TM,),
        in_specs=[pl.BlockSpec((LN_TM, k), lambda i: (i, 0)),
                  pl.BlockSpec((None, k, n), lambda i: (w_layer, 0, 0)),
                  pl.BlockSpec((LN_TM, n), lambda i: (i, 0)), row, row],
        out_specs=pl.BlockSpec((LN_TM, n), lambda i: (i, 0)),
        out_shape=jax.ShapeDtypeStruct((m, n), F32),
        compiler_params=_cparams("parallel"),
        name=name,
    )(a, w, h, g, b)


def _mm_nt_body(w_ref, x_ref, o_ref):
    o_ref[...] = _nt(w_ref[...], x_ref[...])


def _matmul_nt(w_t, x, name):
    n, k = w_t.shape
    m = x.shape[0]
    return pl.pallas_call(
        _mm_nt_body,
        grid=(n // MM_TN, m // MM_TM),
        in_specs=[pl.BlockSpec((MM_TN, k), lambda j, i: (j, 0)),
                  pl.BlockSpec((MM_TM, k), lambda j, i: (i, 0))],
        out_specs=pl.BlockSpec((MM_TN, MM_TM), lambda j, i: (j, i)),
        out_shape=jax.ShapeDtypeStruct((n, m), F32),
        compiler_params=_cparams("parallel", "parallel"),
        name=name,
    )(w_t, x)


def _attn_body(q_ref, k_ref, vt_ref, lam_ref, sw_ref, o_ref, m_ref, l_ref, acc_ref, *, lam_init):
    i = pl.program_id(2)
    t = ATT_T
    q = q_ref[0] * (DA_HEAD_DIM ** -0.5 * math.log2(math.e))
    lane = lax.broadcasted_iota(I32, q.shape, 1)
    q_maps = (jnp.where(lane < DA_HEAD_DIM, q, 0.0), jnp.where(lane >= DA_HEAD_DIM, q, 0.0))
    m_ref[...] = jnp.full_like(m_ref, -jnp.inf)
    l_ref[...] = jnp.zeros_like(l_ref)
    acc_ref[...] = jnp.zeros_like(acc_ref)

    def tile(j, masked):
        rows = pl.ds(pl.multiple_of(j * t, t), t)
        k = k_ref[0, rows, :]
        vt = vt_ref[:, rows]
        for s_idx in range(2):
            st = _nt(k, q_maps[s_idx])
            if masked:
                ri = lax.broadcasted_iota(I32, st.shape, 0)
                ci = lax.broadcasted_iota(I32, st.shape, 1)
                st = jnp.where(ri <= ci, st, -jnp.inf)
            m_old = m_ref[s_idx]
            m_new = jnp.maximum(m_old, jnp.max(st, axis=0, keepdims=True))
            p = jnp.exp2(st - m_new)
            corr = jnp.exp2(m_old - m_new)
            l_ref[s_idx] = corr * l_ref[s_idx] + jnp.sum(p, axis=0, keepdims=True)
            acc_ref[s_idx] = corr * acc_ref[s_idx] + jnp.dot(vt, p, preferred_element_type=F32)
            m_ref[s_idx] = m_new

    def full_tile(j, carry):
        tile(j, False)
        return carry

    lax.fori_loop(0, i, full_tile, 0)
    tile(i, True)

    lp = lam_ref[...]
    lam = (jnp.exp(jnp.sum(lp[0:1] * lp[1:2], axis=-1, keepdims=True))
           - jnp.exp(jnp.sum(lp[2:3] * lp[3:4], axis=-1, keepdims=True)) + lam_init)
    ot = acc_ref[0] / l_ref[0] - lam * (acc_ref[1] / l_ref[1])
    ms = jnp.mean(ot * ot, axis=0, keepdims=True)
    ot = ot * lax.rsqrt(ms + RMS_EPS) * sw_ref[...] * (1.0 - lam_init)
    o_ref[0] = ot.T


def _diff_attention(qp, kp, vt, lam_p, subln_w, layer, lam_init):
    b, t, w = qp.shape
    tq = ATT_T
    return pl.pallas_call(
        functools.partial(_attn_body, lam_init=lam_init),
        grid=(b, DA_HEADS, t // tq),
        in_specs=[pl.BlockSpec((1, tq, DA_V_DIM), lambda bi, h, i: (bi, i, h)),
                  pl.BlockSpec((1, t, DA_V_DIM), lambda bi, h, i: (bi, 0, h)),
                  pl.BlockSpec((DA_V_DIM, t), lambda bi, h, i: (h, bi)),
                  pl.BlockSpec((None, 4, DA_HEAD_DIM), lambda bi, h, i: (layer, 0, 0)),
                  pl.BlockSpec((None, DA_V_DIM, 1), lambda bi, h, i: (layer, 0, 0))],
        out_specs=pl.BlockSpec((1, tq, DA_V_DIM), lambda bi, h, i: (bi, i, h)),
        out_shape=jax.ShapeDtypeStruct((b, t, w), F32),
        scratch_shapes=[pltpu.VMEM((2, 1, tq), F32), pltpu.VMEM((2, 1, tq), F32),
                        pltpu.VMEM((2, DA_V_DIM, tq), F32)],
        compiler_params=_cparams("parallel", "parallel", "parallel"),
        name="diff_attn",
    )(qp, kp, vt, lam_p, subln_w)


def _router_body(h_ref, w_ref, info_ref, gate_ref, cnt_ref, carry_ref):
    tm = h_ref.shape[0]

    @pl.when(pl.program_id(0) == 0)
    def _():
        carry_ref[...] = jnp.zeros_like(carry_ref)

    logits = jnp.dot(h_ref[...], w_ref[...], preferred_element_type=F32, precision=lax.Precision.HIGHEST)
    lane = lax.broadcasted_iota(I32, logits.shape, 1)
    big = jnp.int32(LANES)

    def first_max(vals):
        mx = jnp.max(vals, axis=-1, keepdims=True)
        idx = jnp.min(jnp.where(vals == mx, lane, big), axis=-1, keepdims=True)
        return mx, idx

    gl = jnp.where(lane < N_GROUPS, logits, -jnp.inf)
    gmax, gidx = first_max(gl)
    p_group = 1.0 / jnp.sum(jnp.exp(gl - gmax), axis=-1, keepdims=True)
    lo = N_GROUPS + gidx * EXPERTS_PER_GROUP
    el = jnp.where((lane >= lo) & (lane < lo + EXPERTS_PER_GROUP), logits, -jnp.inf)
    m1, i1 = first_max(el)
    m2, i2 = first_max(jnp.where(lane == i1, -jnp.inf, el))
    e2 = jnp.exp(m2 - m1)
    gate1 = p_group / (1.0 + e2)
    gate2 = p_group * e2 / (1.0 + e2)

    oh = ((lane == i1) | (lane == i2)).astype(F32)
    ri = lax.broadcasted_iota(I32, (tm, tm), 0)
    ci = lax.broadcasted_iota(I32, (tm, tm), 1)
    before = jnp.dot((ci < ri).astype(F32), oh, preferred_element_type=F32) + carry_ref[0:1, :]
    rank1 = jnp.sum(jnp.where(lane == i1, before, 0.0), axis=-1, keepdims=True).astype(I32)
    rank2 = jnp.sum(jnp.where(lane == i2, before, 0.0), axis=-1, keepdims=True).astype(I32)
    total = carry_ref[0:1, :] + jnp.sum(oh, axis=0, keepdims=True)
    carry_ref[0:1, :] = total

    info = jnp.where(lane == 0, i1 - N_GROUPS,
                     jnp.where(lane == 1, i2 - N_GROUPS,
                               jnp.where(lane == 2, rank1, jnp.where(lane == 3, rank2, 0))))
    info_ref[...] = info.T[:SUBLANES]
    gate_ref[...] = jnp.where(lane == 0, gate1, jnp.where(lane == 1, gate2, 0.0))
    cnt_ref[...] = jnp.broadcast_to(total, cnt_ref.shape).astype(I32)


def _moe_router(h, w_route):
    m, k = h.shape
    tm = ROUTE_TM
    return pl.pallas_call(
        _router_body,
        grid=(m // tm,),
        in_specs=[pl.BlockSpec((tm, k), lambda i: (i, 0)),
                  pl.BlockSpec((k, LANES), lambda i: (0, 0))],
        out_specs=[pl.BlockSpec((SUBLANES, tm), lambda i: (0, i)),
                   pl.BlockSpec((tm, LANES), lambda i: (i, 0)),
                   pl.BlockSpec((SUBLANES, LANES), lambda i: (0, 0))],
        out_shape=[jax.ShapeDtypeStruct((SUBLANES, m), I32),
                   jax.ShapeDtypeStruct((m, LANES), F32),
                   jax.ShapeDtypeStruct((SUBLANES, LANES), I32)],
        scratch_shapes=[pltpu.VMEM((SUBLANES, LANES), F32)],
        compiler_params=_cparams("arbitrary"),
        name="moe_router",
    )(h, w_route)


ZERO_ROWS = MOE_TB // 2


def _dispatch_body(pos0_ref, pos1_ref, zlo_ref, zhi_ref, x_ref, xs_hbm, zero_ref, sem, zsem):
    i = pl.program_id(0)
    tm = x_ref.shape[0]

    def zero_pieces(act):
        def per_expert(e, carry):
            lo = zlo_ref[e]
            hi = zhi_ref[e]
            n = hi - lo
            size = ZERO_ROWS
            end = hi
            while size >= SUBLANES:
                take = (n & size) != 0

                @pl.when(take)
                def _(end=end, size=size):
                    off = pl.multiple_of(end - size, SUBLANES)
                    act(pltpu.make_async_copy(zero_ref.at[pl.ds(0, size), :], xs_hbm.at[pl.ds(off, size), :], zsem))

                end = end - jnp.where(take, size, 0)
                size //= 2
            for r in range(SUBLANES - 1):
                @pl.when(r < (n & (SUBLANES - 1)))
                def _(r=r):
                    act(pltpu.make_async_copy(zero_ref.at[pl.ds(0, 1), :], xs_hbm.at[pl.ds(lo + r, 1), :], zsem))
            return carry

        lax.fori_loop(0, N_EXPERTS, per_expert, 0)

        def trailing(p, carry):
            off = pl.multiple_of(p * ZERO_ROWS, ZERO_ROWS)
            act(pltpu.make_async_copy(zero_ref, xs_hbm.at[pl.ds(off, ZERO_ROWS), :], zsem))
            return carry

        lax.fori_loop(zlo_ref[N_EXPERTS] // ZERO_ROWS, xs_hbm.shape[0] // ZERO_ROWS, trailing, 0)

    @pl.when(i == 0)
    def _():
        zero_ref[...] = jnp.zeros_like(zero_ref)
        zero_pieces(lambda cp: cp.start())

    def issue(r, carry):
        t = i * tm + r
        src = x_ref.at[pl.ds(r, 1), :]
        pltpu.make_async_copy(src, xs_hbm.at[pl.ds(pos0_ref[t], 1), :], sem).start()
        pltpu.make_async_copy(src, xs_hbm.at[pl.ds(pos1_ref[t], 1), :], sem).start()
        return carry

    lax.fori_loop(0, tm, issue, 0, unroll=8)
    for _ in range(TOP_K):
        pltpu.make_async_copy(x_ref, xs_hbm.at[pl.ds(0, tm), :], sem).wait()

    @pl.when(i == 0)
    def _():
        zero_pieces(lambda cp: cp.wait())


def _moe_dispatch(x, pos0, pos1, zlo, zhi, n_rows):
    m, d = x.shape
    tm = DISP_TM
    return pl.pallas_call(
        _dispatch_body,
        grid_spec=pltpu.PrefetchScalarGridSpec(
            num_scalar_prefetch=4, grid=(m // tm,),
            in_specs=[pl.BlockSpec((tm, d), lambda i, *_: (i, 0))],
            out_specs=pl.BlockSpec(memory_space=pl.ANY),
            scratch_shapes=[pltpu.VMEM((ZERO_ROWS, d), F32), pltpu.SemaphoreType.DMA(()),
                            pltpu.SemaphoreType.DMA(())]),
        out_shape=jax.ShapeDtypeStruct((n_rows, d), F32),
        compiler_params=_cparams("arbitrary"),
        name="moe_dispatch",
    )(pos0, pos1, zlo, zhi, x)


def _expert_body(eid_ref, nused_ref, x_ref, w13_ref, w2_ref, o_ref):
    ff = w2_ref.shape[0]

    used = pl.program_id(0) < nused_ref[0]

    @pl.when(used)
    def _():
        hcat = jnp.dot(x_ref[...], w13_ref[...], preferred_element_type=F32)
        hid = _silu(hcat[:, :ff]) * hcat[:, ff:]
        o_ref[...] = jnp.dot(hid, w2_ref[...], preferred_element_type=F32)

    @pl.when(jnp.logical_not(used))
    def _():
        o_ref[...] = jnp.zeros_like(o_ref)


def _moe_experts(xs, block_eid, n_used, w13, w2, layer):
    n_rows, d = xs.shape
    tb = MOE_TB
    ff = w2.shape[2]

    def row_block(b, eid, nused):
        return (jnp.minimum(b, nused[0] - 1), 0)

    return pl.pallas_call(
        _expert_body,
        grid_spec=pltpu.PrefetchScalarGridSpec(
            num_scalar_prefetch=2, grid=(n_rows // tb,),
            in_specs=[pl.BlockSpec((tb, d), row_block),
                      pl.BlockSpec((None, None, d, 2 * ff), lambda b, eid, nused: (layer, eid[b], 0, 0)),
                      pl.BlockSpec((None, None, ff, d), lambda b, eid, nused: (layer, eid[b], 0, 0))],
            out_specs=pl.BlockSpec((tb, d), lambda b, eid, nused: (b, 0))),
        out_shape=jax.ShapeDtypeStruct((n_rows, d), F32),
        compiler_params=_cparams("arbitrary"),
        name="moe_experts",
    )(block_eid, n_used, xs, w13, w2)


def _combine_body(pos0_ref, pos1_ref, ys_hbm, h_ref, gate_ref, g_ref, b_ref, o_ref, y0_ref, y1_ref, sem0, sem1):
    i = pl.program_id(0)
    tm = h_ref.shape[0]

    def issue(r, carry):
        t = i * tm + r
        pltpu.make_async_copy(ys_hbm.at[pl.ds(pos0_ref[t], 1), :], y0_ref.at[pl.ds(r, 1), :], sem0).start()
        pltpu.make_async_copy(ys_hbm.at[pl.ds(pos1_ref[t], 1), :], y1_ref.at[pl.ds(r, 1), :], sem1).start()
        return carry

    lax.fori_loop(0, tm, issue, 0, unroll=8)
    pltpu.make_async_copy(ys_hbm.at[pl.ds(0, tm), :], y0_ref, sem0).wait()
    pltpu.make_async_copy(ys_hbm.at[pl.ds(0, tm), :], y1_ref, sem1).wait()
    gates = gate_ref[...]
    ffn = gates[:, 0:1] * y0_ref[...] + gates[:, 1:2] * y1_ref[...]
    o_ref[...] = _layer_norm(ALPHA * h_ref[...] + ffn, g_ref[...], b_ref[...])


def _moe_combine(ys, pos0, pos1, h, gates, g, b, layer):
    m, d = h.shape
    tm = COMB_TM
    row = pl.BlockSpec((None, 1, d), lambda i, *_: (layer, 0, 0))
    return pl.pallas_call(
        _combine_body,
        grid_spec=pltpu.PrefetchScalarGridSpec(
            num_scalar_prefetch=2, grid=(m // tm,),
            in_specs=[pl.BlockSpec(memory_space=pl.ANY),
                      pl.BlockSpec((tm, d), lambda i, *_: (i, 0)),
                      pl.BlockSpec((tm, LANES), lambda i, *_: (i, 0)), row, row],
            out_specs=pl.BlockSpec((tm, d), lambda i, *_: (i, 0)),
            scratch_shapes=[pltpu.VMEM((tm, d), F32), pltpu.VMEM((tm, d), F32),
                            pltpu.SemaphoreType.DMA(()), pltpu.SemaphoreType.DMA(())]),
        out_shape=jax.ShapeDtypeStruct((m, d), F32),
        compiler_params=_cparams("arbitrary"),
        name="moe_combine",
    )(pos0, pos1, ys, h, gates, g, b)


def _moe_layer(h, w_group, w_expert, w13, w2, ln_g, ln_b, layer):
    n, d = h.shape
    tb = MOE_TB
    w_route = jnp.pad(jnp.concatenate([w_group, w_expert], axis=1), ((0, 0), (0, LANES - N_GROUPS - N_EXPERTS)))
    info, gates, cnt = _moe_router(h, w_route)
    counts = cnt[0, N_GROUPS:N_GROUPS + N_EXPERTS]
    padded = (counts + tb - 1) // tb * tb
    pad_end = jnp.cumsum(padded)
    pad_start = pad_end - padded
    pos0 = pad_start[info[0]] + info[2]
    pos1 = pad_start[info[1]] + info[3]
    n_blocks = (n * TOP_K + N_EXPERTS * (tb - 1) + tb - 1) // tb
    n_used = (pad_end[-1] // tb).astype(I32)
    blk = jnp.minimum(jnp.arange(n_blocks, dtype=I32), n_used - 1)
    block_eid = jnp.minimum(jnp.sum(pad_end[None, :] <= (blk * tb)[:, None], axis=1), N_EXPERTS - 1).astype(I32)
    zlo = jnp.concatenate([pad_start + counts, pad_end[-1:]])
    xs = _moe_dispatch(h, pos0, pos1, zlo, pad_end, n_blocks * tb)
    ys = _moe_experts(xs, block_eid, n_used.reshape(1), w13, w2, layer)
    return _moe_combine(ys, pos0, pos1, h, gates, ln_g, ln_b, layer)


def _lambda_init(layer_idx):
    return 0.8 - 0.6 * math.exp(-0.3 * layer_idx)


def kernel(x, a_w_in, a_conv_w, a_a_log, a_dt_bias, a_norm_w, a_w_out, kv_w, b_w_q, b_lambda, b_subln_w, b_w_out,
           ln_mix_g, ln_mix_b, ln_ffn_g, ln_ffn_b, moe_w_group, moe_w_expert, moe_w13, moe_w2):
    b, t, d = x.shape
    n = b * t
    dn_w = DN_HEADS * DN_HEAD_DIM
    h = x.reshape(n, d)

    def per_layer_rows(p):
        return p.reshape(p.shape[0], 1, p.shape[1])

    a_norm_w = per_layer_rows(a_norm_w)
    b_subln_w = b_subln_w.reshape(b_subln_w.shape + (1,))
    ln_mix_g, ln_mix_b = per_layer_rows(ln_mix_g), per_layer_rows(ln_mix_b)
    ln_ffn_g, ln_ffn_b = per_layer_rows(ln_ffn_g), per_layer_rows(ln_ffn_b)
    da_w = DA_HEADS * DA_V_DIM
    kp = vt = None
    for layer in range(DEPTH):
        if layer < N_A_LAYERS:
            proj = _matmul(h, a_w_in, layer, 4 * dn_w, "dn_in_proj").reshape(b, t, 4 * dn_w)
            gb = _dn_gates(h, a_w_in[layer, :, 4 * dn_w:], a_a_log[layer], a_dt_bias[layer]).reshape(b, t, LANES)
            q, k, v = _dn_conv(proj, a_conv_w, layer)
            mix_in = _dn_delta(q, k, v, proj, gb, a_norm_w, layer).reshape(n, dn_w)
            h = _proj_res_ln(mix_in, a_w_out, h, ln_mix_g, ln_mix_b, layer, layer, "dn_out_ln")
        else:
            j = layer - N_A_LAYERS
            if j == 0:
                kp = _matmul(h, kv_w.reshape((1,) + kv_w.shape), 0, da_w, "k_proj").reshape(b, t, da_w)
                vt = _matmul_nt(kv_w[:, da_w:].T, h, "v_proj")
            qp = _matmul(h, b_w_q, j, da_w, "q_proj").reshape(b, t, da_w)
            mix_in = _diff_attention(qp, kp, vt, b_lambda, b_subln_w, j, _lambda_init(layer)).reshape(n, da_w)
            h = _proj_res_ln(mix_in, b_w_out, h, ln_mix_g, ln_mix_b, j, layer, "da_out_ln")
        h = _moe_layer(h, moe_w_group[layer], moe_w_expert[layer], moe_w13, moe_w2, ln_ffn_g, ln_ffn_b, layer)
    return h.reshape(b, t, d)
```

```python
import functools
import math

import jax
import jax.numpy as jnp
from jax import lax
from jax.experimental import pallas as pl
from jax.experimental.pallas import tpu as pltpu

F32 = jnp.float32
I32 = jnp.int32

DEPTH = 4
N_A_LAYERS = DEPTH // 2
DN_HEADS = 8
DN_HEAD_DIM = 128
CONV_K = 4
DN_CHUNK = 64
DA_HEADS = 8
DA_HEAD_DIM = 64
DA_V_DIM = 2 * DA_HEAD_DIM
N_GROUPS = 4
EXPERTS_PER_GROUP = 8
N_EXPERTS = N_GROUPS * EXPERTS_PER_GROUP
TOP_K = 2
ALPHA = (2 * DEPTH) ** 0.25
LN_EPS = 1e-5
RMS_EPS = 1e-6

LANES = 128
SUBLANES = 8
VMEM_LIMIT_BYTES = 56 * 1024 * 1024

MM_TM = 1024
MM_TN = 512
LN_TM = 512
CONV_TT = 256
DN_ROWS = 256
ATT_T = 512
ROUTE_TM = 512
MOE_TB = 256
DISP_TM = 512
COMB_TM = 256


def _cparams(*sem):
    return pltpu.CompilerParams(dimension_semantics=sem, vmem_limit_bytes=VMEM_LIMIT_BYTES)


def _silu(x):
    return x * (1.0 / (1.0 + jnp.exp(-x)))


def _layer_norm(x, g, b):
    mu = jnp.mean(x, axis=-1, keepdims=True)
    xc = x - mu
    var = jnp.mean(xc * xc, axis=-1, keepdims=True)
    return xc * lax.rsqrt(var + LN_EPS) * g + b


def _mm_body(x_ref, w_ref, o_ref):
    o_ref[...] = jnp.dot(x_ref[...], w_ref[...], preferred_element_type=F32)


def _matmul(x, w, layer, n_out, name):
    m, k = x.shape
    return pl.pallas_call(
        _mm_body,
        grid=(m // MM_TM, n_out // MM_TN),
        in_specs=[pl.BlockSpec((MM_TM, k), lambda i, j: (i, 0)),
                  pl.BlockSpec((None, k, MM_TN), lambda i, j: (layer, 0, j))],
        out_specs=pl.BlockSpec((MM_TM, MM_TN), lambda i, j: (i, j)),
        out_shape=jax.ShapeDtypeStruct((m, n_out), F32),
        compiler_params=_cparams("parallel", "parallel"),
        name=name,
    )(x, w)


def _gates_body(x_ref, w_ref, alog_ref, dtb_ref, o_ref):
    logit = jnp.dot(x_ref[...], w_ref[...], preferred_element_type=F32)
    lane = lax.broadcasted_iota(I32, logit.shape, 1)
    beta = 1.0 / (1.0 + jnp.exp(-logit))
    sp_in = logit + dtb_ref[...]
    softplus = jnp.maximum(sp_in, 0.0) + jnp.log1p(jnp.exp(-jnp.abs(sp_in)))
    g = -jnp.exp(alog_ref[...]) * softplus
    o_ref[...] = jnp.where(lane < DN_HEADS, beta, g)


def _dn_gates(x, w_small, a_log, dt_bias):
    m, k = x.shape
    pad = LANES - 2 * DN_HEADS
    w = jnp.pad(w_small, ((0, 0), (0, pad)))
    alog = jnp.pad(a_log, (DN_HEADS, pad)).reshape(1, LANES)
    dtb = jnp.pad(dt_bias, (DN_HEADS, pad)).reshape(1, LANES)
    return pl.pallas_call(
        _gates_body,
        grid=(m // MM_TM,),
        in_specs=[pl.BlockSpec((MM_TM, k), lambda i: (i, 0)),
                  pl.BlockSpec((k, LANES), lambda i: (0, 0)),
                  pl.BlockSpec((1, LANES), lambda i: (0, 0)),
                  pl.BlockSpec((1, LANES), lambda i: (0, 0))],
        out_specs=pl.BlockSpec((MM_TM, LANES), lambda i: (i, 0)),
        out_shape=jax.ShapeDtypeStruct((m, LANES), F32),
        compiler_params=_cparams("parallel"),
        name="dn_gates",
    )(x, w, alog, dtb)


def _conv_body(xq_ref, xk_ref, xv_ref, hq_ref, hk_ref, hv_ref, w_ref, q_ref, k_ref, v_ref):
    first = pl.program_id(1) == 0
    width = xq_ref.shape[-1]

    def conv_silu(x_ref, halo_ref, col0):
        halo = jnp.where(first, 0.0, halo_ref[0])
        xe = jnp.concatenate([halo, x_ref[0]], axis=0)
        tt = x_ref.shape[1]
        acc = xe[SUBLANES:SUBLANES + tt] * w_ref[CONV_K - 1:CONV_K, col0:col0 + width]
        for j in range(CONV_K - 1):
            shifted = pltpu.roll(xe, CONV_K - 1 - j, axis=0)[SUBLANES:SUBLANES + tt]
            acc = acc + shifted * w_ref[j:j + 1, col0:col0 + width]
        return _silu(acc)

    def l2n(y, scale):
        outs = []
        for h in range(DN_HEADS):
            yh = y[:, h * DN_HEAD_DIM:(h + 1) * DN_HEAD_DIM]
            ss = jnp.sum(yh * yh, axis=-1, keepdims=True)
            outs.append(yh * (lax.rsqrt(ss + RMS_EPS) * scale))
        return jnp.concatenate(outs, axis=-1)

    q_ref[0] = l2n(conv_silu(xq_ref, hq_ref, 0), DN_HEAD_DIM ** -0.5)
    k_ref[0] = l2n(conv_silu(xk_ref, hk_ref, width), 1.0)
    v_ref[0] = conv_silu(xv_ref, hv_ref, 2 * width)


def _dn_conv(proj, conv_w, layer):
    b, t, _ = proj.shape
    w = DN_HEADS * DN_HEAD_DIM
    tt = CONV_TT
    hb = tt // SUBLANES

    def xspec(c):
        return pl.BlockSpec((1, tt, w), lambda bi, i: (bi, i, c))

    def hspec(c):
        return pl.BlockSpec((1, SUBLANES, w), lambda bi, i: (bi, jnp.maximum(i * hb - 1, 0), c))

    out = jax.ShapeDtypeStruct((b, t, w), F32)
    ospec = pl.BlockSpec((1, tt, w), lambda bi, i: (bi, i, 0))
    return pl.pallas_call(
        _conv_body,
        grid=(b, t // tt),
        in_specs=[xspec(0), xspec(1), xspec(2), hspec(0), hspec(1), hspec(2),
                  pl.BlockSpec((None, CONV_K, 3 * w), lambda bi, i: (layer, 0, 0))],
        out_specs=[ospec, ospec, ospec],
        out_shape=[out, out, out],
        compiler_params=_cparams("parallel", "parallel"),
        name="dn_conv",
    )(proj, proj, proj, proj, proj, proj, conv_w)


def _split3(x):
    hi = x.astype(jnp.bfloat16).astype(F32)
    r = x - hi
    mid = r.astype(jnp.bfloat16).astype(F32)
    return hi, mid, r - mid


def _nt(a, b):
    return lax.dot_general(a, b, (((1,), (1,)), ((), ())), preferred_element_type=F32)


def _tn(a, b):
    return lax.dot_general(a, b, (((0,), (0,)), ((), ())), preferred_element_type=F32)


def _delta_body(u_ref, w_ref, qd_ref, kd_ref, attn_ref, egl_ref, z_ref, nw_ref, o_ref, state_ref):
    c = DN_CHUNK
    d = DN_HEAD_DIM

    @pl.when(pl.program_id(0) == 0)
    def _():
        state_ref[...] = jnp.zeros_like(state_ref)

    nw = nw_ref[...]
    nb = u_ref.shape[0]
    chains = [(bi, h) for bi in range(nb) for h in range(DN_HEADS)]

    def chunk(ic, carry):
        rows = pl.ds(pl.multiple_of(ic * c, c), c)
        s_old, ws_qs = {}, {}
        for bi, h in chains:
            cols = slice(h * d, (h + 1) * d)
            s_old[bi, h] = state_ref[bi * DN_HEADS + h]
            lhs = jnp.concatenate([w_ref[bi, rows, cols], qd_ref[bi, rows, cols]], axis=0)
            ws_qs[bi, h] = jnp.dot(lhs, s_old[bi, h], preferred_element_type=F32)
        v_new = {}
        for bi, h in chains:
            cols = slice(h * d, (h + 1) * d)
            v_new[bi, h] = u_ref[bi, rows, cols] - ws_qs[bi, h][:c]
        for bi, h in chains:
            cols = slice(h * d, (h + 1) * d)
            acols = slice(h * c, (h + 1) * c)
            vn = v_new[bi, h]
            o = ws_qs[bi, h][c:] + jnp.dot(attn_ref[bi, rows, acols], vn, preferred_element_type=F32)
            gl = egl_ref[bi, pl.ds(ic * DN_HEADS + h, 1), :]
            state_ref[bi * DN_HEADS + h] = s_old[bi, h] * gl + _tn(kd_ref[bi, rows, cols], vn)
            ms = jnp.mean(o * o, axis=-1, keepdims=True)
            o_ref[bi, rows, cols] = o * lax.rsqrt(ms + RMS_EPS) * nw * _silu(z_ref[bi, rows, cols])
        return carry

    lax.fori_loop(0, u_ref.shape[1] // c, chunk, 0)


def _dn_prep_body(q_ref, k_ref, v_ref, gb_ref, u_ref, w_ref, qd_ref, kd_ref, attn_ref, egl_ref):
    c = DN_CHUNK
    d = DN_HEAD_DIM
    ri = lax.broadcasted_iota(I32, (c, c), 0)
    ci = lax.broadcasted_iota(I32, (c, c), 1)
    causal = ri >= ci
    strict = ri > ci
    tri = causal.astype(F32)
    heads = range(DN_HEADS)

    def chunk(ic, carry):
        rows = pl.ds(pl.multiple_of(ic * c, c), c)
        gbt = gb_ref[0, rows, :]
        g_hi, g_mid, g_lo = _split3(gbt)
        gcol = (jnp.dot(tri, g_hi, preferred_element_type=F32)
                + jnp.dot(tri, g_mid, preferred_element_type=F32)
                + jnp.dot(tri, g_lo, preferred_element_type=F32))
        grow = gcol.T
        g_last_col = grow[DN_HEADS:2 * DN_HEADS, c - 1:c]
        egl_ref[0, pl.ds(pl.multiple_of(ic * DN_HEADS, DN_HEADS), DN_HEADS), :] = jnp.broadcast_to(
            jnp.exp(g_last_col), (DN_HEADS, d))
        a_mat, x, p = {}, {}, {}
        for h in heads:
            cols = slice(h * d, (h + 1) * d)
            qh = q_ref[0, rows, cols]
            kh = k_ref[0, rows, cols]
            beta_c = gbt[:, h:h + 1]
            gc_c = gcol[:, DN_HEADS + h:DN_HEADS + h + 1]
            gc_r = grow[DN_HEADS + h:DN_HEADS + h + 1, :]
            decay = jnp.exp(jnp.where(causal, gc_c - gc_r, -jnp.inf))
            eg_c = jnp.exp(gc_c)
            k_beta = kh * beta_c
            a_mat[h] = jnp.where(strict, _nt(k_beta, kh) * decay, 0.0)
            attn_ref[0, rows, h * c:(h + 1) * c] = jnp.where(causal, _nt(qh, kh) * decay, 0.0)
            qd_ref[0, rows, cols] = qh * eg_c
            kd_ref[0, rows, cols] = kh * jnp.exp(gc_r[:, c - 1:c] - gc_c)
            x[h] = jnp.concatenate([v_ref[0, rows, cols] * beta_c, k_beta * eg_c], axis=-1)
        for h in heads:
            x[h] = x[h] - jnp.dot(a_mat[h], x[h], preferred_element_type=F32)
            p[h] = a_mat[h]
        for _ in range(int(math.log2(c)) - 1):
            for h in heads:
                p[h] = jnp.dot(p[h], p[h], preferred_element_type=F32)
            for h in heads:
                x[h] = x[h] + jnp.dot(p[h], x[h], preferred_element_type=F32)
        for h in heads:
            cols = slice(h * d, (h + 1) * d)
            u_ref[0, rows, cols] = x[h][:, :d]
            w_ref[0, rows, cols] = x[h][:, d:]
        return carry

    lax.fori_loop(0, q_ref.shape[1] // c, chunk, 0)


def _dn_delta(q, k, v, proj, gb, norm_w, layer):
    b, t, w = q.shape
    rs = DN_ROWS
    cps = rs // DN_CHUNK
    aw = DN_HEADS * DN_CHUNK
    spec = pl.BlockSpec((1, rs, w), lambda bi, i: (bi, i, 0))
    full = jax.ShapeDtypeStruct((b, t, w), F32)
    u, wy, qd, kd, attn, egl = pl.pallas_call(
        _dn_prep_body,
        grid=(b, t // rs),
        in_specs=[spec, spec, spec, pl.BlockSpec((1, rs, LANES), lambda bi, i: (bi, i, 0))],
        out_specs=[spec, spec, spec, spec,
                   pl.BlockSpec((1, rs, aw), lambda bi, i: (bi, i, 0)),
                   pl.BlockSpec((1, cps * DN_HEADS, DN_HEAD_DIM), lambda bi, i: (bi, i, 0))],
        out_shape=[full, full, full, full,
                   jax.ShapeDtypeStruct((b, t, aw), F32),
                   jax.ShapeDtypeStruct((b, t // DN_CHUNK * DN_HEADS, DN_HEAD_DIM), F32)],
        compiler_params=_cparams("parallel", "parallel"),
        name="dn_prep",
    )(q, k, v, gb)
    bspec = pl.BlockSpec((b, rs, w), lambda i: (0, i, 0))
    return pl.pallas_call(
        _delta_body,
        grid=(t // rs,),
        in_specs=[bspec, bspec, bspec, bspec,
                  pl.BlockSpec((b, rs, aw), lambda i: (0, i, 0)),
                  pl.BlockSpec((b, cps * DN_HEADS, DN_HEAD_DIM), lambda i: (0, i, 0)),
                  pl.BlockSpec((b, rs, w), lambda i: (0, i, 3)),
                  pl.BlockSpec((None, 1, DN_HEAD_DIM), lambda i: (layer, 0, 0))],
        out_specs=bspec,
        out_shape=full,
        scratch_shapes=[pltpu.VMEM((b * DN_HEADS, DN_HEAD_DIM, DN_HEAD_DIM), F32)],
        compiler_params=_cparams("arbitrary"),
        name="dn_scan",
    )(u, wy, qd, kd, attn, egl, proj, norm_w)


def _proj_ln_body(a_ref, w_ref, h_ref, g_ref, b_ref, o_ref):
    y = jnp.dot(a_ref[...], w_ref[...], preferred_element_type=F32)
    o_ref[...] = _layer_norm(ALPHA * h_ref[...] + y, g_ref[...], b_ref[...])


def _proj_res_ln(a, w, h, g, b, w_layer, ln_layer, name):
    m, k = a.shape
    n = h.shape[1]
    row = pl.BlockSpec((None, 1, n), lambda i: (ln_layer, 0, 0))
    return pl.pallas_call(
        _proj_ln_body,
        grid=(m // LN_TM,),
        in_specs=[pl.BlockSpec((LN_TM, k), lambda i: (i, 0)),
                  pl.BlockSpec((None, k, n), lambda i: (w_layer, 0, 0)),
                  pl.BlockSpec((LN_TM, n), lambda i: (i, 0)), row, row],
        out_specs=pl.BlockSpec((LN_TM, n), lambda i: (i, 0)),
        out_shape=jax.ShapeDtypeStruct((m, n), F32),
        compiler_params=_cparams("parallel"),
        name=name,
    )(a, w, h, g, b)


def _mm_nt_body(w_ref, x_ref, o_ref):
    o_ref[...] = _nt(w_ref[...], x_ref[...])


def _matmul_nt(w_t, x, name):
    n, k = w_t.shape
    m = x.shape[0]
    return pl.pallas_call(
        _mm_nt_body,
        grid=(n // MM_TN, m // MM_TM),
        in_specs=[pl.BlockSpec((MM_TN, k), lambda j, i: (j, 0)),
                  pl.BlockSpec((MM_TM, k), lambda j, i: (i, 0))],
        out_specs=pl.BlockSpec((MM_TN, MM_TM), lambda j, i: (j, i)),
        out_shape=jax.ShapeDtypeStruct((n, m), F32),
        compiler_params=_cparams("parallel", "parallel"),
        name=name,
    )(w_t, x)


def _attn_body(q_ref, k_ref, vt_ref, lam_ref, sw_ref, o_ref, m_ref, l_ref, acc_ref, sta_ref, stb_ref, *, lam_init):
    i = pl.program_id(2)
    t = ATT_T
    q = q_ref[0] * (DA_HEAD_DIM ** -0.5 * math.log2(math.e))
    lane = lax.broadcasted_iota(I32, q.shape, 1)
    q_maps = (jnp.where(lane < DA_HEAD_DIM, q, 0.0), jnp.where(lane >= DA_HEAD_DIM, q, 0.0))
    m_ref[...] = jnp.full_like(m_ref, -jnp.inf)
    l_ref[...] = jnp.zeros_like(l_ref)
    acc_ref[...] = jnp.zeros_like(acc_ref)

    maps = range(2)
    st_bufs = (sta_ref, stb_ref)

    def scores(j, buf):
        k = k_ref[0, pl.ds(pl.multiple_of(j * t, t), t), :]
        for s in maps:
            st_bufs[buf][s] = _nt(k, q_maps[s])

    def accumulate(j, buf, masked):
        vt = vt_ref[:, pl.ds(pl.multiple_of(j * t, t), t)]
        st = [st_bufs[buf][s] for s in maps]
        if masked:
            ri = lax.broadcasted_iota(I32, st[0].shape, 0)
            ci = lax.broadcasted_iota(I32, st[0].shape, 1)
            st = [jnp.where(ri <= ci, x, -jnp.inf) for x in st]
        m_old = [m_ref[s] for s in maps]
        m_new = [jnp.maximum(m_old[s], jnp.max(st[s], axis=0, keepdims=True)) for s in maps]
        p = [jnp.exp2(st[s] - m_new[s]) for s in maps]
        pv = [jnp.dot(vt, p[s], preferred_element_type=F32) for s in maps]
        for s in maps:
            corr = jnp.exp2(m_old[s] - m_new[s])
            l_ref[s] = corr * l_ref[s] + jnp.sum(p[s], axis=0, keepdims=True)
            acc_ref[s] = corr * acc_ref[s] + pv[s]
            m_ref[s] = m_new[s]

    scores(0, 0)

    def two_tiles(u, carry):
        j = 2 * u
        scores(j + 1, 1)
        accumulate(j, 0, False)
        scores(j + 2, 0)
        accumulate(j + 1, 1, False)
        return carry

    lax.fori_loop(0, i // 2, two_tiles, 0)
    odd = i % 2 == 1

    @pl.when(odd)
    def _():
        scores(i, 1)
        accumulate(i - 1, 0, False)
        accumulate(i, 1, True)

    @pl.when(jnp.logical_not(odd))
    def _():
        accumulate(i, 0, True)

    lp = lam_ref[...]
    lam = (jnp.exp(jnp.sum(lp[0:1] * lp[1:2], axis=-1, keepdims=True))
           - jnp.exp(jnp.sum(lp[2:3] * lp[3:4], axis=-1, keepdims=True)) + lam_init)
    ot = acc_ref[0] / l_ref[0] - lam * (acc_ref[1] / l_ref[1])
    ms = jnp.mean(ot * ot, axis=0, keepdims=True)
    ot = ot * lax.rsqrt(ms + RMS_EPS) * sw_ref[...] * (1.0 - lam_init)
    o_ref[0] = ot.T


def _diff_attention(qp, kp, vt, lam_p, subln_w, layer, lam_init):
    b, t, w = qp.shape
    tq = ATT_T
    return pl.pallas_call(
        functools.partial(_attn_body, lam_init=lam_init),
        grid=(b, DA_HEADS, t // tq),
        in_specs=[pl.BlockSpec((1, tq, DA_V_DIM), lambda bi, h, i: (bi, i, h)),
                  pl.BlockSpec((1, t, DA_V_DIM), lambda bi, h, i: (bi, 0, h)),
                  pl.BlockSpec((DA_V_DIM, t), lambda bi, h, i: (h, bi)),
                  pl.BlockSpec((None, 4, DA_HEAD_DIM), lambda bi, h, i: (layer, 0, 0)),
                  pl.BlockSpec((None, DA_V_DIM, 1), lambda bi, h, i: (layer, 0, 0))],
        out_specs=pl.BlockSpec((1, tq, DA_V_DIM), lambda bi, h, i: (bi, i, h)),
        out_shape=jax.ShapeDtypeStruct((b, t, w), F32),
        scratch_shapes=[pltpu.VMEM((2, 1, tq), F32), pltpu.VMEM((2, 1, tq), F32),
                        pltpu.VMEM((2, DA_V_DIM, tq), F32),
                        pltpu.VMEM((2, tq, tq), F32), pltpu.VMEM((2, tq, tq), F32)],
        compiler_params=_cparams("parallel", "parallel", "parallel"),
        name="diff_attn",
    )(qp, kp, vt, lam_p, subln_w)


def _router_body(h_ref, w_ref, info_ref, gate_ref, cnt_ref, carry_ref):
    tm = h_ref.shape[0]

    @pl.when(pl.program_id(0) == 0)
    def _():
        carry_ref[...] = jnp.zeros_like(carry_ref)

    h = h_ref[...]
    w = w_ref[...]
    h_hi = h.astype(jnp.bfloat16)
    h_lo = (h - h_hi.astype(F32)).astype(jnp.bfloat16)
    w_hi = w.astype(jnp.bfloat16)
    w_lo = (w - w_hi.astype(F32)).astype(jnp.bfloat16)
    logits = (jnp.dot(h_hi, w_hi, preferred_element_type=F32) + jnp.dot(h_lo, w_hi, preferred_element_type=F32)
              + jnp.dot(h_hi, w_lo, preferred_element_type=F32))
    lane = lax.broadcasted_iota(I32, logits.shape, 1)
    big = jnp.int32(LANES)

    def first_max(vals):
        mx = jnp.max(vals, axis=-1, keepdims=True)
        idx = jnp.min(jnp.where(vals == mx, lane, big), axis=-1, keepdims=True)
        return mx, idx

    gl = jnp.where(lane < N_GROUPS, logits, -jnp.inf)
    gmax, gidx = first_max(gl)
    p_group = 1.0 / jnp.sum(jnp.exp(gl - gmax), axis=-1, keepdims=True)
    lo = N_GROUPS + gidx * EXPERTS_PER_GROUP
    el = jnp.where((lane >= lo) & (lane < lo + EXPERTS_PER_GROUP), logits, -jnp.inf)
    m1, i1 = first_max(el)
    m2, i2 = first_max(jnp.where(lane == i1, -jnp.inf, el))
    e2 = jnp.exp(m2 - m1)
    gate1 = p_group / (1.0 + e2)
    gate2 = p_group * e2 / (1.0 + e2)

    oh = ((lane == i1) | (lane == i2)).astype(F32)
    ri = lax.broadcasted_iota(I32, (tm, tm), 0)
    ci = lax.broadcasted_iota(I32, (tm, tm), 1)
    before = jnp.dot((ci < ri).astype(F32), oh, preferred_element_type=F32) + carry_ref[0:1, :]
    rank1 = jnp.sum(jnp.where(lane == i1, before, 0.0), axis=-1, keepdims=True).astype(I32)
    rank2 = jnp.sum(jnp.where(lane == i2, before, 0.0), axis=-1, keepdims=True).astype(I32)
    total = carry_ref[0:1, :] + jnp.sum(oh, axis=0, keepdims=True)
    carry_ref[0:1, :] = total

    info = jnp.where(lane == 0, i1 - N_GROUPS,
                     jnp.where(lane == 1, i2 - N_GROUPS,
                               jnp.where(lane == 2, rank1, jnp.where(lane == 3, rank2, 0))))
    info_ref[...] = info.T[:SUBLANES]
    gate_ref[...] = jnp.where(lane == 0, gate1, jnp.where(lane == 1, gate2, 0.0))
    cnt_ref[...] = jnp.broadcast_to(total, cnt_ref.shape).astype(I32)


def _moe_router(h, w_route):
    m, k = h.shape
    tm = ROUTE_TM
    return pl.pallas_call(
        _router_body,
        grid=(m // tm,),
        in_specs=[pl.BlockSpec((tm, k), lambda i: (i, 0)),
                  pl.BlockSpec((k, LANES), lambda i: (0, 0))],
        out_specs=[pl.BlockSpec((SUBLANES, tm), lambda i: (0, i)),
                   pl.BlockSpec((tm, LANES), lambda i: (i, 0)),
                   pl.BlockSpec((SUBLANES, LANES), lambda i: (0, 0))],
        out_shape=[jax.ShapeDtypeStruct((SUBLANES, m), I32),
                   jax.ShapeDtypeStruct((m, LANES), F32),
                   jax.ShapeDtypeStruct((SUBLANES, LANES), I32)],
        scratch_shapes=[pltpu.VMEM((SUBLANES, LANES), F32)],
        compiler_params=_cparams("arbitrary"),
        name="moe_router",
    )(h, w_route)


ZERO_ROWS = MOE_TB // 2


def _dispatch_body(pos0_ref, pos1_ref, zlo_ref, zhi_ref, x_ref, xs_hbm, zero_ref, sem, zsem):
    i = pl.program_id(0)
    tm = x_ref.shape[0]

    def zero_pieces(act):
        def per_expert(e, carry):
            lo = zlo_ref[e]
            hi = zhi_ref[e]
            n = hi - lo
            size = ZERO_ROWS
            end = hi
            while size >= SUBLANES:
                take = (n & size) != 0

                @pl.when(take)
                def _(end=end, size=size):
                    off = pl.multiple_of(end - size, SUBLANES)
                    act(pltpu.make_async_copy(zero_ref.at[pl.ds(0, size), :], xs_hbm.at[pl.ds(off, size), :], zsem))

                end = end - jnp.where(take, size, 0)
                size //= 2
            for r in range(SUBLANES - 1):
                @pl.when(r < (n & (SUBLANES - 1)))
                def _(r=r):
                    act(pltpu.make_async_copy(zero_ref.at[pl.ds(0, 1), :], xs_hbm.at[pl.ds(lo + r, 1), :], zsem))
            return carry

        lax.fori_loop(0, N_EXPERTS, per_expert, 0)

        def trailing(p, carry):
            off = pl.multiple_of(p * ZERO_ROWS, ZERO_ROWS)
            act(pltpu.make_async_copy(zero_ref, xs_hbm.at[pl.ds(off, ZERO_ROWS), :], zsem))
            return carry

        lax.fori_loop(zlo_ref[N_EXPERTS] // ZERO_ROWS, xs_hbm.shape[0] // ZERO_ROWS, trailing, 0)

    @pl.when(i == 0)
    def _():
        zero_ref[...] = jnp.zeros_like(zero_ref)
        zero_pieces(lambda cp: cp.start())

    def issue(r, carry):
        t = i * tm + r
        src = x_ref.at[pl.ds(r, 1), :]
        pltpu.make_async_copy(src, xs_hbm.at[pl.ds(pos0_ref[t], 1), :], sem).start()
        pltpu.make_async_copy(src, xs_hbm.at[pl.ds(pos1_ref[t], 1), :], sem).start()
        return carry

    lax.fori_loop(0, tm, issue, 0, unroll=8)
    for _ in range(TOP_K):
        pltpu.make_async_copy(x_ref, xs_hbm.at[pl.ds(0, tm), :], sem).wait()

    @pl.when(i == 0)
    def _():
        zero_pieces(lambda cp: cp.wait())


def _moe_dispatch(x, pos0, pos1, zlo, zhi, n_rows):
    m, d = x.shape
    tm = DISP_TM
    return pl.pallas_call(
        _dispatch_body,
        grid_spec=pltpu.PrefetchScalarGridSpec(
            num_scalar_prefetch=4, grid=(m // tm,),
            in_specs=[pl.BlockSpec((tm, d), lambda i, *_: (i, 0))],
            out_specs=pl.BlockSpec(memory_space=pl.ANY),
            scratch_shapes=[pltpu.VMEM((ZERO_ROWS, d), F32), pltpu.SemaphoreType.DMA(()),
                            pltpu.SemaphoreType.DMA(())]),
        out_shape=jax.ShapeDtypeStruct((n_rows, d), F32),
        compiler_params=_cparams("arbitrary"),
        name="moe_dispatch",
    )(pos0, pos1, zlo, zhi, x)


def _expert_body(eid_ref, nused_ref, x_ref, w13_ref, w2_ref, o_ref):
    ff = w2_ref.shape[0]

    used = pl.program_id(0) < nused_ref[0]

    @pl.when(used)
    def _():
        hcat = jnp.dot(x_ref[...], w13_ref[...], preferred_element_type=F32)
        hid = _silu(hcat[:, :ff]) * hcat[:, ff:]
        o_ref[...] = jnp.dot(hid, w2_ref[...], preferred_element_type=F32)

    @pl.when(jnp.logical_not(used))
    def _():
        o_ref[...] = jnp.zeros_like(o_ref)


def _moe_experts(xs, block_eid, n_used, w13, w2, layer):
    n_rows, d = xs.shape
    tb = MOE_TB
    ff = w2.shape[2]

    def row_block(b, eid, nused):
        return (jnp.minimum(b, nused[0] - 1), 0)

    return pl.pallas_call(
        _expert_body,
        grid_spec=pltpu.PrefetchScalarGridSpec(
            num_scalar_prefetch=2, grid=(n_rows // tb,),
            in_specs=[pl.BlockSpec((tb, d), row_block),
                      pl.BlockSpec((None, None, d, 2 * ff), lambda b, eid, nused: (layer, eid[b], 0, 0)),
                      pl.BlockSpec((None, None, ff, d), lambda b, eid, nused: (layer, eid[b], 0, 0))],
            out_specs=pl.BlockSpec((tb, d), lambda b, eid, nused: (b, 0))),
        out_shape=jax.ShapeDtypeStruct((n_rows, d), F32),
        compiler_params=_cparams("arbitrary"),
        name="moe_experts",
    )(block_eid, n_used, xs, w13, w2)


def _combine_body(pos0_ref, pos1_ref, ys_hbm, h_ref, gate_ref, g_ref, b_ref, o_ref, y0_ref, y1_ref, sem0, sem1):
    i = pl.program_id(0)
    tm = h_ref.shape[0]

    def issue(r, carry):
        t = i * tm + r
        pltpu.make_async_copy(ys_hbm.at[pl.ds(pos0_ref[t], 1), :], y0_ref.at[pl.ds(r, 1), :], sem0).start()
        pltpu.make_async_copy(ys_hbm.at[pl.ds(pos1_ref[t], 1), :], y1_ref.at[pl.ds(r, 1), :], sem1).start()
        return carry

    lax.fori_loop(0, tm, issue, 0, unroll=8)
    pltpu.make_async_copy(ys_hbm.at[pl.ds(0, tm), :], y0_ref, sem0).wait()
    pltpu.make_async_copy(ys_hbm.at[pl.ds(0, tm), :], y1_ref, sem1).wait()
    gates = gate_ref[...]
    ffn = gates[:, 0:1] * y0_ref[...] + gates[:, 1:2] * y1_ref[...]
    o_ref[...] = _layer_norm(ALPHA * h_ref[...] + ffn, g_ref[...], b_ref[...])


def _moe_combine(ys, pos0, pos1, h, gates, g, b, layer):
    m, d = h.shape
    tm = COMB_TM
    row = pl.BlockSpec((None, 1, d), lambda i, *_: (layer, 0, 0))
    return pl.pallas_call(
        _combine_body,
        grid_spec=pltpu.PrefetchScalarGridSpec(
            num_scalar_prefetch=2, grid=(m // tm,),
            in_specs=[pl.BlockSpec(memory_space=pl.ANY),
                      pl.BlockSpec((tm, d), lambda i, *_: (i, 0)),
                      pl.BlockSpec((tm, LANES), lambda i, *_: (i, 0)), row, row],
            out_specs=pl.BlockSpec((tm, d), lambda i, *_: (i, 0)),
            scratch_shapes=[pltpu.VMEM((tm, d), F32), pltpu.VMEM((tm, d), F32),
                            pltpu.SemaphoreType.DMA(()), pltpu.SemaphoreType.DMA(())]),
        out_shape=jax.ShapeDtypeStruct((m, d), F32),
        compiler_params=_cparams("arbitrary"),
        name="moe_combine",
    )(pos0, pos1, ys, h, gates, g, b)


def _moe_layer(h, w_group, w_expert, w13, w2, ln_g, ln_b, layer):
    n, d = h.shape
    tb = MOE_TB
    w_route = jnp.pad(jnp.concatenate([w_group, w_expert], axis=1), ((0, 0), (0, LANES - N_GROUPS - N_EXPERTS)))
    info, gates, cnt = _moe_router(h, w_route)
    counts = cnt[0, N_GROUPS:N_GROUPS + N_EXPERTS]
    padded = (counts + tb - 1) // tb * tb
    pad_end = jnp.cumsum(padded)
    pad_start = pad_end - padded
    pos0 = pad_start[info[0]] + info[2]
    pos1 = pad_start[info[1]] + info[3]
    n_blocks = (n * TOP_K + N_EXPERTS * (tb - 1) + tb - 1) // tb
    n_used = (pad_end[-1] // tb).astype(I32)
    blk = jnp.minimum(jnp.arange(n_blocks, dtype=I32), n_used - 1)
    block_eid = jnp.minimum(jnp.sum(pad_end[None, :] <= (blk * tb)[:, None], axis=1), N_EXPERTS - 1).astype(I32)
    zlo = jnp.concatenate([pad_start + counts, pad_end[-1:]])
    xs = _moe_dispatch(h, pos0, pos1, zlo, pad_end, n_blocks * tb)
    ys = _moe_experts(xs, block_eid, n_used.reshape(1), w13, w2, layer)
    return _moe_combine(ys, pos0, pos1, h, gates, ln_g, ln_b, layer)


def _lambda_init(layer_idx):
    return 0.8 - 0.6 * math.exp(-0.3 * layer_idx)


def kernel(x, a_w_in, a_conv_w, a_a_log, a_dt_bias, a_norm_w, a_w_out, kv_w, b_w_q, b_lambda, b_subln_w, b_w_out,
           ln_mix_g, ln_mix_b, ln_ffn_g, ln_ffn_b, moe_w_group, moe_w_expert, moe_w13, moe_w2):
    b, t, d = x.shape
    n = b * t
    dn_w = DN_HEADS * DN_HEAD_DIM
    h = x.reshape(n, d)

    def per_layer_rows(p):
        return p.reshape(p.shape[0], 1, p.shape[1])

    a_norm_w = per_layer_rows(a_norm_w)
    b_subln_w = b_subln_w.reshape(b_subln_w.shape + (1,))
    ln_mix_g, ln_mix_b = per_layer_rows(ln_mix_g), per_layer_rows(ln_mix_b)
    ln_ffn_g, ln_ffn_b = per_layer_rows(ln_ffn_g), per_layer_rows(ln_ffn_b)
    da_w = DA_HEADS * DA_V_DIM
    kp = vt = None
    for layer in range(DEPTH):
        if layer < N_A_LAYERS:
            proj = _matmul(h, a_w_in, layer, 4 * dn_w, "dn_in_proj").reshape(b, t, 4 * dn_w)
            gb = _dn_gates(h, a_w_in[layer, :, 4 * dn_w:], a_a_log[layer], a_dt_bias[layer]).reshape(b, t, LANES)
            q, k, v = _dn_conv(proj, a_conv_w, layer)
            mix_in = _dn_delta(q, k, v, proj, gb, a_norm_w, layer).reshape(n, dn_w)
            h = _proj_res_ln(mix_in, a_w_out, h, ln_mix_g, ln_mix_b, layer, layer, "dn_out_ln")
        else:
            j = layer - N_A_LAYERS
            if j == 0:
                kp = _matmul(h, kv_w.reshape((1,) + kv_w.shape), 0, da_w, "k_proj").reshape(b, t, da_w)
                vt = _matmul_nt(kv_w[:, da_w:].T, h, "v_proj")
            qp = _matmul(h, b_w_q, j, da_w, "q_proj").reshape(b, t, da_w)
            mix_in = _diff_attention(qp, kp, vt, b_lambda, b_subln_w, j, _lambda_init(layer)).reshape(n, da_w)
            h = _proj_res_ln(mix_in, b_w_out, h, ln_mix_g, ln_mix_b, j, layer, "da_out_ln")
        h = _moe_layer(h, moe_w_group[layer], moe_w_expert[layer], moe_w13, moe_w2, ln_ffn_g, ln_ffn_b, layer)
    return h.reshape(b, t, d)
```

```python
import functools
import math

import jax
import jax.numpy as jnp
from jax import lax
from jax.experimental import pallas as pl
from jax.experimental.pallas import tpu as pltpu

F32 = jnp.float32
BF16 = jnp.bfloat16
I32 = jnp.int32

DEPTH = 4
N_A_LAYERS = DEPTH // 2
DN_HEADS = 8
DN_HEAD_DIM = 128
CONV_K = 4
DN_CHUNK = 64
DA_HEADS = 8
DA_HEAD_DIM = 64
DA_V_DIM = 2 * DA_HEAD_DIM
N_GROUPS = 4
EXPERTS_PER_GROUP = 8
N_EXPERTS = N_GROUPS * EXPERTS_PER_GROUP
TOP_K = 2
ALPHA = (2 * DEPTH) ** 0.25
LN_EPS = 1e-5
RMS_EPS = 1e-6

LANES = 128
SUBLANES = 8
VMEM_LIMIT_BYTES = 56 * 1024 * 1024

MM_TM = 1024
MM_TN = 512
LN_TM = 512
CONV_TT = 256
DN_ROWS = 256
ATT_TK = 512
ROUTE_TM = 512
MOE_TB = 256
DISP_TM = 512
COMB_TM = 256


def _cparams(*sem):
    return pltpu.CompilerParams(dimension_semantics=sem, vmem_limit_bytes=VMEM_LIMIT_BYTES)


def _silu(x):
    return x * (1.0 / (1.0 + jnp.exp(-x)))


def _layer_norm(x, g, b):
    mu = jnp.mean(x, axis=-1, keepdims=True)
    xc = x - mu
    var = jnp.mean(xc * xc, axis=-1, keepdims=True)
    return xc * lax.rsqrt(var + LN_EPS) * g + b


def _mxu(x):
    return x.astype(BF16)


def _mm_body(x_ref, w_ref, o_ref):
    o_ref[...] = jnp.dot(_mxu(x_ref[...]), _mxu(w_ref[...]), preferred_element_type=F32).astype(o_ref.dtype)


def _matmul(x, w, layer, n_out, name, out_dtype=F32):
    m, k = x.shape
    return pl.pallas_call(
        _mm_body,
        grid=(m // MM_TM, n_out // MM_TN),
        in_specs=[pl.BlockSpec((MM_TM, k), lambda i, j: (i, 0)),
                  pl.BlockSpec((None, k, MM_TN), lambda i, j: (layer, 0, j))],
        out_specs=pl.BlockSpec((MM_TM, MM_TN), lambda i, j: (i, j)),
        out_shape=jax.ShapeDtypeStruct((m, n_out), out_dtype),
        compiler_params=_cparams("parallel", "parallel"),
        name=name,
    )(x, w)


def _gates_body(x_ref, w_ref, alog_ref, dtb_ref, o_ref):
    logit = jnp.dot(x_ref[...], w_ref[...], preferred_element_type=F32)
    lane = lax.broadcasted_iota(I32, logit.shape, 1)
    beta = 1.0 / (1.0 + jnp.exp(-logit))
    sp_in = logit + dtb_ref[...]
    softplus = jnp.maximum(sp_in, 0.0) + jnp.log1p(jnp.exp(-jnp.abs(sp_in)))
    g = -jnp.exp(alog_ref[...]) * softplus
    o_ref[...] = jnp.where(lane < DN_HEADS, beta, g)


def _dn_gates(x, w_small, a_log, dt_bias):
    m, k = x.shape
    pad = LANES - 2 * DN_HEADS
    w = jnp.pad(w_small, ((0, 0), (0, pad)))
    alog = jnp.pad(a_log, (DN_HEADS, pad)).reshape(1, LANES)
    dtb = jnp.pad(dt_bias, (DN_HEADS, pad)).reshape(1, LANES)
    return pl.pallas_call(
        _gates_body,
        grid=(m // MM_TM,),
        in_specs=[pl.BlockSpec((MM_TM, k), lambda i: (i, 0)),
                  pl.BlockSpec((k, LANES), lambda i: (0, 0)),
                  pl.BlockSpec((1, LANES), lambda i: (0, 0)),
                  pl.BlockSpec((1, LANES), lambda i: (0, 0))],
        out_specs=pl.BlockSpec((MM_TM, LANES), lambda i: (i, 0)),
        out_shape=jax.ShapeDtypeStruct((m, LANES), F32),
        compiler_params=_cparams("parallel"),
        name="dn_gates",
    )(x, w, alog, dtb)


def _conv_body(xq_ref, xk_ref, xv_ref, hq_ref, hk_ref, hv_ref, w_ref, q_ref, k_ref, v_ref):
    first = pl.program_id(1) == 0
    width = xq_ref.shape[-1]

    def conv_silu(x_ref, halo_ref, col0):
        halo = jnp.where(first, 0.0, halo_ref[0])
        xe = jnp.concatenate([halo, x_ref[0]], axis=0)
        tt = x_ref.shape[1]
        acc = xe[SUBLANES:SUBLANES + tt] * w_ref[CONV_K - 1:CONV_K, col0:col0 + width]
        for j in range(CONV_K - 1):
            shifted = pltpu.roll(xe, CONV_K - 1 - j, axis=0)[SUBLANES:SUBLANES + tt]
            acc = acc + shifted * w_ref[j:j + 1, col0:col0 + width]
        return _silu(acc)

    def l2n(y, scale):
        outs = []
        for h in range(DN_HEADS):
            yh = y[:, h * DN_HEAD_DIM:(h + 1) * DN_HEAD_DIM]
            ss = jnp.sum(yh * yh, axis=-1, keepdims=True)
            outs.append(yh * (lax.rsqrt(ss + RMS_EPS) * scale))
        return jnp.concatenate(outs, axis=-1)

    q_ref[0] = l2n(conv_silu(xq_ref, hq_ref, 0), DN_HEAD_DIM ** -0.5)
    k_ref[0] = l2n(conv_silu(xk_ref, hk_ref, width), 1.0)
    v_ref[0] = conv_silu(xv_ref, hv_ref, 2 * width)


def _dn_conv(proj, conv_w, layer):
    b, t, _ = proj.shape
    w = DN_HEADS * DN_HEAD_DIM
    tt = CONV_TT
    hb = tt // SUBLANES

    def xspec(c):
        return pl.BlockSpec((1, tt, w), lambda bi, i: (bi, i, c))

    def hspec(c):
        return pl.BlockSpec((1, SUBLANES, w), lambda bi, i: (bi, jnp.maximum(i * hb - 1, 0), c))

    out = jax.ShapeDtypeStruct((b, t, w), F32)
    ospec = pl.BlockSpec((1, tt, w), lambda bi, i: (bi, i, 0))
    return pl.pallas_call(
        _conv_body,
        grid=(b, t // tt),
        in_specs=[xspec(0), xspec(1), xspec(2), hspec(0), hspec(1), hspec(2),
                  pl.BlockSpec((None, CONV_K, 3 * w), lambda bi, i: (layer, 0, 0))],
        out_specs=[ospec, ospec, ospec],
        out_shape=[out, out, out],
        compiler_params=_cparams("parallel", "parallel"),
        name="dn_conv",
    )(proj, proj, proj, proj, proj, proj, conv_w)


def _split3(x):
    hi = x.astype(jnp.bfloat16).astype(F32)
    r = x - hi
    mid = r.astype(jnp.bfloat16).astype(F32)
    return hi, mid, r - mid


def _nt(a, b):
    return lax.dot_general(a, b, (((1,), (1,)), ((), ())), preferred_element_type=F32)


def _tn(a, b):
    return lax.dot_general(a, b, (((0,), (0,)), ((), ())), preferred_element_type=F32)


def _delta_body(u_ref, w_ref, qd_ref, kd_ref, attn_ref, egl_ref, z_ref, nw_ref, o_ref, state_ref):
    c = DN_CHUNK
    d = DN_HEAD_DIM

    @pl.when(pl.program_id(0) == 0)
    def _():
        state_ref[...] = jnp.zeros_like(state_ref)

    nw = nw_ref[...]
    nb = u_ref.shape[0]
    chains = [(bi, h) for bi in range(nb) for h in range(DN_HEADS)]

    def chunk(ic, carry):
        rows = pl.ds(pl.multiple_of(ic * c, c), c)
        s_old, ws_qs = {}, {}
        for bi, h in chains:
            cols = slice(h * d, (h + 1) * d)
            s_old[bi, h] = state_ref[bi * DN_HEADS + h]
            lhs = jnp.concatenate([w_ref[bi, rows, cols], qd_ref[bi, rows, cols]], axis=0)
            ws_qs[bi, h] = jnp.dot(lhs, _mxu(s_old[bi, h]), preferred_element_type=F32)
        v_new = {}
        for bi, h in chains:
            cols = slice(h * d, (h + 1) * d)
            v_new[bi, h] = u_ref[bi, rows, cols] - ws_qs[bi, h][:c]
        for bi, h in chains:
            cols = slice(h * d, (h + 1) * d)
            acols = slice(h * c, (h + 1) * c)
            vn = _mxu(v_new[bi, h])
            o = ws_qs[bi, h][c:] + jnp.dot(attn_ref[bi, rows, acols], vn, preferred_element_type=F32)
            gl = egl_ref[bi, pl.ds(ic * DN_HEADS + h, 1), :]
            state_ref[bi * DN_HEADS + h] = s_old[bi, h] * gl + _tn(kd_ref[bi, rows, cols], vn)
            ms = jnp.mean(o * o, axis=-1, keepdims=True)
            o_ref[bi, rows, cols] = (o * lax.rsqrt(ms + RMS_EPS) * nw * _silu(z_ref[bi, rows, cols])).astype(o_ref.dtype)
        return carry

    lax.fori_loop(0, u_ref.shape[1] // c, chunk, 0)


def _dn_prep_body(q_ref, k_ref, v_ref, gb_ref, u_ref, w_ref, qd_ref, kd_ref, attn_ref, egl_ref):
    c = DN_CHUNK
    d = DN_HEAD_DIM
    ri = lax.broadcasted_iota(I32, (c, c), 0)
    ci = lax.broadcasted_iota(I32, (c, c), 1)
    causal = ri >= ci
    strict = ri > ci
    tri = causal.astype(F32)
    heads = range(DN_HEADS)

    def chunk(ic, carry):
        rows = pl.ds(pl.multiple_of(ic * c, c), c)
        gbt = gb_ref[0, rows, :]
        g_hi, g_mid, g_lo = _split3(gbt)
        gcol = (jnp.dot(tri, g_hi, preferred_element_type=F32)
                + jnp.dot(tri, g_mid, preferred_element_type=F32)
                + jnp.dot(tri, g_lo, preferred_element_type=F32))
        grow = gcol.T
        g_last_col = grow[DN_HEADS:2 * DN_HEADS, c - 1:c]
        egl_ref[0, pl.ds(pl.multiple_of(ic * DN_HEADS, DN_HEADS), DN_HEADS), :] = jnp.broadcast_to(
            jnp.exp(g_last_col), (DN_HEADS, d))
        a_mat, x, p = {}, {}, {}
        for h in heads:
            cols = slice(h * d, (h + 1) * d)
            qh = q_ref[0, rows, cols]
            kh = k_ref[0, rows, cols]
            beta_c = gbt[:, h:h + 1]
            gc_c = gcol[:, DN_HEADS + h:DN_HEADS + h + 1]
            gc_r = grow[DN_HEADS + h:DN_HEADS + h + 1, :]
            decay = jnp.exp(jnp.where(causal, gc_c - gc_r, -jnp.inf))
            eg_c = jnp.exp(gc_c)
            k_beta = kh * beta_c
            a_mat[h] = jnp.where(strict, _nt(k_beta, kh) * decay, 0.0)
            attn_ref[0, rows, h * c:(h + 1) * c] = _mxu(jnp.where(causal, _nt(qh, kh) * decay, 0.0))
            qd_ref[0, rows, cols] = _mxu(qh * eg_c)
            kd_ref[0, rows, cols] = _mxu(kh * jnp.exp(gc_r[:, c - 1:c] - gc_c))
            x[h] = jnp.concatenate([v_ref[0, rows, cols] * beta_c, k_beta * eg_c], axis=-1)
        for h in heads:
            x[h] = x[h] - jnp.dot(a_mat[h], x[h], preferred_element_type=F32)
            p[h] = a_mat[h]
        for _ in range(int(math.log2(c)) - 1):
            for h in heads:
                p[h] = jnp.dot(p[h], p[h], preferred_element_type=F32)
            for h in heads:
                x[h] = x[h] + jnp.dot(p[h], x[h], preferred_element_type=F32)
        for h in heads:
            cols = slice(h * d, (h + 1) * d)
            u_ref[0, rows, cols] = x[h][:, :d]
            w_ref[0, rows, cols] = _mxu(x[h][:, d:])
        return carry

    lax.fori_loop(0, q_ref.shape[1] // c, chunk, 0)


def _dn_delta(q, k, v, proj, gb, norm_w, layer):
    b, t, w = q.shape
    rs = DN_ROWS
    cps = rs // DN_CHUNK
    aw = DN_HEADS * DN_CHUNK
    spec = pl.BlockSpec((1, rs, w), lambda bi, i: (bi, i, 0))
    full = jax.ShapeDtypeStruct((b, t, w), F32)
    mxu_only = jax.ShapeDtypeStruct((b, t, w), BF16)
    u, wy, qd, kd, attn, egl = pl.pallas_call(
        _dn_prep_body,
        grid=(b, t // rs),
        in_specs=[spec, spec, spec, pl.BlockSpec((1, rs, LANES), lambda bi, i: (bi, i, 0))],
        out_specs=[spec, spec, spec, spec,
                   pl.BlockSpec((1, rs, aw), lambda bi, i: (bi, i, 0)),
                   pl.BlockSpec((1, cps * DN_HEADS, DN_HEAD_DIM), lambda bi, i: (bi, i, 0))],
        out_shape=[full, mxu_only, mxu_only, mxu_only,
                   jax.ShapeDtypeStruct((b, t, aw), BF16),
                   jax.ShapeDtypeStruct((b, t // DN_CHUNK * DN_HEADS, DN_HEAD_DIM), F32)],
        compiler_params=_cparams("parallel", "parallel"),
        name="dn_prep",
    )(q, k, v, gb)
    bspec = pl.BlockSpec((b, rs, w), lambda i: (0, i, 0))
    return pl.pallas_call(
        _delta_body,
        grid=(t // rs,),
        in_specs=[bspec, bspec, bspec, bspec,
                  pl.BlockSpec((b, rs, aw), lambda i: (0, i, 0)),
                  pl.BlockSpec((b, cps * DN_HEADS, DN_HEAD_DIM), lambda i: (0, i, 0)),
                  pl.BlockSpec((b, rs, w), lambda i: (0, i, 3)),
                  pl.BlockSpec((None, 1, DN_HEAD_DIM), lambda i: (layer, 0, 0))],
        out_specs=bspec,
        out_shape=mxu_only,
        scratch_shapes=[pltpu.VMEM((b * DN_HEADS, DN_HEAD_DIM, DN_HEAD_DIM), F32)],
        compiler_params=_cparams("arbitrary"),
        name="dn_scan",
    )(u, wy, qd, kd, attn, egl, proj, norm_w)


def _proj_ln_body(a_ref, w_ref, h_ref, g_ref, b_ref, o_ref):
    y = jnp.dot(_mxu(a_ref[...]), _mxu(w_ref[...]), preferred_element_type=F32)
    o_ref[...] = _layer_norm(ALPHA * h_ref[...] + y, g_ref[...], b_ref[...])


def _proj_res_ln(a, w, h, g, b, w_layer, ln_layer, name):
    m, k = a.shape
    n = h.shape[1]
    row = pl.BlockSpec((None, 1, n), lambda i: (ln_layer, 0, 0))
    return pl.pallas_call(
        _proj_ln_body,
        grid=(m // LN_TM,),
        in_specs=[pl.BlockSpec((LN_TM, k), lambda i: (i, 0)),
                  pl.BlockSpec((None, k, n), lambda i: (w_layer, 0, 0)),
                  pl.BlockSpec((LN_TM, n), lambda i: (i, 0)), row, row],
        out_specs=pl.BlockSpec((LN_TM, n), lambda i: (i, 0)),
        out_shape=jax.ShapeDtypeStruct((m, n), F32),
        compiler_params=_cparams("parallel"),
        name=name,
    )(a, w, h, g, b)


def _mm_nt_body(w_ref, x_ref, o_ref):
    o_ref[...] = _nt(_mxu(w_ref[...]), _mxu(x_ref[...])).astype(o_ref.dtype)


def _matmul_nt(w_t, x, name, out_dtype=F32):
    n, k = w_t.shape
    m = x.shape[0]
    return pl.pallas_call(
        _mm_nt_body,
        grid=(n // MM_TN, m // MM_TM),
        in_specs=[pl.BlockSpec((MM_TN, k), lambda j, i: (j, 0)),
                  pl.BlockSpec((MM_TM, k), lambda j, i: (i, 0))],
        out_specs=pl.BlockSpec((MM_TN, MM_TM), lambda j, i: (j, i)),
        out_shape=jax.ShapeDtypeStruct((n, m), out_dtype),
        compiler_params=_cparams("parallel", "parallel"),
        name=name,
    )(w_t, x)


def _attn_body(q_ref, k_ref, vt_ref, lam_ref, sw_ref, o_ref, m_ref, l_ref, acc_ref, sta_ref, stb_ref, *, lam_init):
    i = pl.program_id(2)
    t = ATT_TK
    q = q_ref[0] * (DA_HEAD_DIM ** -0.5 * math.log2(math.e))
    lane = lax.broadcasted_iota(I32, q.shape, 1)
    q_maps = (_mxu(jnp.where(lane < DA_HEAD_DIM, q, 0.0)), _mxu(jnp.where(lane >= DA_HEAD_DIM, q, 0.0)))
    m_ref[...] = jnp.full_like(m_ref, -jnp.inf)
    l_ref[...] = jnp.zeros_like(l_ref)
    acc_ref[...] = jnp.zeros_like(acc_ref)

    maps = range(2)
    st_bufs = (sta_ref, stb_ref)

    def scores(j, buf):
        k = k_ref[0, pl.ds(pl.multiple_of(j * t, t), t), :]
        for s in maps:
            st_bufs[buf][s] = _nt(k, q_maps[s])

    def accumulate(j, buf, diag_offset=None):
        vt = vt_ref[:, pl.ds(pl.multiple_of(j * t, t), t)]
        st = [st_bufs[buf][s] for s in maps]
        if diag_offset is not None:
            ri = lax.broadcasted_iota(I32, st[0].shape, 0)
            ci = lax.broadcasted_iota(I32, st[0].shape, 1)
            st = [jnp.where(ri + diag_offset <= ci, x, -jnp.inf) for x in st]
        m_old = [m_ref[s] for s in maps]
        m_new = [jnp.maximum(m_old[s], jnp.max(st[s], axis=0, keepdims=True)) for s in maps]
        p = [jnp.exp2(st[s] - m_new[s]) for s in maps]
        pv = [jnp.dot(vt, _mxu(p[s]), preferred_element_type=F32) for s in maps]
        for s in maps:
            corr = jnp.exp2(m_old[s] - m_new[s])
            l_ref[s] = corr * l_ref[s] + jnp.sum(p[s], axis=0, keepdims=True)
            acc_ref[s] = corr * acc_ref[s] + pv[s]
            m_ref[s] = m_new[s]

    scores(0, 0)

    def two_tiles(u, carry):
        j = 2 * u
        scores(j + 1, 1)
        accumulate(j, 0)
        scores(j + 2, 0)
        accumulate(j + 1, 1)
        return carry

    lax.fori_loop(0, i, two_tiles, 0)
    scores(2 * i + 1, 1)
    accumulate(2 * i, 0, diag_offset=0)
    accumulate(2 * i + 1, 1, diag_offset=t)

    lp = lam_ref[...]
    lam = (jnp.exp(jnp.sum(lp[0:1] * lp[1:2], axis=-1, keepdims=True))
           - jnp.exp(jnp.sum(lp[2:3] * lp[3:4], axis=-1, keepdims=True)) + lam_init)
    ot = acc_ref[0] / l_ref[0] - lam * (acc_ref[1] / l_ref[1])
    ms = jnp.mean(ot * ot, axis=0, keepdims=True)
    ot = ot * lax.rsqrt(ms + RMS_EPS) * sw_ref[...] * (1.0 - lam_init)
    o_ref[0] = ot.T.astype(o_ref.dtype)


def _diff_attention(qp, kp, vt, lam_p, subln_w, layer, lam_init):
    b, t, w = qp.shape
    tq, tk = 2 * ATT_TK, ATT_TK
    return pl.pallas_call(
        functools.partial(_attn_body, lam_init=lam_init),
        grid=(b, DA_HEADS, t // tq),
        in_specs=[pl.BlockSpec((1, tq, DA_V_DIM), lambda bi, h, i: (bi, i, h)),
                  pl.BlockSpec((1, t, DA_V_DIM), lambda bi, h, i: (bi, 0, h)),
                  pl.BlockSpec((DA_V_DIM, t), lambda bi, h, i: (h, bi)),
                  pl.BlockSpec((None, 4, DA_HEAD_DIM), lambda bi, h, i: (layer, 0, 0)),
                  pl.BlockSpec((None, DA_V_DIM, 1), lambda bi, h, i: (layer, 0, 0))],
        out_specs=pl.BlockSpec((1, tq, DA_V_DIM), lambda bi, h, i: (bi, i, h)),
        out_shape=jax.ShapeDtypeStruct((b, t, w), BF16),
        scratch_shapes=[pltpu.VMEM((2, 1, tq), F32), pltpu.VMEM((2, 1, tq), F32),
                        pltpu.VMEM((2, DA_V_DIM, tq), F32),
                        pltpu.VMEM((2, tk, tq), F32), pltpu.VMEM((2, tk, tq), F32)],
        compiler_params=_cparams("parallel", "parallel", "parallel"),
        name="diff_attn",
    )(qp, kp, vt, lam_p, subln_w)


def _router_body(h_ref, w_ref, info_ref, gate_ref, cnt_ref, carry_ref):
    tm = h_ref.shape[0]

    @pl.when(pl.program_id(0) == 0)
    def _():
        carry_ref[...] = jnp.zeros_like(carry_ref)

    h = h_ref[...]
    w = w_ref[...]
    h_hi = h.astype(jnp.bfloat16)
    h_lo = (h - h_hi.astype(F32)).astype(jnp.bfloat16)
    w_hi = w.astype(jnp.bfloat16)
    w_lo = (w - w_hi.astype(F32)).astype(jnp.bfloat16)
    logits = (jnp.dot(h_hi, w_hi, preferred_element_type=F32) + jnp.dot(h_lo, w_hi, preferred_element_type=F32)
              + jnp.dot(h_hi, w_lo, preferred_element_type=F32))
    lane = lax.broadcasted_iota(I32, logits.shape, 1)
    big = jnp.int32(LANES)

    def first_max(vals):
        mx = jnp.max(vals, axis=-1, keepdims=True)
        idx = jnp.min(jnp.where(vals == mx, lane, big), axis=-1, keepdims=True)
        return mx, idx

    gl = jnp.where(lane < N_GROUPS, logits, -jnp.inf)
    gmax, gidx = first_max(gl)
    p_group = 1.0 / jnp.sum(jnp.exp(gl - gmax), axis=-1, keepdims=True)
    lo = N_GROUPS + gidx * EXPERTS_PER_GROUP
    el = jnp.where((lane >= lo) & (lane < lo + EXPERTS_PER_GROUP), logits, -jnp.inf)
    m1, i1 = first_max(el)
    m2, i2 = first_max(jnp.where(lane == i1, -jnp.inf, el))
    e2 = jnp.exp(m2 - m1)
    gate1 = p_group / (1.0 + e2)
    gate2 = p_group * e2 / (1.0 + e2)

    oh = ((lane == i1) | (lane == i2)).astype(F32)
    ri = lax.broadcasted_iota(I32, (tm, tm), 0)
    ci = lax.broadcasted_iota(I32, (tm, tm), 1)
    before = jnp.dot((ci < ri).astype(F32), oh, preferred_element_type=F32) + carry_ref[0:1, :]
    rank1 = jnp.sum(jnp.where(lane == i1, before, 0.0), axis=-1, keepdims=True).astype(I32)
    rank2 = jnp.sum(jnp.where(lane == i2, before, 0.0), axis=-1, keepdims=True).astype(I32)
    total = carry_ref[0:1, :] + jnp.sum(oh, axis=0, keepdims=True)
    carry_ref[0:1, :] = total

    info = jnp.where(lane == 0, i1 - N_GROUPS,
                     jnp.where(lane == 1, i2 - N_GROUPS,
                               jnp.where(lane == 2, rank1, jnp.where(lane == 3, rank2, 0))))
    info_ref[...] = info.T[:SUBLANES]
    gate_ref[...] = jnp.where(lane == 0, gate1, jnp.where(lane == 1, gate2, 0.0))
    cnt_ref[...] = jnp.broadcast_to(total, cnt_ref.shape).astype(I32)


def _moe_router(h, w_route):
    m, k = h.shape
    tm = ROUTE_TM
    return pl.pallas_call(
        _router_body,
        grid=(m // tm,),
        in_specs=[pl.BlockSpec((tm, k), lambda i: (i, 0)),
                  pl.BlockSpec((k, LANES), lambda i: (0, 0))],
        out_specs=[pl.BlockSpec((SUBLANES, tm), lambda i: (0, i)),
                   pl.BlockSpec((tm, LANES), lambda i: (i, 0)),
                   pl.BlockSpec((SUBLANES, LANES), lambda i: (0, 0))],
        out_shape=[jax.ShapeDtypeStruct((SUBLANES, m), I32),
                   jax.ShapeDtypeStruct((m, LANES), F32),
                   jax.ShapeDtypeStruct((SUBLANES, LANES), I32)],
        scratch_shapes=[pltpu.VMEM((SUBLANES, LANES), F32)],
        compiler_params=_cparams("arbitrary"),
        name="moe_router",
    )(h, w_route)


ZERO_ROWS = MOE_TB // 2


def _dispatch_body(pos0_ref, pos1_ref, zlo_ref, zhi_ref, x_ref, xs_hbm, zero_ref, sem, zsem):
    i = pl.program_id(0)
    tm = x_ref.shape[0]

    def zero_pieces(act):
        def per_expert(e, carry):
            lo = zlo_ref[e]
            hi = zhi_ref[e]
            n = hi - lo
            size = ZERO_ROWS
            end = hi
            while size >= SUBLANES:
                take = (n & size) != 0

                @pl.when(take)
                def _(end=end, size=size):
                    off = pl.multiple_of(end - size, SUBLANES)
                    act(pltpu.make_async_copy(zero_ref.at[pl.ds(0, size), :], xs_hbm.at[pl.ds(off, size), :], zsem))

                end = end - jnp.where(take, size, 0)
                size //= 2
            for r in range(SUBLANES - 1):
                @pl.when(r < (n & (SUBLANES - 1)))
                def _(r=r):
                    act(pltpu.make_async_copy(zero_ref.at[pl.ds(0, 1), :], xs_hbm.at[pl.ds(lo + r, 1), :], zsem))
            return carry

        lax.fori_loop(0, N_EXPERTS, per_expert, 0)

        def trailing(p, carry):
            off = pl.multiple_of(p * ZERO_ROWS, ZERO_ROWS)
            act(pltpu.make_async_copy(zero_ref, xs_hbm.at[pl.ds(off, ZERO_ROWS), :], zsem))
            return carry

        lax.fori_loop(zlo_ref[N_EXPERTS] // ZERO_ROWS, xs_hbm.shape[0] // ZERO_ROWS, trailing, 0)

    @pl.when(i == 0)
    def _():
        zero_ref[...] = jnp.zeros_like(zero_ref)
        zero_pieces(lambda cp: cp.start())

    def issue(r, carry):
        t = i * tm + r
        src = x_ref.at[pl.ds(r, 1), :]
        pltpu.make_async_copy(src, xs_hbm.at[pl.ds(pos0_ref[t], 1), :], sem).start()
        pltpu.make_async_copy(src, xs_hbm.at[pl.ds(pos1_ref[t], 1), :], sem).start()
        return carry

    lax.fori_loop(0, tm, issue, 0, unroll=8)
    for _ in range(TOP_K):
        pltpu.make_async_copy(x_ref, xs_hbm.at[pl.ds(0, tm), :], sem).wait()

    @pl.when(i == 0)
    def _():
        zero_pieces(lambda cp: cp.wait())


def _moe_dispatch(x, pos0, pos1, zlo, zhi, n_rows):
    m, d = x.shape
    tm = DISP_TM
    return pl.pallas_call(
        _dispatch_body,
        grid_spec=pltpu.PrefetchScalarGridSpec(
            num_scalar_prefetch=4, grid=(m // tm,),
            in_specs=[pl.BlockSpec((tm, d), lambda i, *_: (i, 0))],
            out_specs=pl.BlockSpec(memory_space=pl.ANY),
            scratch_shapes=[pltpu.VMEM((ZERO_ROWS, d), F32), pltpu.SemaphoreType.DMA(()),
                            pltpu.SemaphoreType.DMA(())]),
        out_shape=jax.ShapeDtypeStruct((n_rows, d), F32),
        compiler_params=_cparams("arbitrary"),
        name="moe_dispatch",
    )(pos0, pos1, zlo, zhi, x)


def _expert_body(eid_ref, nused_ref, x_ref, w13_ref, w2_ref, o_ref):
    ff = w2_ref.shape[0]

    used = pl.program_id(0) < nused_ref[0]

    @pl.when(used)
    def _():
        hcat = jnp.dot(x_ref[...], w13_ref[...], preferred_element_type=F32)
        hid = _silu(hcat[:, :ff]) * hcat[:, ff:]
        o_ref[...] = jnp.dot(hid, w2_ref[...], preferred_element_type=F32)

    @pl.when(jnp.logical_not(used))
    def _():
        o_ref[...] = jnp.zeros_like(o_ref)


def _moe_experts(xs, block_eid, n_used, w13, w2, layer):
    n_rows, d = xs.shape
    tb = MOE_TB
    ff = w2.shape[2]

    def row_block(b, eid, nused):
        return (jnp.minimum(b, nused[0] - 1), 0)

    return pl.pallas_call(
        _expert_body,
        grid_spec=pltpu.PrefetchScalarGridSpec(
            num_scalar_prefetch=2, grid=(n_rows // tb,),
            in_specs=[pl.BlockSpec((tb, d), row_block),
                      pl.BlockSpec((None, None, d, 2 * ff), lambda b, eid, nused: (layer, eid[b], 0, 0)),
                      pl.BlockSpec((None, None, ff, d), lambda b, eid, nused: (layer, eid[b], 0, 0))],
            out_specs=pl.BlockSpec((tb, d), lambda b, eid, nused: (b, 0))),
        out_shape=jax.ShapeDtypeStruct((n_rows, d), F32),
        compiler_params=_cparams("arbitrary"),
        name="moe_experts",
    )(block_eid, n_used, xs, w13, w2)


def _combine_body(pos0_ref, pos1_ref, ys_hbm, h_ref, gate_ref, g_ref, b_ref, o_ref, y0_ref, y1_ref, sem0, sem1):
    i = pl.program_id(0)
    tm = h_ref.shape[0]
    slot = i % 2

    def gather(tile, dst):
        def issue(r, carry):
            t = tile * tm + r
            pltpu.make_async_copy(ys_hbm.at[pl.ds(pos0_ref[t], 1), :], y0_ref.at[dst, pl.ds(r, 1), :],
                                  sem0.at[dst]).start()
            pltpu.make_async_copy(ys_hbm.at[pl.ds(pos1_ref[t], 1), :], y1_ref.at[dst, pl.ds(r, 1), :],
                                  sem1.at[dst]).start()
            return carry

        lax.fori_loop(0, tm, issue, 0, unroll=8)

    @pl.when(i == 0)
    def _():
        gather(0, 0)

    @pl.when(i + 1 < pl.num_programs(0))
    def _():
        gather(i + 1, 1 - slot)

    pltpu.make_async_copy(ys_hbm.at[pl.ds(0, tm), :], y0_ref.at[slot], sem0.at[slot]).wait()
    pltpu.make_async_copy(ys_hbm.at[pl.ds(0, tm), :], y1_ref.at[slot], sem1.at[slot]).wait()
    gates = gate_ref[...]
    ffn = gates[:, 0:1] * y0_ref[slot] + gates[:, 1:2] * y1_ref[slot]
    o_ref[...] = _layer_norm(ALPHA * h_ref[...] + ffn, g_ref[...], b_ref[...])


def _moe_combine(ys, pos0, pos1, h, gates, g, b, layer):
    m, d = h.shape
    tm = COMB_TM
    row = pl.BlockSpec((None, 1, d), lambda i, *_: (layer, 0, 0))
    return pl.pallas_call(
        _combine_body,
        grid_spec=pltpu.PrefetchScalarGridSpec(
            num_scalar_prefetch=2, grid=(m // tm,),
            in_specs=[pl.BlockSpec(memory_space=pl.ANY),
                      pl.BlockSpec((tm, d), lambda i, *_: (i, 0)),
                      pl.BlockSpec((tm, LANES), lambda i, *_: (i, 0)), row, row],
            out_specs=pl.BlockSpec((tm, d), lambda i, *_: (i, 0)),
            scratch_shapes=[pltpu.VMEM((2, tm, d), F32), pltpu.VMEM((2, tm, d), F32),
                            pltpu.SemaphoreType.DMA((2,)), pltpu.SemaphoreType.DMA((2,))]),
        out_shape=jax.ShapeDtypeStruct((m, d), F32),
        compiler_params=_cparams("arbitrary"),
        name="moe_combine",
    )(pos0, pos1, ys, h, gates, g, b)


def _moe_layer(h, w_group, w_expert, w13, w2, ln_g, ln_b, layer):
    n, d = h.shape
    tb = MOE_TB
    w_route = jnp.pad(jnp.concatenate([w_group, w_expert], axis=1), ((0, 0), (0, LANES - N_GROUPS - N_EXPERTS)))
    info, gates, cnt = _moe_router(h, w_route)
    counts = cnt[0, N_GROUPS:N_GROUPS + N_EXPERTS]
    padded = (counts + tb - 1) // tb * tb
    pad_end = jnp.cumsum(padded)
    pad_start = pad_end - padded
    pos0 = pad_start[info[0]] + info[2]
    pos1 = pad_start[info[1]] + info[3]
    n_blocks = (n * TOP_K + N_EXPERTS * (tb - 1) + tb - 1) // tb
    n_used = (pad_end[-1] // tb).astype(I32)
    blk = jnp.minimum(jnp.arange(n_blocks, dtype=I32), n_used - 1)
    block_eid = jnp.minimum(jnp.sum(pad_end[None, :] <= (blk * tb)[:, None], axis=1), N_EXPERTS - 1).astype(I32)
    zlo = jnp.concatenate([pad_start + counts, pad_end[-1:]])
    xs = _moe_dispatch(h, pos0, pos1, zlo, pad_end, n_blocks * tb)
    ys = _moe_experts(xs, block_eid, n_used.reshape(1), w13, w2, layer)
    return _moe_combine(ys, pos0, pos1, h, gates, ln_g, ln_b, layer)


def _lambda_init(layer_idx):
    return 0.8 - 0.6 * math.exp(-0.3 * layer_idx)


def kernel(x, a_w_in, a_conv_w, a_a_log, a_dt_bias, a_norm_w, a_w_out, kv_w, b_w_q, b_lambda, b_subln_w, b_w_out,
           ln_mix_g, ln_mix_b, ln_ffn_g, ln_ffn_b, moe_w_group, moe_w_expert, moe_w13, moe_w2):
    b, t, d = x.shape
    n = b * t
    dn_w = DN_HEADS * DN_HEAD_DIM
    h = x.reshape(n, d)

    def per_layer_rows(p):
        return p.reshape(p.shape[0], 1, p.shape[1])

    a_norm_w = per_layer_rows(a_norm_w)
    b_subln_w = b_subln_w.reshape(b_subln_w.shape + (1,))
    ln_mix_g, ln_mix_b = per_layer_rows(ln_mix_g), per_layer_rows(ln_mix_b)
    ln_ffn_g, ln_ffn_b = per_layer_rows(ln_ffn_g), per_layer_rows(ln_ffn_b)
    da_w = DA_HEADS * DA_V_DIM
    w_in, w_out_a = _mxu(a_w_in), _mxu(a_w_out)
    w_k, w_vt = _mxu(kv_w[:, :da_w]).reshape(1, d, da_w), _mxu(kv_w[:, da_w:].T)
    w_q, w_out_b = _mxu(b_w_q), _mxu(b_w_out)
    kp = vt = None
    for layer in range(DEPTH):
        if layer < N_A_LAYERS:
            proj = _matmul(h, w_in, layer, 4 * dn_w, "dn_in_proj").reshape(b, t, 4 * dn_w)
            gb = _dn_gates(h, a_w_in[layer, :, 4 * dn_w:], a_a_log[layer], a_dt_bias[layer]).reshape(b, t, LANES)
            q, k, v = _dn_conv(proj, a_conv_w, layer)
            mix_in = _dn_delta(q, k, v, proj, gb, a_norm_w, layer).reshape(n, dn_w)
            h = _proj_res_ln(mix_in, w_out_a, h, ln_mix_g, ln_mix_b, layer, layer, "dn_out_ln")
        else:
            j = layer - N_A_LAYERS
            if j == 0:
                kp = _matmul(h, w_k, 0, da_w, "k_proj", BF16).reshape(b, t, da_w)
                vt = _matmul_nt(w_vt, h, "v_proj", BF16)
            qp = _matmul(h, w_q, j, da_w, "q_proj").reshape(b, t, da_w)
            mix_in = _diff_attention(qp, kp, vt, b_lambda, b_subln_w, j, _lambda_init(layer)).reshape(n, da_w)
            h = _proj_res_ln(mix_in, w_out_b, h, ln_mix_g, ln_mix_b, j, layer, "da_out_ln")
        h = _moe_layer(h, moe_w_group[layer], moe_w_expert[layer], moe_w13, moe_w2, ln_ffn_g, ln_ffn_b, layer)
    return h.reshape(b, t, d)
```

```python
import functools
import math

import jax
import jax.numpy as jnp
from jax import lax
from jax.experimental import pallas as pl
from jax.experimental.pallas import tpu as pltpu

F32 = jnp.float32
BF16 = jnp.bfloat16
I32 = jnp.int32

DEPTH = 4
N_A_LAYERS = DEPTH // 2
DN_HEADS = 8
DN_HEAD_DIM = 128
CONV_K = 4
DN_CHUNK = 64
DA_HEADS = 8
DA_HEAD_DIM = 64
DA_V_DIM = 2 * DA_HEAD_DIM
N_GROUPS = 4
EXPERTS_PER_GROUP = 8
N_EXPERTS = N_GROUPS * EXPERTS_PER_GROUP
TOP_K = 2
ALPHA = (2 * DEPTH) ** 0.25
LN_EPS = 1e-5
RMS_EPS = 1e-6

LANES = 128
SUBLANES = 8
BF16_SUBLANES = 16
VMEM_LIMIT_BYTES = 56 * 1024 * 1024

MM_TM = 1024
MM_TN = 512
LN_TM = 512
CONV_TT = 256
DN_ROWS = 256
ATT_TK = 512
ROUTE_TM = 512
MOE_TB = 256
DISP_TM = 512
COMB_TM = 256


def _cparams(*sem):
    return pltpu.CompilerParams(dimension_semantics=sem, vmem_limit_bytes=VMEM_LIMIT_BYTES)


def _silu(x):
    return x * (1.0 / (1.0 + jnp.exp(-x)))


def _layer_norm(x, g, b):
    mu = jnp.mean(x, axis=-1, keepdims=True)
    xc = x - mu
    var = jnp.mean(xc * xc, axis=-1, keepdims=True)
    return xc * lax.rsqrt(var + LN_EPS) * g + b


def _mxu(x):
    return x.astype(BF16)


def _mm_body(x_ref, w_ref, o_ref):
    o_ref[...] = jnp.dot(_mxu(x_ref[...]), _mxu(w_ref[...]), preferred_element_type=F32).astype(o_ref.dtype)


def _matmul(x, w, layer, n_out, name, out_dtype=F32):
    m, k = x.shape
    return pl.pallas_call(
        _mm_body,
        grid=(m // MM_TM, n_out // MM_TN),
        in_specs=[pl.BlockSpec((MM_TM, k), lambda i, j: (i, 0)),
                  pl.BlockSpec((None, k, MM_TN), lambda i, j: (layer, 0, j))],
        out_specs=pl.BlockSpec((MM_TM, MM_TN), lambda i, j: (i, j)),
        out_shape=jax.ShapeDtypeStruct((m, n_out), out_dtype),
        compiler_params=_cparams("parallel", "parallel"),
        name=name,
    )(x, w)


def _gates_body(x_ref, w_ref, alog_ref, dtb_ref, o_ref):
    logit = jnp.dot(x_ref[...], w_ref[...], preferred_element_type=F32)
    lane = lax.broadcasted_iota(I32, logit.shape, 1)
    beta = 1.0 / (1.0 + jnp.exp(-logit))
    sp_in = logit + dtb_ref[...]
    softplus = jnp.maximum(sp_in, 0.0) + jnp.log1p(jnp.exp(-jnp.abs(sp_in)))
    g = -jnp.exp(alog_ref[...]) * softplus
    o_ref[...] = jnp.where(lane < DN_HEADS, beta, g)


def _dn_gates(x, w_small, a_log, dt_bias):
    m, k = x.shape
    pad = LANES - 2 * DN_HEADS
    w = jnp.pad(w_small, ((0, 0), (0, pad)))
    alog = jnp.pad(a_log, (DN_HEADS, pad)).reshape(1, LANES)
    dtb = jnp.pad(dt_bias, (DN_HEADS, pad)).reshape(1, LANES)
    return pl.pallas_call(
        _gates_body,
        grid=(m // MM_TM,),
        in_specs=[pl.BlockSpec((MM_TM, k), lambda i: (i, 0)),
                  pl.BlockSpec((k, LANES), lambda i: (0, 0)),
                  pl.BlockSpec((1, LANES), lambda i: (0, 0)),
                  pl.BlockSpec((1, LANES), lambda i: (0, 0))],
        out_specs=pl.BlockSpec((MM_TM, LANES), lambda i: (i, 0)),
        out_shape=jax.ShapeDtypeStruct((m, LANES), F32),
        compiler_params=_cparams("parallel"),
        name="dn_gates",
    )(x, w, alog, dtb)


def _conv_body(xq_ref, xk_ref, xv_ref, hq_ref, hk_ref, hv_ref, w_ref, q_ref, k_ref, v_ref):
    first = pl.program_id(1) == 0
    width = xq_ref.shape[-1]

    def conv_silu(x_ref, halo_ref, col0):
        halo = jnp.where(first, 0.0, halo_ref[0])
        xe = jnp.concatenate([halo, x_ref[0]], axis=0)
        tt = x_ref.shape[1]
        acc = xe[SUBLANES:SUBLANES + tt] * w_ref[CONV_K - 1:CONV_K, col0:col0 + width]
        for j in range(CONV_K - 1):
            shifted = pltpu.roll(xe, CONV_K - 1 - j, axis=0)[SUBLANES:SUBLANES + tt]
            acc = acc + shifted * w_ref[j:j + 1, col0:col0 + width]
        return _silu(acc)

    def l2n(y, scale):
        outs = []
        for h in range(DN_HEADS):
            yh = y[:, h * DN_HEAD_DIM:(h + 1) * DN_HEAD_DIM]
            ss = jnp.sum(yh * yh, axis=-1, keepdims=True)
            outs.append(yh * (lax.rsqrt(ss + RMS_EPS) * scale))
        return jnp.concatenate(outs, axis=-1)

    q_ref[0] = l2n(conv_silu(xq_ref, hq_ref, 0), DN_HEAD_DIM ** -0.5)
    k_ref[0] = l2n(conv_silu(xk_ref, hk_ref, width), 1.0)
    v_ref[0] = conv_silu(xv_ref, hv_ref, 2 * width)


def _dn_conv(proj, conv_w, layer):
    b, t, _ = proj.shape
    w = DN_HEADS * DN_HEAD_DIM
    tt = CONV_TT
    hb = tt // SUBLANES

    def xspec(c):
        return pl.BlockSpec((1, tt, w), lambda bi, i: (bi, i, c))

    def hspec(c):
        return pl.BlockSpec((1, SUBLANES, w), lambda bi, i: (bi, jnp.maximum(i * hb - 1, 0), c))

    out = jax.ShapeDtypeStruct((b, t, w), F32)
    ospec = pl.BlockSpec((1, tt, w), lambda bi, i: (bi, i, 0))
    return pl.pallas_call(
        _conv_body,
        grid=(b, t // tt),
        in_specs=[xspec(0), xspec(1), xspec(2), hspec(0), hspec(1), hspec(2),
                  pl.BlockSpec((None, CONV_K, 3 * w), lambda bi, i: (layer, 0, 0))],
        out_specs=[ospec, ospec, ospec],
        out_shape=[out, out, out],
        compiler_params=_cparams("parallel", "parallel"),
        name="dn_conv",
    )(proj, proj, proj, proj, proj, proj, conv_w)


def _split3(x):
    hi = x.astype(jnp.bfloat16).astype(F32)
    r = x - hi
    mid = r.astype(jnp.bfloat16).astype(F32)
    return hi, mid, r - mid


def _nt(a, b):
    return lax.dot_general(a, b, (((1,), (1,)), ((), ())), preferred_element_type=F32)


def _tn(a, b):
    return lax.dot_general(a, b, (((0,), (0,)), ((), ())), preferred_element_type=F32)


def _delta_body(u_ref, w_ref, qd_ref, kd_ref, attn_ref, egl_ref, z_ref, nw_ref, o_ref, state_ref):
    c = DN_CHUNK
    d = DN_HEAD_DIM

    @pl.when(pl.program_id(0) == 0)
    def _():
        state_ref[...] = jnp.zeros_like(state_ref)

    nw = nw_ref[...]
    nb = u_ref.shape[0]
    chains = [(bi, h) for bi in range(nb) for h in range(DN_HEADS)]

    def chunk(ic, carry):
        rows = pl.ds(pl.multiple_of(ic * c, c), c)
        s_old, ws_qs = {}, {}
        for bi, h in chains:
            cols = slice(h * d, (h + 1) * d)
            s_old[bi, h] = state_ref[bi * DN_HEADS + h]
            lhs = jnp.concatenate([w_ref[bi, rows, cols], qd_ref[bi, rows, cols]], axis=0)
            ws_qs[bi, h] = jnp.dot(lhs, _mxu(s_old[bi, h]), preferred_element_type=F32)
        v_new = {}
        for bi, h in chains:
            cols = slice(h * d, (h + 1) * d)
            v_new[bi, h] = u_ref[bi, rows, cols] - ws_qs[bi, h][:c]
        for bi, h in chains:
            cols = slice(h * d, (h + 1) * d)
            acols = slice(h * c, (h + 1) * c)
            vn = _mxu(v_new[bi, h])
            o = ws_qs[bi, h][c:] + jnp.dot(attn_ref[bi, rows, acols], vn, preferred_element_type=F32)
            gl = egl_ref[bi, pl.ds(ic * DN_HEADS + h, 1), :]
            state_ref[bi * DN_HEADS + h] = s_old[bi, h] * gl + _tn(kd_ref[bi, rows, cols], vn)
            ms = jnp.mean(o * o, axis=-1, keepdims=True)
            o_ref[bi, rows, cols] = (o * lax.rsqrt(ms + RMS_EPS) * nw * _silu(z_ref[bi, rows, cols])).astype(o_ref.dtype)
        return carry

    lax.fori_loop(0, u_ref.shape[1] // c, chunk, 0)


def _dn_prep_body(q_ref, k_ref, v_ref, gb_ref, u_ref, w_ref, qd_ref, kd_ref, attn_ref, egl_ref):
    c = DN_CHUNK
    d = DN_HEAD_DIM
    ri = lax.broadcasted_iota(I32, (c, c), 0)
    ci = lax.broadcasted_iota(I32, (c, c), 1)
    causal = ri >= ci
    strict = ri > ci
    tri = causal.astype(F32)
    heads = range(DN_HEADS)

    def chunk(ic, carry):
        rows = pl.ds(pl.multiple_of(ic * c, c), c)
        gbt = gb_ref[0, rows, :]
        g_hi, g_mid, g_lo = _split3(gbt)
        gcol = (jnp.dot(tri, g_hi, preferred_element_type=F32)
                + jnp.dot(tri, g_mid, preferred_element_type=F32)
                + jnp.dot(tri, g_lo, preferred_element_type=F32))
        grow = gcol.T
        g_last_col = grow[DN_HEADS:2 * DN_HEADS, c - 1:c]
        egl_ref[0, pl.ds(pl.multiple_of(ic * DN_HEADS, DN_HEADS), DN_HEADS), :] = jnp.broadcast_to(
            jnp.exp(g_last_col), (DN_HEADS, d))
        a_mat, x, p = {}, {}, {}
        for h in heads:
            cols = slice(h * d, (h + 1) * d)
            qh = q_ref[0, rows, cols]
            kh = k_ref[0, rows, cols]
            beta_c = gbt[:, h:h + 1]
            gc_c = gcol[:, DN_HEADS + h:DN_HEADS + h + 1]
            gc_r = grow[DN_HEADS + h:DN_HEADS + h + 1, :]
            decay = jnp.exp(jnp.where(causal, gc_c - gc_r, -jnp.inf))
            eg_c = jnp.exp(gc_c)
            k_beta = kh * beta_c
            a_mat[h] = jnp.where(strict, _nt(k_beta, kh) * decay, 0.0)
            attn_ref[0, rows, h * c:(h + 1) * c] = _mxu(jnp.where(causal, _nt(qh, kh) * decay, 0.0))
            qd_ref[0, rows, cols] = _mxu(qh * eg_c)
            kd_ref[0, rows, cols] = _mxu(kh * jnp.exp(gc_r[:, c - 1:c] - gc_c))
            x[h] = jnp.concatenate([v_ref[0, rows, cols] * beta_c, k_beta * eg_c], axis=-1)
        for h in heads:
            x[h] = x[h] - jnp.dot(a_mat[h], x[h], preferred_element_type=F32)
            p[h] = a_mat[h]
        for _ in range(int(math.log2(c)) - 1):
            for h in heads:
                p[h] = jnp.dot(p[h], p[h], preferred_element_type=F32)
            for h in heads:
                x[h] = x[h] + jnp.dot(p[h], x[h], preferred_element_type=F32)
        for h in heads:
            cols = slice(h * d, (h + 1) * d)
            u_ref[0, rows, cols] = x[h][:, :d]
            w_ref[0, rows, cols] = _mxu(x[h][:, d:])
        return carry

    lax.fori_loop(0, q_ref.shape[1] // c, chunk, 0)


def _dn_delta(q, k, v, proj, gb, norm_w, layer):
    b, t, w = q.shape
    rs = DN_ROWS
    cps = rs // DN_CHUNK
    aw = DN_HEADS * DN_CHUNK
    spec = pl.BlockSpec((1, rs, w), lambda bi, i: (bi, i, 0))
    full = jax.ShapeDtypeStruct((b, t, w), F32)
    mxu_only = jax.ShapeDtypeStruct((b, t, w), BF16)
    u, wy, qd, kd, attn, egl = pl.pallas_call(
        _dn_prep_body,
        grid=(b, t // rs),
        in_specs=[spec, spec, spec, pl.BlockSpec((1, rs, LANES), lambda bi, i: (bi, i, 0))],
        out_specs=[spec, spec, spec, spec,
                   pl.BlockSpec((1, rs, aw), lambda bi, i: (bi, i, 0)),
                   pl.BlockSpec((1, cps * DN_HEADS, DN_HEAD_DIM), lambda bi, i: (bi, i, 0))],
        out_shape=[full, mxu_only, mxu_only, mxu_only,
                   jax.ShapeDtypeStruct((b, t, aw), BF16),
                   jax.ShapeDtypeStruct((b, t // DN_CHUNK * DN_HEADS, DN_HEAD_DIM), F32)],
        compiler_params=_cparams("parallel", "parallel"),
        name="dn_prep",
    )(q, k, v, gb)
    bspec = pl.BlockSpec((b, rs, w), lambda i: (0, i, 0))
    return pl.pallas_call(
        _delta_body,
        grid=(t // rs,),
        in_specs=[bspec, bspec, bspec, bspec,
                  pl.BlockSpec((b, rs, aw), lambda i: (0, i, 0)),
                  pl.BlockSpec((b, cps * DN_HEADS, DN_HEAD_DIM), lambda i: (0, i, 0)),
                  pl.BlockSpec((b, rs, w), lambda i: (0, i, 3)),
                  pl.BlockSpec((None, 1, DN_HEAD_DIM), lambda i: (layer, 0, 0))],
        out_specs=bspec,
        out_shape=mxu_only,
        scratch_shapes=[pltpu.VMEM((b * DN_HEADS, DN_HEAD_DIM, DN_HEAD_DIM), F32)],
        compiler_params=_cparams("arbitrary"),
        name="dn_scan",
    )(u, wy, qd, kd, attn, egl, proj, norm_w)


def _proj_ln_body(a_ref, w_ref, h_ref, g_ref, b_ref, o_ref):
    y = jnp.dot(_mxu(a_ref[...]), _mxu(w_ref[...]), preferred_element_type=F32)
    o_ref[...] = _layer_norm(ALPHA * h_ref[...] + y, g_ref[...], b_ref[...])


def _proj_res_ln(a, w, h, g, b, w_layer, ln_layer, name):
    m, k = a.shape
    n = h.shape[1]
    row = pl.BlockSpec((None, 1, n), lambda i: (ln_layer, 0, 0))
    return pl.pallas_call(
        _proj_ln_body,
        grid=(m // LN_TM,),
        in_specs=[pl.BlockSpec((LN_TM, k), lambda i: (i, 0)),
                  pl.BlockSpec((None, k, n), lambda i: (w_layer, 0, 0)),
                  pl.BlockSpec((LN_TM, n), lambda i: (i, 0)), row, row],
        out_specs=pl.BlockSpec((LN_TM, n), lambda i: (i, 0)),
        out_shape=jax.ShapeDtypeStruct((m, n), F32),
        compiler_params=_cparams("parallel"),
        name=name,
    )(a, w, h, g, b)


def _mm_nt_body(w_ref, x_ref, o_ref):
    o_ref[...] = _nt(_mxu(w_ref[...]), _mxu(x_ref[...])).astype(o_ref.dtype)


def _matmul_nt(w_t, x, name, out_dtype=F32):
    n, k = w_t.shape
    m = x.shape[0]
    return pl.pallas_call(
        _mm_nt_body,
        grid=(n // MM_TN, m // MM_TM),
        in_specs=[pl.BlockSpec((MM_TN, k), lambda j, i: (j, 0)),
                  pl.BlockSpec((MM_TM, k), lambda j, i: (i, 0))],
        out_specs=pl.BlockSpec((MM_TN, MM_TM), lambda j, i: (j, i)),
        out_shape=jax.ShapeDtypeStruct((n, m), out_dtype),
        compiler_params=_cparams("parallel", "parallel"),
        name=name,
    )(w_t, x)


def _attn_body(q_ref, k_ref, vt_ref, lam_ref, sw_ref, o_ref, m_ref, l_ref, acc_ref, sta_ref, stb_ref, *, lam_init):
    i = pl.program_id(2)
    t = ATT_TK
    q = q_ref[0] * (DA_HEAD_DIM ** -0.5 * math.log2(math.e))
    lane = lax.broadcasted_iota(I32, q.shape, 1)
    q_maps = (_mxu(jnp.where(lane < DA_HEAD_DIM, q, 0.0)), _mxu(jnp.where(lane >= DA_HEAD_DIM, q, 0.0)))
    m_ref[...] = jnp.full_like(m_ref, -jnp.inf)
    l_ref[...] = jnp.zeros_like(l_ref)
    acc_ref[...] = jnp.zeros_like(acc_ref)

    maps = range(2)
    st_bufs = (sta_ref, stb_ref)

    def scores(j, buf, q0=0):
        k = k_ref[0, pl.ds(pl.multiple_of(j * t, t), t), :]
        for s in maps:
            st_bufs[buf][s, :, q0:] = _nt(k, q_maps[s][q0:])

    def accumulate(j, buf, q0=0, diagonal=False):
        vt = vt_ref[:, pl.ds(pl.multiple_of(j * t, t), t)]
        vt = jnp.concatenate([vt, jnp.ones((BF16_SUBLANES, t), BF16)], axis=0)
        st = [st_bufs[buf][s, :, q0:] for s in maps]
        if diagonal:
            ri = lax.broadcasted_iota(I32, st[0].shape, 0)
            ci = lax.broadcasted_iota(I32, st[0].shape, 1)
            st = [jnp.where(ri <= ci, x, -jnp.inf) for x in st]
        m_old = [m_ref[s, :, q0:] for s in maps]
        m_new = [jnp.maximum(m_old[s], jnp.max(st[s], axis=0, keepdims=True)) for s in maps]
        p = [_mxu(jnp.exp2(st[s] - m_new[s])) for s in maps]
        pv = [jnp.dot(vt, p[s], preferred_element_type=F32) for s in maps]
        for s in maps:
            corr = jnp.exp2(m_old[s] - m_new[s])
            l_ref[s, :, q0:] = corr * l_ref[s, :, q0:] + pv[s][DA_V_DIM:DA_V_DIM + 1]
            acc_ref[s, :, q0:] = corr * acc_ref[s, :, q0:] + pv[s][:DA_V_DIM]
            m_ref[s, :, q0:] = m_new[s]

    scores(0, 0)

    def two_tiles(u, carry):
        j = 2 * u
        scores(j + 1, 1)
        accumulate(j, 0)
        scores(j + 2, 0)
        accumulate(j + 1, 1)
        return carry

    lax.fori_loop(0, i, two_tiles, 0)
    scores(2 * i + 1, 1, q0=t)
    accumulate(2 * i, 0, diagonal=True)
    accumulate(2 * i + 1, 1, q0=t, diagonal=True)

    lp = lam_ref[...]
    lam = (jnp.exp(jnp.sum(lp[0:1] * lp[1:2], axis=-1, keepdims=True))
           - jnp.exp(jnp.sum(lp[2:3] * lp[3:4], axis=-1, keepdims=True)) + lam_init)
    ot = acc_ref[0] / l_ref[0] - lam * (acc_ref[1] / l_ref[1])
    ms = jnp.mean(ot * ot, axis=0, keepdims=True)
    ot = ot * lax.rsqrt(ms + RMS_EPS) * sw_ref[...] * (1.0 - lam_init)
    o_ref[0] = ot.T.astype(o_ref.dtype)


def _diff_attention(qp, kp, vt, lam_p, subln_w, layer, lam_init):
    b, t, w = qp.shape
    tq, tk = 2 * ATT_TK, ATT_TK
    return pl.pallas_call(
        functools.partial(_attn_body, lam_init=lam_init),
        grid=(b, DA_HEADS, t // tq),
        in_specs=[pl.BlockSpec((1, tq, DA_V_DIM), lambda bi, h, i: (bi, i, h)),
                  pl.BlockSpec((1, t, DA_V_DIM), lambda bi, h, i: (bi, 0, h)),
                  pl.BlockSpec((DA_V_DIM, t), lambda bi, h, i: (h, bi)),
                  pl.BlockSpec((None, 4, DA_HEAD_DIM), lambda bi, h, i: (layer, 0, 0)),
                  pl.BlockSpec((None, DA_V_DIM, 1), lambda bi, h, i: (layer, 0, 0))],
        out_specs=pl.BlockSpec((1, tq, DA_V_DIM), lambda bi, h, i: (bi, i, h)),
        out_shape=jax.ShapeDtypeStruct((b, t, w), BF16),
        scratch_shapes=[pltpu.VMEM((2, 1, tq), F32), pltpu.VMEM((2, 1, tq), F32),
                        pltpu.VMEM((2, DA_V_DIM, tq), F32),
                        pltpu.VMEM((2, tk, tq), F32), pltpu.VMEM((2, tk, tq), F32)],
        compiler_params=_cparams("parallel", "parallel", "parallel"),
        name="diff_attn",
    )(qp, kp, vt, lam_p, subln_w)


def _router_body(h_ref, w_ref, info_ref, gate_ref, cnt_ref, carry_ref):
    tm = h_ref.shape[0]

    @pl.when(pl.program_id(0) == 0)
    def _():
        carry_ref[...] = jnp.zeros_like(carry_ref)

    h = h_ref[...]
    w = w_ref[...]
    h_hi = h.astype(jnp.bfloat16)
    h_lo = (h - h_hi.astype(F32)).astype(jnp.bfloat16)
    w_hi = w.astype(jnp.bfloat16)
    w_lo = (w - w_hi.astype(F32)).astype(jnp.bfloat16)
    logits = (jnp.dot(h_hi, w_hi, preferred_element_type=F32) + jnp.dot(h_lo, w_hi, preferred_element_type=F32)
              + jnp.dot(h_hi, w_lo, preferred_element_type=F32))
    lane = lax.broadcasted_iota(I32, logits.shape, 1)
    big = jnp.int32(LANES)

    def first_max(vals):
        mx = jnp.max(vals, axis=-1, keepdims=True)
        idx = jnp.min(jnp.where(vals == mx, lane, big), axis=-1, keepdims=True)
        return mx, idx

    gl = jnp.where(lane < N_GROUPS, logits, -jnp.inf)
    gmax, gidx = first_max(gl)
    p_group = 1.0 / jnp.sum(jnp.exp(gl - gmax), axis=-1, keepdims=True)
    lo = N_GROUPS + gidx * EXPERTS_PER_GROUP
    el = jnp.where((lane >= lo) & (lane < lo + EXPERTS_PER_GROUP), logits, -jnp.inf)
    m1, i1 = first_max(el)
    m2, i2 = first_max(jnp.where(lane == i1, -jnp.inf, el))
    e2 = jnp.exp(m2 - m1)
    gate1 = p_group / (1.0 + e2)
    gate2 = p_group * e2 / (1.0 + e2)

    oh = ((lane == i1) | (lane == i2)).astype(F32)
    ri = lax.broadcasted_iota(I32, (tm, tm), 0)
    ci = lax.broadcasted_iota(I32, (tm, tm), 1)
    before = jnp.dot((ci < ri).astype(F32), oh, preferred_element_type=F32) + carry_ref[0:1, :]
    rank1 = jnp.sum(jnp.where(lane == i1, before, 0.0), axis=-1, keepdims=True).astype(I32)
    rank2 = jnp.sum(jnp.where(lane == i2, before, 0.0), axis=-1, keepdims=True).astype(I32)
    total = carry_ref[0:1, :] + jnp.sum(oh, axis=0, keepdims=True)
    carry_ref[0:1, :] = total

    info = jnp.where(lane == 0, i1 - N_GROUPS,
                     jnp.where(lane == 1, i2 - N_GROUPS,
                               jnp.where(lane == 2, rank1, jnp.where(lane == 3, rank2, 0))))
    info_ref[...] = info.T[:SUBLANES]
    gate_ref[...] = jnp.where(lane == 0, gate1, jnp.where(lane == 1, gate2, 0.0))
    cnt_ref[...] = jnp.broadcast_to(total, cnt_ref.shape).astype(I32)


def _moe_router(h, w_route):
    m, k = h.shape
    tm = ROUTE_TM
    return pl.pallas_call(
        _router_body,
        grid=(m // tm,),
        in_specs=[pl.BlockSpec((tm, k), lambda i: (i, 0)),
                  pl.BlockSpec((k, LANES), lambda i: (0, 0))],
        out_specs=[pl.BlockSpec((SUBLANES, tm), lambda i: (0, i)),
                   pl.BlockSpec((tm, LANES), lambda i: (i, 0)),
                   pl.BlockSpec((SUBLANES, LANES), lambda i: (0, 0))],
        out_shape=[jax.ShapeDtypeStruct((SUBLANES, m), I32),
                   jax.ShapeDtypeStruct((m, LANES), F32),
                   jax.ShapeDtypeStruct((SUBLANES, LANES), I32)],
        scratch_shapes=[pltpu.VMEM((SUBLANES, LANES), F32)],
        compiler_params=_cparams("arbitrary"),
        name="moe_router",
    )(h, w_route)


def _pos_body(start_ref, info_ref, pos_ref):
    info = info_ref[...]
    base = jnp.zeros_like(info)
    for e in range(N_EXPERTS):
        base = jnp.where(info == e, start_ref[e], base)
    pos_ref[...] = base + pltpu.roll(info, SUBLANES - TOP_K, axis=0)


def _moe_positions(info, pad_start):
    return pl.pallas_call(
        _pos_body,
        grid_spec=pltpu.PrefetchScalarGridSpec(
            num_scalar_prefetch=1, grid=(1,),
            in_specs=[pl.BlockSpec(info.shape, lambda i, *_: (0, 0))],
            out_specs=pl.BlockSpec(info.shape, lambda i, *_: (0, 0))),
        out_shape=jax.ShapeDtypeStruct(info.shape, I32),
        compiler_params=_cparams("arbitrary"),
        name="moe_positions",
    )(pad_start, info)


ZERO_ROWS = MOE_TB // 2


def _dispatch_body(pos0_ref, pos1_ref, zlo_ref, zhi_ref, x_ref, xs_hbm, zero_ref, sem, zsem):
    i = pl.program_id(0)
    tm = x_ref.shape[0]

    def zero_pieces(act):
        def per_expert(e, carry):
            lo = zlo_ref[e]
            hi = zhi_ref[e]
            n = hi - lo
            size = ZERO_ROWS
            end = hi
            while size >= SUBLANES:
                take = (n & size) != 0

                @pl.when(take)
                def _(end=end, size=size):
                    off = pl.multiple_of(end - size, SUBLANES)
                    act(pltpu.make_async_copy(zero_ref.at[pl.ds(0, size), :], xs_hbm.at[pl.ds(off, size), :], zsem))

                end = end - jnp.where(take, size, 0)
                size //= 2
            for r in range(SUBLANES - 1):
                @pl.when(r < (n & (SUBLANES - 1)))
                def _(r=r):
                    act(pltpu.make_async_copy(zero_ref.at[pl.ds(0, 1), :], xs_hbm.at[pl.ds(lo + r, 1), :], zsem))
            return carry

        lax.fori_loop(0, N_EXPERTS, per_expert, 0)

        def trailing(p, carry):
            off = pl.multiple_of(p * ZERO_ROWS, ZERO_ROWS)
            act(pltpu.make_async_copy(zero_ref, xs_hbm.at[pl.ds(off, ZERO_ROWS), :], zsem))
            return carry

        lax.fori_loop(zlo_ref[N_EXPERTS] // ZERO_ROWS, xs_hbm.shape[0] // ZERO_ROWS, trailing, 0)

    @pl.when(i == 0)
    def _():
        zero_ref[...] = jnp.zeros_like(zero_ref)
        zero_pieces(lambda cp: cp.start())

    def issue(r, carry):
        t = i * tm + r
        src = x_ref.at[pl.ds(r, 1), :]
        pltpu.make_async_copy(src, xs_hbm.at[pl.ds(pos0_ref[t], 1), :], sem).start()
        pltpu.make_async_copy(src, xs_hbm.at[pl.ds(pos1_ref[t], 1), :], sem).start()
        return carry

    lax.fori_loop(0, tm, issue, 0, unroll=8)
    for _ in range(TOP_K):
        pltpu.make_async_copy(x_ref, xs_hbm.at[pl.ds(0, tm), :], sem).wait()

    @pl.when(i == 0)
    def _():
        zero_pieces(lambda cp: cp.wait())


def _moe_dispatch(x, pos0, pos1, zlo, zhi, n_rows):
    m, d = x.shape
    tm = DISP_TM
    return pl.pallas_call(
        _dispatch_body,
        grid_spec=pltpu.PrefetchScalarGridSpec(
            num_scalar_prefetch=4, grid=(m // tm,),
            in_specs=[pl.BlockSpec((tm, d), lambda i, *_: (i, 0))],
            out_specs=pl.BlockSpec(memory_space=pl.ANY),
            scratch_shapes=[pltpu.VMEM((ZERO_ROWS, d), F32), pltpu.SemaphoreType.DMA(()),
                            pltpu.SemaphoreType.DMA(())]),
        out_shape=jax.ShapeDtypeStruct((n_rows, d), F32),
        compiler_params=_cparams("arbitrary"),
        name="moe_dispatch",
    )(pos0, pos1, zlo, zhi, x)


def _expert_body(eid_ref, nused_ref, x_ref, w13_ref, w2_ref, o_ref):
    ff = w2_ref.shape[0]

    used = pl.program_id(0) < nused_ref[0]

    @pl.when(used)
    def _():
        hcat = jnp.dot(x_ref[...], w13_ref[...], preferred_element_type=F32)
        hid = _silu(hcat[:, :ff]) * hcat[:, ff:]
        o_ref[...] = jnp.dot(hid, w2_ref[...], preferred_element_type=F32)

    @pl.when(jnp.logical_not(used))
    def _():
        o_ref[...] = jnp.zeros_like(o_ref)


def _moe_experts(xs, block_eid, n_used, w13, w2, layer):
    n_rows, d = xs.shape
    tb = MOE_TB
    ff = w2.shape[2]

    def row_block(b, eid, nused):
        return (jnp.minimum(b, nused[0] - 1), 0)

    return pl.pallas_call(
        _expert_body,
        grid_spec=pltpu.PrefetchScalarGridSpec(
            num_scalar_prefetch=2, grid=(n_rows // tb,),
            in_specs=[pl.BlockSpec((tb, d), row_block),
                      pl.BlockSpec((None, None, d, 2 * ff), lambda b, eid, nused: (layer, eid[b], 0, 0)),
                      pl.BlockSpec((None, None, ff, d), lambda b, eid, nused: (layer, eid[b], 0, 0))],
            out_specs=pl.BlockSpec((tb, d), lambda b, eid, nused: (b, 0))),
        out_shape=jax.ShapeDtypeStruct((n_rows, d), F32),
        compiler_params=_cparams("arbitrary"),
        name="moe_experts",
    )(block_eid, n_used, xs, w13, w2)


def _combine_body(pos0_ref, pos1_ref, ys_hbm, h_ref, gate_ref, g_ref, b_ref, o_ref, y0_ref, y1_ref, sem0, sem1):
    i = pl.program_id(0)
    tm = h_ref.shape[0]
    slot = i % 2

    def gather(tile, dst):
        def issue(r, carry):
            t = tile * tm + r
            pltpu.make_async_copy(ys_hbm.at[pl.ds(pos0_ref[t], 1), :], y0_ref.at[dst, pl.ds(r, 1), :],
                                  sem0.at[dst]).start()
            pltpu.make_async_copy(ys_hbm.at[pl.ds(pos1_ref[t], 1), :], y1_ref.at[dst, pl.ds(r, 1), :],
                                  sem1.at[dst]).start()
            return carry

        lax.fori_loop(0, tm, issue, 0, unroll=8)

    @pl.when(i == 0)
    def _():
        gather(0, 0)

    @pl.when(i + 1 < pl.num_programs(0))
    def _():
        gather(i + 1, 1 - slot)

    pltpu.make_async_copy(ys_hbm.at[pl.ds(0, tm), :], y0_ref.at[slot], sem0.at[slot]).wait()
    pltpu.make_async_copy(ys_hbm.at[pl.ds(0, tm), :], y1_ref.at[slot], sem1.at[slot]).wait()
    gates = gate_ref[...]
    ffn = gates[:, 0:1] * y0_ref[slot] + gates[:, 1:2] * y1_ref[slot]
    o_ref[...] = _layer_norm(ALPHA * h_ref[...] + ffn, g_ref[...], b_ref[...])


def _moe_combine(ys, pos0, pos1, h, gates, g, b, layer):
    m, d = h.shape
    tm = COMB_TM
    row = pl.BlockSpec((None, 1, d), lambda i, *_: (layer, 0, 0))
    return pl.pallas_call(
        _combine_body,
        grid_spec=pltpu.PrefetchScalarGridSpec(
            num_scalar_prefetch=2, grid=(m // tm,),
            in_specs=[pl.BlockSpec(memory_space=pl.ANY),
                      pl.BlockSpec((tm, d), lambda i, *_: (i, 0)),
                      pl.BlockSpec((tm, LANES), lambda i, *_: (i, 0)), row, row],
            out_specs=pl.BlockSpec((tm, d), lambda i, *_: (i, 0)),
            scratch_shapes=[pltpu.VMEM((2, tm, d), F32), pltpu.VMEM((2, tm, d), F32),
                            pltpu.SemaphoreType.DMA((2,)), pltpu.SemaphoreType.DMA((2,))]),
        out_shape=jax.ShapeDtypeStruct((m, d), F32),
        compiler_params=_cparams("arbitrary"),
        name="moe_combine",
    )(pos0, pos1, ys, h, gates, g, b)


def _moe_layer(h, w_group, w_expert, w13, w2, ln_g, ln_b, layer):
    n, d = h.shape
    tb = MOE_TB
    w_route = jnp.pad(jnp.concatenate([w_group, w_expert], axis=1), ((0, 0), (0, LANES - N_GROUPS - N_EXPERTS)))
    info, gates, cnt = _moe_router(h, w_route)
    counts = cnt[0, N_GROUPS:N_GROUPS + N_EXPERTS]
    padded = (counts + tb - 1) // tb * tb
    pad_end = jnp.cumsum(padded)
    pad_start = pad_end - padded
    pos = _moe_positions(info, pad_start)
    pos0, pos1 = pos[0], pos[1]
    n_blocks = (n * TOP_K + N_EXPERTS * (tb - 1) + tb - 1) // tb
    n_used = (pad_end[-1] // tb).astype(I32)
    blk = jnp.minimum(jnp.arange(n_blocks, dtype=I32), n_used - 1)
    block_eid = jnp.minimum(jnp.sum(pad_end[None, :] <= (blk * tb)[:, None], axis=1), N_EXPERTS - 1).astype(I32)
    zlo = jnp.concatenate([pad_start + counts, pad_end[-1:]])
    xs = _moe_dispatch(h, pos0, pos1, zlo, pad_end, n_blocks * tb)
    ys = _moe_experts(xs, block_eid, n_used.reshape(1), w13, w2, layer)
    return _moe_combine(ys, pos0, pos1, h, gates, ln_g, ln_b, layer)


def _lambda_init(layer_idx):
    return 0.8 - 0.6 * math.exp(-0.3 * layer_idx)


def kernel(x, a_w_in, a_conv_w, a_a_log, a_dt_bias, a_norm_w, a_w_out, kv_w, b_w_q, b_lambda, b_subln_w, b_w_out,
           ln_mix_g, ln_mix_b, ln_ffn_g, ln_ffn_b, moe_w_group, moe_w_expert, moe_w13, moe_w2):
    b, t, d = x.shape
    n = b * t
    dn_w = DN_HEADS * DN_HEAD_DIM
    h = x.reshape(n, d)

    def per_layer_rows(p):
        return p.reshape(p.shape[0], 1, p.shape[1])

    a_norm_w = per_layer_rows(a_norm_w)
    b_subln_w = b_subln_w.reshape(b_subln_w.shape + (1,))
    ln_mix_g, ln_mix_b = per_layer_rows(ln_mix_g), per_layer_rows(ln_mix_b)
    ln_ffn_g, ln_ffn_b = per_layer_rows(ln_ffn_g), per_layer_rows(ln_ffn_b)
    da_w = DA_HEADS * DA_V_DIM
    w_in, w_out_a = _mxu(a_w_in), _mxu(a_w_out)
    w_k, w_vt = _mxu(kv_w[:, :da_w]).reshape(1, d, da_w), _mxu(kv_w[:, da_w:].T)
    w_q, w_out_b = _mxu(b_w_q), _mxu(b_w_out)
    kp = vt = None
    for layer in range(DEPTH):
        if layer < N_A_LAYERS:
            proj = _matmul(h, w_in, layer, 4 * dn_w, "dn_in_proj").reshape(b, t, 4 * dn_w)
            gb = _dn_gates(h, a_w_in[layer, :, 4 * dn_w:], a_a_log[layer], a_dt_bias[layer]).reshape(b, t, LANES)
            q, k, v = _dn_conv(proj, a_conv_w, layer)
            mix_in = _dn_delta(q, k, v, proj, gb, a_norm_w, layer).reshape(n, dn_w)
            h = _proj_res_ln(mix_in, w_out_a, h, ln_mix_g, ln_mix_b, layer, layer, "dn_out_ln")
        else:
            j = layer - N_A_LAYERS
            if j == 0:
                kp = _matmul(h, w_k, 0, da_w, "k_proj", BF16).reshape(b, t, da_w)
                vt = _matmul_nt(w_vt, h, "v_proj", BF16)
            qp = _matmul(h, w_q, j, da_w, "q_proj").reshape(b, t, da_w)
            mix_in = _diff_attention(qp, kp, vt, b_lambda, b_subln_w, j, _lambda_init(layer)).reshape(n, da_w)
            h = _proj_res_ln(mix_in, w_out_b, h, ln_mix_g, ln_mix_b, j, layer, "da_out_ln")
        h = _moe_layer(h, moe_w_group[layer], moe_w_expert[layer], moe_w13, moe_w2, ln_ffn_g, ln_ffn_b, layer)
    return h.reshape(b, t, d)
```

```python
import functools
import math

import jax
import jax.numpy as jnp
from jax import lax
from jax.experimental import pallas as pl
from jax.experimental.pallas import tpu as pltpu

F32 = jnp.float32
BF16 = jnp.bfloat16
I32 = jnp.int32

DEPTH = 4
N_A_LAYERS = DEPTH // 2
DN_HEADS = 8
DN_HEAD_DIM = 128
CONV_K = 4
DN_CHUNK = 64
DA_HEADS = 8
DA_HEAD_DIM = 64
DA_V_DIM = 2 * DA_HEAD_DIM
N_GROUPS = 4
EXPERTS_PER_GROUP = 8
N_EXPERTS = N_GROUPS * EXPERTS_PER_GROUP
TOP_K = 2
ALPHA = (2 * DEPTH) ** 0.25
LN_EPS = 1e-5
RMS_EPS = 1e-6

LANES = 128
SUBLANES = 8
BF16_SUBLANES = 16
VMEM_LIMIT_BYTES = 56 * 1024 * 1024

MM_TM = 1024
MM_TN = 512
LN_TM = 512
CONV_TT = 256
DN_ROWS = 256
ATT_TK = 512
ROUTE_TM = 512
MOE_TB = 256
COMB_TM = 256


def _cparams(*sem):
    return pltpu.CompilerParams(dimension_semantics=sem, vmem_limit_bytes=VMEM_LIMIT_BYTES)


def _silu(x):
    return x * (1.0 / (1.0 + jnp.exp(-x)))


def _layer_norm(x, g, b):
    mu = jnp.mean(x, axis=-1, keepdims=True)
    xc = x - mu
    var = jnp.mean(xc * xc, axis=-1, keepdims=True)
    return xc * lax.rsqrt(var + LN_EPS) * g + b


def _mxu(x):
    return x.astype(BF16)


def _mm_body(x_ref, w_ref, o_ref):
    o_ref[...] = jnp.dot(_mxu(x_ref[...]), _mxu(w_ref[...]), preferred_element_type=F32).astype(o_ref.dtype)


def _matmul(x, w, layer, n_out, name, out_dtype=F32):
    m, k = x.shape
    return pl.pallas_call(
        _mm_body,
        grid=(m // MM_TM, n_out // MM_TN),
        in_specs=[pl.BlockSpec((MM_TM, k), lambda i, j: (i, 0)),
                  pl.BlockSpec((None, k, MM_TN), lambda i, j: (layer, 0, j))],
        out_specs=pl.BlockSpec((MM_TM, MM_TN), lambda i, j: (i, j)),
        out_shape=jax.ShapeDtypeStruct((m, n_out), out_dtype),
        compiler_params=_cparams("parallel", "parallel"),
        name=name,
    )(x, w)


def _gates_body(x_ref, w_ref, alog_ref, dtb_ref, o_ref):
    logit = jnp.dot(x_ref[...], w_ref[...], preferred_element_type=F32)
    lane = lax.broadcasted_iota(I32, logit.shape, 1)
    beta = 1.0 / (1.0 + jnp.exp(-logit))
    sp_in = logit + dtb_ref[...]
    softplus = jnp.maximum(sp_in, 0.0) + jnp.log1p(jnp.exp(-jnp.abs(sp_in)))
    g = -jnp.exp(alog_ref[...]) * softplus
    o_ref[...] = jnp.where(lane < DN_HEADS, beta, g)


def _dn_gates(x, w_small, a_log, dt_bias):
    m, k = x.shape
    pad = LANES - 2 * DN_HEADS
    w = jnp.pad(w_small, ((0, 0), (0, pad)))
    alog = jnp.pad(a_log, (DN_HEADS, pad)).reshape(1, LANES)
    dtb = jnp.pad(dt_bias, (DN_HEADS, pad)).reshape(1, LANES)
    return pl.pallas_call(
        _gates_body,
        grid=(m // MM_TM,),
        in_specs=[pl.BlockSpec((MM_TM, k), lambda i: (i, 0)),
                  pl.BlockSpec((k, LANES), lambda i: (0, 0)),
                  pl.BlockSpec((1, LANES), lambda i: (0, 0)),
                  pl.BlockSpec((1, LANES), lambda i: (0, 0))],
        out_specs=pl.BlockSpec((MM_TM, LANES), lambda i: (i, 0)),
        out_shape=jax.ShapeDtypeStruct((m, LANES), F32),
        compiler_params=_cparams("parallel"),
        name="dn_gates",
    )(x, w, alog, dtb)


def _conv_body(xq_ref, xk_ref, xv_ref, hq_ref, hk_ref, hv_ref, w_ref, q_ref, k_ref, v_ref):
    first = pl.program_id(1) == 0
    width = xq_ref.shape[-1]

    def conv_silu(x_ref, halo_ref, col0):
        halo = jnp.where(first, 0.0, halo_ref[0])
        xe = jnp.concatenate([halo, x_ref[0]], axis=0)
        tt = x_ref.shape[1]
        acc = xe[SUBLANES:SUBLANES + tt] * w_ref[CONV_K - 1:CONV_K, col0:col0 + width]
        for j in range(CONV_K - 1):
            shifted = pltpu.roll(xe, CONV_K - 1 - j, axis=0)[SUBLANES:SUBLANES + tt]
            acc = acc + shifted * w_ref[j:j + 1, col0:col0 + width]
        return _silu(acc)

    def l2n(y, scale):
        outs = []
        for h in range(DN_HEADS):
            yh = y[:, h * DN_HEAD_DIM:(h + 1) * DN_HEAD_DIM]
            ss = jnp.sum(yh * yh, axis=-1, keepdims=True)
            outs.append(yh * (lax.rsqrt(ss + RMS_EPS) * scale))
        return jnp.concatenate(outs, axis=-1)

    q_ref[0] = l2n(conv_silu(xq_ref, hq_ref, 0), DN_HEAD_DIM ** -0.5)
    k_ref[0] = l2n(conv_silu(xk_ref, hk_ref, width), 1.0)
    v_ref[0] = conv_silu(xv_ref, hv_ref, 2 * width)


def _dn_conv(proj, conv_w, layer):
    b, t, _ = proj.shape
    w = DN_HEADS * DN_HEAD_DIM
    tt = CONV_TT
    hb = tt // SUBLANES

    def xspec(c):
        return pl.BlockSpec((1, tt, w), lambda bi, i: (bi, i, c))

    def hspec(c):
        return pl.BlockSpec((1, SUBLANES, w), lambda bi, i: (bi, jnp.maximum(i * hb - 1, 0), c))

    out = jax.ShapeDtypeStruct((b, t, w), F32)
    ospec = pl.BlockSpec((1, tt, w), lambda bi, i: (bi, i, 0))
    return pl.pallas_call(
        _conv_body,
        grid=(b, t // tt),
        in_specs=[xspec(0), xspec(1), xspec(2), hspec(0), hspec(1), hspec(2),
                  pl.BlockSpec((None, CONV_K, 3 * w), lambda bi, i: (layer, 0, 0))],
        out_specs=[ospec, ospec, ospec],
        out_shape=[out, out, out],
        compiler_params=_cparams("parallel", "parallel"),
        name="dn_conv",
    )(proj, proj, proj, proj, proj, proj, conv_w)


def _split3(x):
    hi = x.astype(jnp.bfloat16).astype(F32)
    r = x - hi
    mid = r.astype(jnp.bfloat16).astype(F32)
    return hi, mid, r - mid


def _nt(a, b):
    return lax.dot_general(a, b, (((1,), (1,)), ((), ())), preferred_element_type=F32)


def _tn(a, b):
    return lax.dot_general(a, b, (((0,), (0,)), ((), ())), preferred_element_type=F32)


def _delta_body(u_ref, w_ref, qd_ref, kd_ref, attn_ref, egl_ref, z_ref, nw_ref, o_ref, state_ref):
    c = DN_CHUNK
    d = DN_HEAD_DIM

    @pl.when(pl.program_id(0) == 0)
    def _():
        state_ref[...] = jnp.zeros_like(state_ref)

    nw = nw_ref[...]
    nb = u_ref.shape[0]
    chains = [(bi, h) for bi in range(nb) for h in range(DN_HEADS)]

    def chunk(ic, carry):
        rows = pl.ds(pl.multiple_of(ic * c, c), c)
        s_old, ws_qs = {}, {}
        for bi, h in chains:
            cols = slice(h * d, (h + 1) * d)
            s_old[bi, h] = state_ref[bi * DN_HEADS + h]
            lhs = jnp.concatenate([w_ref[bi, rows, cols], qd_ref[bi, rows, cols]], axis=0)
            ws_qs[bi, h] = jnp.dot(lhs, _mxu(s_old[bi, h]), preferred_element_type=F32)
        v_new = {}
        for bi, h in chains:
            cols = slice(h * d, (h + 1) * d)
            v_new[bi, h] = u_ref[bi, rows, cols] - ws_qs[bi, h][:c]
        for bi, h in chains:
            cols = slice(h * d, (h + 1) * d)
            acols = slice(h * c, (h + 1) * c)
            vn = _mxu(v_new[bi, h])
            o = ws_qs[bi, h][c:] + jnp.dot(attn_ref[bi, rows, acols], vn, preferred_element_type=F32)
            gl = egl_ref[bi, pl.ds(ic * DN_HEADS + h, 1), :]
            state_ref[bi * DN_HEADS + h] = s_old[bi, h] * gl + _tn(kd_ref[bi, rows, cols], vn)
            ms = jnp.mean(o * o, axis=-1, keepdims=True)
            o_ref[bi, rows, cols] = (o * lax.rsqrt(ms + RMS_EPS) * nw * _silu(z_ref[bi, rows, cols])).astype(o_ref.dtype)
        return carry

    lax.fori_loop(0, u_ref.shape[1] // c, chunk, 0)


def _dn_prep_body(q_ref, k_ref, v_ref, gb_ref, u_ref, w_ref, qd_ref, kd_ref, attn_ref, egl_ref):
    c = DN_CHUNK
    d = DN_HEAD_DIM
    ri = lax.broadcasted_iota(I32, (c, c), 0)
    ci = lax.broadcasted_iota(I32, (c, c), 1)
    causal = ri >= ci
    strict = ri > ci
    tri = causal.astype(F32)
    heads = range(DN_HEADS)

    def chunk(ic, carry):
        rows = pl.ds(pl.multiple_of(ic * c, c), c)
        gbt = gb_ref[0, rows, :]
        g_hi, g_mid, g_lo = _split3(gbt)
        gcol = (jnp.dot(tri, g_hi, preferred_element_type=F32)
                + jnp.dot(tri, g_mid, preferred_element_type=F32)
                + jnp.dot(tri, g_lo, preferred_element_type=F32))
        grow = gcol.T
        g_last_col = grow[DN_HEADS:2 * DN_HEADS, c - 1:c]
        egl_ref[0, pl.ds(pl.multiple_of(ic * DN_HEADS, DN_HEADS), DN_HEADS), :] = jnp.broadcast_to(
            jnp.exp(g_last_col), (DN_HEADS, d))
        a_mat, x, p = {}, {}, {}
        for h in heads:
            cols = slice(h * d, (h + 1) * d)
            qh = q_ref[0, rows, cols]
            kh = k_ref[0, rows, cols]
            beta_c = gbt[:, h:h + 1]
            gc_c = gcol[:, DN_HEADS + h:DN_HEADS + h + 1]
            gc_r = grow[DN_HEADS + h:DN_HEADS + h + 1, :]
            decay = jnp.exp(jnp.where(causal, gc_c - gc_r, -jnp.inf))
            eg_c = jnp.exp(gc_c)
            k_beta = kh * beta_c
            a_mat[h] = jnp.where(strict, _nt(k_beta, kh) * decay, 0.0)
            attn_ref[0, rows, h * c:(h + 1) * c] = _mxu(jnp.where(causal, _nt(qh, kh) * decay, 0.0))
            qd_ref[0, rows, cols] = _mxu(qh * eg_c)
            kd_ref[0, rows, cols] = _mxu(kh * jnp.exp(gc_r[:, c - 1:c] - gc_c))
            x[h] = jnp.concatenate([v_ref[0, rows, cols] * beta_c, k_beta * eg_c], axis=-1)
        for h in heads:
            x[h] = x[h] - jnp.dot(a_mat[h], x[h], preferred_element_type=F32)
            p[h] = a_mat[h]
        for _ in range(int(math.log2(c)) - 1):
            for h in heads:
                p[h] = jnp.dot(p[h], p[h], preferred_element_type=F32)
            for h in heads:
                x[h] = x[h] + jnp.dot(p[h], x[h], preferred_element_type=F32)
        for h in heads:
            cols = slice(h * d, (h + 1) * d)
            u_ref[0, rows, cols] = x[h][:, :d]
            w_ref[0, rows, cols] = _mxu(x[h][:, d:])
        return carry

    lax.fori_loop(0, q_ref.shape[1] // c, chunk, 0)


def _dn_delta(q, k, v, proj, gb, norm_w, layer):
    b, t, w = q.shape
    rs = DN_ROWS
    cps = rs // DN_CHUNK
    aw = DN_HEADS * DN_CHUNK
    spec = pl.BlockSpec((1, rs, w), lambda bi, i: (bi, i, 0))
    full = jax.ShapeDtypeStruct((b, t, w), F32)
    mxu_only = jax.ShapeDtypeStruct((b, t, w), BF16)
    u, wy, qd, kd, attn, egl = pl.pallas_call(
        _dn_prep_body,
        grid=(b, t // rs),
        in_specs=[spec, spec, spec, pl.BlockSpec((1, rs, LANES), lambda bi, i: (bi, i, 0))],
        out_specs=[spec, spec, spec, spec,
                   pl.BlockSpec((1, rs, aw), lambda bi, i: (bi, i, 0)),
                   pl.BlockSpec((1, cps * DN_HEADS, DN_HEAD_DIM), lambda bi, i: (bi, i, 0))],
        out_shape=[full, mxu_only, mxu_only, mxu_only,
                   jax.ShapeDtypeStruct((b, t, aw), BF16),
                   jax.ShapeDtypeStruct((b, t // DN_CHUNK * DN_HEADS, DN_HEAD_DIM), F32)],
        compiler_params=_cparams("parallel", "parallel"),
        name="dn_prep",
    )(q, k, v, gb)
    bspec = pl.BlockSpec((b, rs, w), lambda i: (0, i, 0))
    return pl.pallas_call(
        _delta_body,
        grid=(t // rs,),
        in_specs=[bspec, bspec, bspec, bspec,
                  pl.BlockSpec((b, rs, aw), lambda i: (0, i, 0)),
                  pl.BlockSpec((b, cps * DN_HEADS, DN_HEAD_DIM), lambda i: (0, i, 0)),
                  pl.BlockSpec((b, rs, w), lambda i: (0, i, 3)),
                  pl.BlockSpec((None, 1, DN_HEAD_DIM), lambda i: (layer, 0, 0))],
        out_specs=bspec,
        out_shape=mxu_only,
        scratch_shapes=[pltpu.VMEM((b * DN_HEADS, DN_HEAD_DIM, DN_HEAD_DIM), F32)],
        compiler_params=_cparams("arbitrary"),
        name="dn_scan",
    )(u, wy, qd, kd, attn, egl, proj, norm_w)


def _proj_ln_body(a_ref, w_ref, h_ref, g_ref, b_ref, o_ref):
    y = jnp.dot(_mxu(a_ref[...]), _mxu(w_ref[...]), preferred_element_type=F32)
    o_ref[...] = _layer_norm(ALPHA * h_ref[...] + y, g_ref[...], b_ref[...])


def _proj_res_ln(a, w, h, g, b, w_layer, ln_layer, name):
    m, k = a.shape
    n = h.shape[1]
    row = pl.BlockSpec((None, 1, n), lambda i: (ln_layer, 0, 0))
    return pl.pallas_call(
        _proj_ln_body,
        grid=(m // LN_TM,),
        in_specs=[pl.BlockSpec((LN_TM, k), lambda i: (i, 0)),
                  pl.BlockSpec((None, k, n), lambda i: (w_layer, 0, 0)),
                  pl.BlockSpec((LN_TM, n), lambda i: (i, 0)), row, row],
        out_specs=pl.BlockSpec((LN_TM, n), lambda i: (i, 0)),
        out_shape=jax.ShapeDtypeStruct((m, n), F32),
        compiler_params=_cparams("parallel"),
        name=name,
    )(a, w, h, g, b)


def _mm_nt_body(w_ref, x_ref, o_ref):
    o_ref[...] = _nt(_mxu(w_ref[...]), _mxu(x_ref[...])).astype(o_ref.dtype)


def _matmul_nt(w_t, x, name, out_dtype=F32):
    n, k = w_t.shape
    m = x.shape[0]
    return pl.pallas_call(
        _mm_nt_body,
        grid=(n // MM_TN, m // MM_TM),
        in_specs=[pl.BlockSpec((MM_TN, k), lambda j, i: (j, 0)),
                  pl.BlockSpec((MM_TM, k), lambda j, i: (i, 0))],
        out_specs=pl.BlockSpec((MM_TN, MM_TM), lambda j, i: (j, i)),
        out_shape=jax.ShapeDtypeStruct((n, m), out_dtype),
        compiler_params=_cparams("parallel", "parallel"),
        name=name,
    )(w_t, x)


def _attn_body(q_ref, k_ref, vt_ref, lam_ref, sw_ref, o_ref, m_ref, l_ref, acc_ref, sta_ref, stb_ref, *, lam_init):
    i = pl.program_id(2)
    t = ATT_TK
    q = q_ref[0] * (DA_HEAD_DIM ** -0.5 * math.log2(math.e))
    lane = lax.broadcasted_iota(I32, q.shape, 1)
    q_maps = (_mxu(jnp.where(lane < DA_HEAD_DIM, q, 0.0)), _mxu(jnp.where(lane >= DA_HEAD_DIM, q, 0.0)))
    m_ref[...] = jnp.full_like(m_ref, -jnp.inf)
    l_ref[...] = jnp.zeros_like(l_ref)
    acc_ref[...] = jnp.zeros_like(acc_ref)

    maps = range(2)
    st_bufs = (sta_ref, stb_ref)

    def scores(j, buf, q0=0):
        k = k_ref[0, pl.ds(pl.multiple_of(j * t, t), t), :]
        for s in maps:
            st_bufs[buf][s, :, q0:] = _nt(k, q_maps[s][q0:])

    def accumulate(j, buf, q0=0, diagonal=False):
        vt = vt_ref[:, pl.ds(pl.multiple_of(j * t, t), t)]
        vt = jnp.concatenate([vt, jnp.ones((BF16_SUBLANES, t), BF16)], axis=0)
        st = [st_bufs[buf][s, :, q0:] for s in maps]
        if diagonal:
            ri = lax.broadcasted_iota(I32, st[0].shape, 0)
            ci = lax.broadcasted_iota(I32, st[0].shape, 1)
            st = [jnp.where(ri <= ci, x, -jnp.inf) for x in st]
        m_old = [m_ref[s, :, q0:] for s in maps]
        m_new = [jnp.maximum(m_old[s], jnp.max(st[s], axis=0, keepdims=True)) for s in maps]
        p = [_mxu(jnp.exp2(st[s] - m_new[s])) for s in maps]
        pv = [jnp.dot(vt, p[s], preferred_element_type=F32) for s in maps]
        for s in maps:
            corr = jnp.exp2(m_old[s] - m_new[s])
            l_ref[s, :, q0:] = corr * l_ref[s, :, q0:] + pv[s][DA_V_DIM:DA_V_DIM + 1]
            acc_ref[s, :, q0:] = corr * acc_ref[s, :, q0:] + pv[s][:DA_V_DIM]
            m_ref[s, :, q0:] = m_new[s]

    scores(0, 0)

    def two_tiles(u, carry):
        j = 2 * u
        scores(j + 1, 1)
        accumulate(j, 0)
        scores(j + 2, 0)
        accumulate(j + 1, 1)
        return carry

    lax.fori_loop(0, i, two_tiles, 0)
    scores(2 * i + 1, 1, q0=t)
    accumulate(2 * i, 0, diagonal=True)
    accumulate(2 * i + 1, 1, q0=t, diagonal=True)

    lp = lam_ref[...]
    lam = (jnp.exp(jnp.sum(lp[0:1] * lp[1:2], axis=-1, keepdims=True))
           - jnp.exp(jnp.sum(lp[2:3] * lp[3:4], axis=-1, keepdims=True)) + lam_init)
    ot = acc_ref[0] / l_ref[0] - lam * (acc_ref[1] / l_ref[1])
    ms = jnp.mean(ot * ot, axis=0, keepdims=True)
    ot = ot * lax.rsqrt(ms + RMS_EPS) * sw_ref[...] * (1.0 - lam_init)
    o_ref[0] = ot.T.astype(o_ref.dtype)


def _diff_attention(qp, kp, vt, lam_p, subln_w, layer, lam_init):
    b, t, w = qp.shape
    tq, tk = 2 * ATT_TK, ATT_TK
    return pl.pallas_call(
        functools.partial(_attn_body, lam_init=lam_init),
        grid=(b, DA_HEADS, t // tq),
        in_specs=[pl.BlockSpec((1, tq, DA_V_DIM), lambda bi, h, i: (bi, i, h)),
                  pl.BlockSpec((1, t, DA_V_DIM), lambda bi, h, i: (bi, 0, h)),
                  pl.BlockSpec((DA_V_DIM, t), lambda bi, h, i: (h, bi)),
                  pl.BlockSpec((None, 4, DA_HEAD_DIM), lambda bi, h, i: (layer, 0, 0)),
                  pl.BlockSpec((None, DA_V_DIM, 1), lambda bi, h, i: (layer, 0, 0))],
        out_specs=pl.BlockSpec((1, tq, DA_V_DIM), lambda bi, h, i: (bi, i, h)),
        out_shape=jax.ShapeDtypeStruct((b, t, w), BF16),
        scratch_shapes=[pltpu.VMEM((2, 1, tq), F32), pltpu.VMEM((2, 1, tq), F32),
                        pltpu.VMEM((2, DA_V_DIM, tq), F32),
                        pltpu.VMEM((2, tk, tq), F32), pltpu.VMEM((2, tk, tq), F32)],
        compiler_params=_cparams("parallel", "parallel", "parallel"),
        name="diff_attn",
    )(qp, kp, vt, lam_p, subln_w)


def _router_body(h_ref, w_ref, info_ref, gate_ref, cnt_ref, carry_ref):
    tm = h_ref.shape[0]

    @pl.when(pl.program_id(0) == 0)
    def _():
        carry_ref[...] = jnp.zeros_like(carry_ref)

    h = h_ref[...]
    w = w_ref[...]
    h_hi = h.astype(jnp.bfloat16)
    h_lo = (h - h_hi.astype(F32)).astype(jnp.bfloat16)
    w_hi = w.astype(jnp.bfloat16)
    w_lo = (w - w_hi.astype(F32)).astype(jnp.bfloat16)
    logits = (jnp.dot(h_hi, w_hi, preferred_element_type=F32) + jnp.dot(h_lo, w_hi, preferred_element_type=F32)
              + jnp.dot(h_hi, w_lo, preferred_element_type=F32))
    lane = lax.broadcasted_iota(I32, logits.shape, 1)
    big = jnp.int32(LANES)

    def first_max(vals):
        mx = jnp.max(vals, axis=-1, keepdims=True)
        idx = jnp.min(jnp.where(vals == mx, lane, big), axis=-1, keepdims=True)
        return mx, idx

    gl = jnp.where(lane < N_GROUPS, logits, -jnp.inf)
    gmax, gidx = first_max(gl)
    p_group = 1.0 / jnp.sum(jnp.exp(gl - gmax), axis=-1, keepdims=True)
    lo = N_GROUPS + gidx * EXPERTS_PER_GROUP
    el = jnp.where((lane >= lo) & (lane < lo + EXPERTS_PER_GROUP), logits, -jnp.inf)
    m1, i1 = first_max(el)
    m2, i2 = first_max(jnp.where(lane == i1, -jnp.inf, el))
    e2 = jnp.exp(m2 - m1)
    gate1 = p_group / (1.0 + e2)
    gate2 = p_group * e2 / (1.0 + e2)

    oh = ((lane == i1) | (lane == i2)).astype(F32)
    ri = lax.broadcasted_iota(I32, (tm, tm), 0)
    ci = lax.broadcasted_iota(I32, (tm, tm), 1)
    before = jnp.dot((ci < ri).astype(F32), oh, preferred_element_type=F32) + carry_ref[0:1, :]
    rank1 = jnp.sum(jnp.where(lane == i1, before, 0.0), axis=-1, keepdims=True).astype(I32)
    rank2 = jnp.sum(jnp.where(lane == i2, before, 0.0), axis=-1, keepdims=True).astype(I32)
    total = carry_ref[0:1, :] + jnp.sum(oh, axis=0, keepdims=True)
    carry_ref[0:1, :] = total

    info = jnp.where(lane == 0, i1 - N_GROUPS,
                     jnp.where(lane == 1, i2 - N_GROUPS,
                               jnp.where(lane == 2, rank1, jnp.where(lane == 3, rank2, 0))))
    info_ref[...] = info.T[:SUBLANES]
    gate_ref[...] = jnp.where(lane == 0, gate1, jnp.where(lane == 1, gate2, 0.0))
    cnt_ref[...] = jnp.broadcast_to(total, cnt_ref.shape).astype(I32)


def _moe_router(h, w_route):
    m, k = h.shape
    tm = ROUTE_TM
    return pl.pallas_call(
        _router_body,
        grid=(m // tm,),
        in_specs=[pl.BlockSpec((tm, k), lambda i: (i, 0)),
                  pl.BlockSpec((k, LANES), lambda i: (0, 0))],
        out_specs=[pl.BlockSpec((SUBLANES, tm), lambda i: (0, i)),
                   pl.BlockSpec((tm, LANES), lambda i: (i, 0)),
                   pl.BlockSpec((SUBLANES, LANES), lambda i: (0, 0))],
        out_shape=[jax.ShapeDtypeStruct((SUBLANES, m), I32),
                   jax.ShapeDtypeStruct((m, LANES), F32),
                   jax.ShapeDtypeStruct((SUBLANES, LANES), I32)],
        scratch_shapes=[pltpu.VMEM((SUBLANES, LANES), F32)],
        compiler_params=_cparams("arbitrary"),
        name="moe_router",
    )(h, w_route)


def _pos_body(start_ref, info_ref, pos_ref):
    info = info_ref[...]
    base = jnp.zeros_like(info)
    for e in range(N_EXPERTS):
        base = jnp.where(info == e, start_ref[e], base)
    pos_ref[...] = base + pltpu.roll(info, SUBLANES - TOP_K, axis=0)


def _moe_positions(info, pad_start):
    return pl.pallas_call(
        _pos_body,
        grid_spec=pltpu.PrefetchScalarGridSpec(
            num_scalar_prefetch=1, grid=(1,),
            in_specs=[pl.BlockSpec(info.shape, lambda i, *_: (0, 0))],
            out_specs=pl.BlockSpec(info.shape, lambda i, *_: (0, 0))),
        out_shape=jax.ShapeDtypeStruct(info.shape, I32),
        compiler_params=_cparams("arbitrary"),
        name="moe_positions",
    )(pad_start, info)


def _row_tokens_body(pos0_ref, pos1_ref, rt_ref):
    def clear(r, carry):
        rt_ref[r] = 0
        return carry

    lax.fori_loop(0, rt_ref.shape[0], clear, 0, unroll=8)

    def place(t, carry):
        rt_ref[pos0_ref[t]] = t
        rt_ref[pos1_ref[t]] = t
        return carry

    lax.fori_loop(0, pos0_ref.shape[0], place, 0, unroll=8)


def _moe_row_tokens(pos0, pos1, n_rows):
    return pl.pallas_call(
        _row_tokens_body,
        grid_spec=pltpu.PrefetchScalarGridSpec(
            num_scalar_prefetch=2, grid=(1,), in_specs=[],
            out_specs=pl.BlockSpec(memory_space=pltpu.SMEM)),
        out_shape=jax.ShapeDtypeStruct((n_rows,), I32),
        compiler_params=_cparams("arbitrary"),
        name="moe_row_tokens",
    )(pos0, pos1)


def _expert_body(rt_ref, eid_ref, nused_ref, h_hbm, w13_ref, w2_ref, o_ref, xa_ref, xb_ref, sem):
    b = pl.program_id(0)
    n_used = nused_ref[0]
    tb = o_ref.shape[0]
    ff = w2_ref.shape[0]
    bufs = (xa_ref, xb_ref)

    def start_row(block, dst, r):
        tok = rt_ref[block * tb + r]
        pltpu.make_async_copy(h_hbm.at[pl.ds(tok, 1), :], bufs[dst].at[pl.ds(r, 1), :], sem.at[dst]).start()

    def wait_block(dst):
        pltpu.make_async_copy(h_hbm.at[pl.ds(0, tb), :], bufs[dst], sem.at[dst]).wait()

    @pl.when(b == 0)
    def _():
        def first(r, carry):
            start_row(0, 0, r)
            return carry

        lax.fori_loop(0, tb, first, 0, unroll=8)

    def block(cur):
        wait_block(cur)
        nxt = jnp.minimum(b + 1, n_used - 1)
        for r in range(tb):
            start_row(nxt, 1 - cur, r)
        hcat = jnp.dot(_mxu(bufs[cur][...]), _mxu(w13_ref[...]), preferred_element_type=F32)
        hid = _silu(hcat[:, :ff]) * hcat[:, ff:]
        o_ref[...] = jnp.dot(_mxu(hid), _mxu(w2_ref[...]), preferred_element_type=F32)

        @pl.when(b == n_used - 1)
        def _():
            wait_block(1 - cur)

    used = b < n_used
    for parity in range(2):
        pl.when(used & (b % 2 == parity))(functools.partial(block, parity))

    @pl.when(jnp.logical_not(used))
    def _():
        o_ref[...] = jnp.zeros_like(o_ref)


def _moe_experts(h, row_tok, block_eid, n_used, w13, w2, layer):
    n_rows = row_tok.shape[0]
    d = h.shape[1]
    tb = MOE_TB
    ff = w2.shape[2]
    return pl.pallas_call(
        _expert_body,
        grid_spec=pltpu.PrefetchScalarGridSpec(
            num_scalar_prefetch=3, grid=(n_rows // tb,),
            in_specs=[pl.BlockSpec(memory_space=pl.ANY),
                      pl.BlockSpec((None, None, d, 2 * ff), lambda b, rt, eid, nused: (layer, eid[b], 0, 0)),
                      pl.BlockSpec((None, None, ff, d), lambda b, rt, eid, nused: (layer, eid[b], 0, 0))],
            out_specs=pl.BlockSpec((tb, d), lambda b, rt, eid, nused: (b, 0)),
            scratch_shapes=[pltpu.VMEM((tb, d), F32), pltpu.VMEM((tb, d), F32), pltpu.SemaphoreType.DMA((2,))]),
        out_shape=jax.ShapeDtypeStruct((n_rows, d), F32),
        compiler_params=_cparams("arbitrary"),
        name="moe_experts",
    )(row_tok, block_eid, n_used, h, w13, w2)


def _combine_body(pos0_ref, pos1_ref, ys_hbm, h_ref, gate_ref, g_ref, b_ref, o_ref,
                  y0a_ref, y0b_ref, y1a_ref, y1b_ref, sem0, sem1):
    i = pl.program_id(0)
    last = pl.num_programs(0) - 1
    tm = h_ref.shape[0]
    y0_bufs = (y0a_ref, y0b_ref)
    y1_bufs = (y1a_ref, y1b_ref)

    def start_row(tile, dst, r):
        t = tile * tm + r
        pltpu.make_async_copy(ys_hbm.at[pl.ds(pos0_ref[t], 1), :], y0_bufs[dst].at[pl.ds(r, 1), :],
                              sem0.at[dst]).start()
        pltpu.make_async_copy(ys_hbm.at[pl.ds(pos1_ref[t], 1), :], y1_bufs[dst].at[pl.ds(r, 1), :],
                              sem1.at[dst]).start()

    def wait_tile(dst):
        pltpu.make_async_copy(ys_hbm.at[pl.ds(0, tm), :], y0_bufs[dst], sem0.at[dst]).wait()
        pltpu.make_async_copy(ys_hbm.at[pl.ds(0, tm), :], y1_bufs[dst], sem1.at[dst]).wait()

    @pl.when(i == 0)
    def _():
        def first(r, carry):
            start_row(0, 0, r)
            return carry

        lax.fori_loop(0, tm, first, 0, unroll=8)

    def tile(cur):
        wait_tile(cur)
        nxt = jnp.minimum(i + 1, last)
        for r in range(tm):
            start_row(nxt, 1 - cur, r)
        gates = gate_ref[...]
        ffn = gates[:, 0:1] * y0_bufs[cur][...] + gates[:, 1:2] * y1_bufs[cur][...]
        o_ref[...] = _layer_norm(ALPHA * h_ref[...] + ffn, g_ref[...], b_ref[...])

        @pl.when(i == last)
        def _():
            wait_tile(1 - cur)

    for parity in range(2):
        pl.when(i % 2 == parity)(functools.partial(tile, parity))


def _moe_combine(ys, pos0, pos1, h, gates, g, b, layer):
    m, d = h.shape
    tm = COMB_TM
    row = pl.BlockSpec((None, 1, d), lambda i, *_: (layer, 0, 0))
    return pl.pallas_call(
        _combine_body,
        grid_spec=pltpu.PrefetchScalarGridSpec(
            num_scalar_prefetch=2, grid=(m // tm,),
            in_specs=[pl.BlockSpec(memory_space=pl.ANY),
                      pl.BlockSpec((tm, d), lambda i, *_: (i, 0)),
                      pl.BlockSpec((tm, LANES), lambda i, *_: (i, 0)), row, row],
            out_specs=pl.BlockSpec((tm, d), lambda i, *_: (i, 0)),
            scratch_shapes=[pltpu.VMEM((tm, d), F32), pltpu.VMEM((tm, d), F32),
                            pltpu.VMEM((tm, d), F32), pltpu.VMEM((tm, d), F32),
                            pltpu.SemaphoreType.DMA((2,)), pltpu.SemaphoreType.DMA((2,))]),
        out_shape=jax.ShapeDtypeStruct((m, d), F32),
        compiler_params=_cparams("arbitrary"),
        name="moe_combine",
    )(pos0, pos1, ys, h, gates, g, b)


def _moe_layer(h, w_group, w_expert, w13, w2, ln_g, ln_b, layer):
    n, d = h.shape
    tb = MOE_TB
    w_route = jnp.pad(jnp.concatenate([w_group, w_expert], axis=1), ((0, 0), (0, LANES - N_GROUPS - N_EXPERTS)))
    info, gates, cnt = _moe_router(h, w_route)
    counts = cnt[0, N_GROUPS:N_GROUPS + N_EXPERTS]
    padded = (counts + tb - 1) // tb * tb
    pad_end = jnp.cumsum(padded)
    pad_start = pad_end - padded
    pos = _moe_positions(info, pad_start)
    pos0, pos1 = pos[0], pos[1]
    n_blocks = (n * TOP_K + N_EXPERTS * (tb - 1) + tb - 1) // tb
    n_used = (pad_end[-1] // tb).astype(I32)
    blk = jnp.minimum(jnp.arange(n_blocks, dtype=I32), n_used - 1)
    block_eid = jnp.minimum(jnp.sum(pad_end[None, :] <= (blk * tb)[:, None], axis=1), N_EXPERTS - 1).astype(I32)
    row_tok = _moe_row_tokens(pos0, pos1, n_blocks * tb)
    ys = _moe_experts(h, row_tok, block_eid, n_used.reshape(1), w13, w2, layer)
    return _moe_combine(ys, pos0, pos1, h, gates, ln_g, ln_b, layer)


def _lambda_init(layer_idx):
    return 0.8 - 0.6 * math.exp(-0.3 * layer_idx)


def kernel(x, a_w_in, a_conv_w, a_a_log, a_dt_bias, a_norm_w, a_w_out, kv_w, b_w_q, b_lambda, b_subln_w, b_w_out,
           ln_mix_g, ln_mix_b, ln_ffn_g, ln_ffn_b, moe_w_group, moe_w_expert, moe_w13, moe_w2):
    b, t, d = x.shape
    n = b * t
    dn_w = DN_HEADS * DN_HEAD_DIM
    h = x.reshape(n, d)

    def per_layer_rows(p):
        return p.reshape(p.shape[0], 1, p.shape[1])

    a_norm_w = per_layer_rows(a_norm_w)
    b_subln_w = b_subln_w.reshape(b_subln_w.shape + (1,))
    ln_mix_g, ln_mix_b = per_layer_rows(ln_mix_g), per_layer_rows(ln_mix_b)
    ln_ffn_g, ln_ffn_b = per_layer_rows(ln_ffn_g), per_layer_rows(ln_ffn_b)
    da_w = DA_HEADS * DA_V_DIM
    w_in, w_out_a = _mxu(a_w_in), _mxu(a_w_out)
    w_k, w_vt = _mxu(kv_w[:, :da_w]).reshape(1, d, da_w), _mxu(kv_w[:, da_w:].T)
    w_q, w_out_b = _mxu(b_w_q), _mxu(b_w_out)
    kp = vt = None
    for layer in range(DEPTH):
        if layer < N_A_LAYERS:
            proj = _matmul(h, w_in, layer, 4 * dn_w, "dn_in_proj").reshape(b, t, 4 * dn_w)
            gb = _dn_gates(h, a_w_in[layer, :, 4 * dn_w:], a_a_log[layer], a_dt_bias[layer]).reshape(b, t, LANES)
            q, k, v = _dn_conv(proj, a_conv_w, layer)
            mix_in = _dn_delta(q, k, v, proj, gb, a_norm_w, layer).reshape(n, dn_w)
            h = _proj_res_ln(mix_in, w_out_a, h, ln_mix_g, ln_mix_b, layer, layer, "dn_out_ln")
        else:
            j = layer - N_A_LAYERS
            if j == 0:
                kp = _matmul(h, w_k, 0, da_w, "k_proj", BF16).reshape(b, t, da_w)
                vt = _matmul_nt(w_vt, h, "v_proj", BF16)
            qp = _matmul(h, w_q, j, da_w, "q_proj").reshape(b, t, da_w)
            mix_in = _diff_attention(qp, kp, vt, b_lambda, b_subln_w, j, _lambda_init(layer)).reshape(n, da_w)
            h = _proj_res_ln(mix_in, w_out_b, h, ln_mix_g, ln_mix_b, j, layer, "da_out_ln")
        h = _moe_layer(h, moe_w_group[layer], moe_w_expert[layer], moe_w13, moe_w2, ln_ffn_g, ln_ffn_b, layer)
    return h.reshape(b, t, d)
```

```python
import functools
import math

import jax
import jax.numpy as jnp
from jax import lax
from jax.experimental import pallas as pl
from jax.experimental.pallas import tpu as pltpu

F32 = jnp.float32
BF16 = jnp.bfloat16
I32 = jnp.int32

DEPTH = 4
N_A_LAYERS = DEPTH // 2
DN_HEADS = 8
DN_HEAD_DIM = 128
CONV_K = 4
DN_CHUNK = 64
DA_HEADS = 8
DA_HEAD_DIM = 64
DA_V_DIM = 2 * DA_HEAD_DIM
N_GROUPS = 4
EXPERTS_PER_GROUP = 8
N_EXPERTS = N_GROUPS * EXPERTS_PER_GROUP
TOP_K = 2
ALPHA = (2 * DEPTH) ** 0.25
LN_EPS = 1e-5
RMS_EPS = 1e-6

LANES = 128
SUBLANES = 8
BF16_SUBLANES = 16
VMEM_LIMIT_BYTES = 56 * 1024 * 1024

MM_TM = 2048
MM_TN = 512
LN_TM = 512
CONV_TT = 256
DN_ROWS = 256
ATT_TK = 512
ROUTE_TM = 512
MOE_TB = 256
DISP_TM = 512
COMB_TM = 256


def _cparams(*sem):
    return pltpu.CompilerParams(dimension_semantics=sem, vmem_limit_bytes=VMEM_LIMIT_BYTES)


def _silu(x):
    return x * (1.0 / (1.0 + jnp.exp(-x)))


def _layer_norm(x, g, b):
    mu = jnp.mean(x, axis=-1, keepdims=True)
    xc = x - mu
    var = jnp.mean(xc * xc, axis=-1, keepdims=True)
    return xc * lax.rsqrt(var + LN_EPS) * g + b


def _mxu(x):
    return x.astype(BF16)


def _mm_body(x_ref, w_ref, o_ref):
    o_ref[...] = jnp.dot(_mxu(x_ref[...]), _mxu(w_ref[...]), preferred_element_type=F32).astype(o_ref.dtype)


def _matmul(x, w, layer, n_out, name, out_dtype=F32):
    m, k = x.shape
    return pl.pallas_call(
        _mm_body,
        grid=(m // MM_TM, n_out // MM_TN),
        in_specs=[pl.BlockSpec((MM_TM, k), lambda i, j: (i, 0)),
                  pl.BlockSpec((None, k, MM_TN), lambda i, j: (layer, 0, j))],
        out_specs=pl.BlockSpec((MM_TM, MM_TN), lambda i, j: (i, j)),
        out_shape=jax.ShapeDtypeStruct((m, n_out), out_dtype),
        compiler_params=_cparams("parallel", "parallel"),
        name=name,
    )(x, w)


def _gates_body(x_ref, w_ref, alog_ref, dtb_ref, o_ref):
    logit = jnp.dot(x_ref[...], w_ref[...], preferred_element_type=F32)
    lane = lax.broadcasted_iota(I32, logit.shape, 1)
    beta = 1.0 / (1.0 + jnp.exp(-logit))
    sp_in = logit + dtb_ref[...]
    softplus = jnp.maximum(sp_in, 0.0) + jnp.log1p(jnp.exp(-jnp.abs(sp_in)))
    g = -jnp.exp(alog_ref[...]) * softplus
    o_ref[...] = jnp.where(lane < DN_HEADS, beta, g)


def _dn_gates(x, w_small, a_log, dt_bias):
    m, k = x.shape
    pad = LANES - 2 * DN_HEADS
    w = jnp.pad(w_small, ((0, 0), (0, pad)))
    alog = jnp.pad(a_log, (DN_HEADS, pad)).reshape(1, LANES)
    dtb = jnp.pad(dt_bias, (DN_HEADS, pad)).reshape(1, LANES)
    return pl.pallas_call(
        _gates_body,
        grid=(m // MM_TM,),
        in_specs=[pl.BlockSpec((MM_TM, k), lambda i: (i, 0)),
                  pl.BlockSpec((k, LANES), lambda i: (0, 0)),
                  pl.BlockSpec((1, LANES), lambda i: (0, 0)),
                  pl.BlockSpec((1, LANES), lambda i: (0, 0))],
        out_specs=pl.BlockSpec((MM_TM, LANES), lambda i: (i, 0)),
        out_shape=jax.ShapeDtypeStruct((m, LANES), F32),
        compiler_params=_cparams("parallel"),
        name="dn_gates",
    )(x, w, alog, dtb)


def _conv_body(xq_ref, xk_ref, xv_ref, hq_ref, hk_ref, hv_ref, w_ref, q_ref, k_ref, v_ref):
    first = pl.program_id(1) == 0
    width = xq_ref.shape[-1]

    def conv_silu(x_ref, halo_ref, col0):
        halo = jnp.where(first, 0.0, halo_ref[0].astype(F32))
        xe = jnp.concatenate([halo, x_ref[0].astype(F32)], axis=0)
        tt = x_ref.shape[1]
        h0 = halo_ref.shape[1]
        acc = xe[h0:h0 + tt] * w_ref[CONV_K - 1:CONV_K, col0:col0 + width]
        for j in range(CONV_K - 1):
            shifted = pltpu.roll(xe, CONV_K - 1 - j, axis=0)[h0:h0 + tt]
            acc = acc + shifted * w_ref[j:j + 1, col0:col0 + width]
        return _silu(acc)

    def l2n(y, scale):
        outs = []
        for h in range(DN_HEADS):
            yh = y[:, h * DN_HEAD_DIM:(h + 1) * DN_HEAD_DIM]
            ss = jnp.sum(yh * yh, axis=-1, keepdims=True)
            outs.append(yh * (lax.rsqrt(ss + RMS_EPS) * scale))
        return jnp.concatenate(outs, axis=-1)

    q_ref[0] = l2n(conv_silu(xq_ref, hq_ref, 0), DN_HEAD_DIM ** -0.5).astype(q_ref.dtype)
    k_ref[0] = l2n(conv_silu(xk_ref, hk_ref, width), 1.0).astype(k_ref.dtype)
    v_ref[0] = conv_silu(xv_ref, hv_ref, 2 * width).astype(v_ref.dtype)


def _dn_conv(proj, conv_w, layer):
    b, t, _ = proj.shape
    w = DN_HEADS * DN_HEAD_DIM
    tt = CONV_TT
    halo = BF16_SUBLANES
    hb = tt // halo

    def xspec(c):
        return pl.BlockSpec((1, tt, w), lambda bi, i: (bi, i, c))

    def hspec(c):
        return pl.BlockSpec((1, halo, w), lambda bi, i: (bi, jnp.maximum(i * hb - 1, 0), c))

    out = jax.ShapeDtypeStruct((b, t, w), BF16)
    ospec = pl.BlockSpec((1, tt, w), lambda bi, i: (bi, i, 0))
    return pl.pallas_call(
        _conv_body,
        grid=(b, t // tt),
        in_specs=[xspec(0), xspec(1), xspec(2), hspec(0), hspec(1), hspec(2),
                  pl.BlockSpec((None, CONV_K, 3 * w), lambda bi, i: (layer, 0, 0))],
        out_specs=[ospec, ospec, ospec],
        out_shape=[out, out, out],
        compiler_params=_cparams("parallel", "parallel"),
        name="dn_conv",
    )(proj, proj, proj, proj, proj, proj, conv_w)


def _split3(x):
    hi = x.astype(jnp.bfloat16).astype(F32)
    r = x - hi
    mid = r.astype(jnp.bfloat16).astype(F32)
    return hi, mid, r - mid


def _nt(a, b):
    return lax.dot_general(a, b, (((1,), (1,)), ((), ())), preferred_element_type=F32)


def _tn(a, b):
    return lax.dot_general(a, b, (((0,), (0,)), ((), ())), preferred_element_type=F32)


def _delta_body(u_ref, w_ref, qd_ref, kd_ref, attn_ref, egl_ref, z_ref, nw_ref, o_ref, state_ref):
    c = DN_CHUNK
    d = DN_HEAD_DIM

    @pl.when(pl.program_id(0) == 0)
    def _():
        state_ref[...] = jnp.zeros_like(state_ref)

    nw = nw_ref[...]
    nb = u_ref.shape[0]
    chains = [(bi, h) for bi in range(nb) for h in range(DN_HEADS)]

    def chunk(ic, carry):
        rows = pl.ds(pl.multiple_of(ic * c, c), c)
        s_old, ws_qs = {}, {}
        for bi, h in chains:
            cols = slice(h * d, (h + 1) * d)
            s_old[bi, h] = state_ref[bi * DN_HEADS + h]
            lhs = jnp.concatenate([w_ref[bi, rows, cols], qd_ref[bi, rows, cols]], axis=0)
            ws_qs[bi, h] = jnp.dot(lhs, _mxu(s_old[bi, h]), preferred_element_type=F32)
        v_new = {}
        for bi, h in chains:
            cols = slice(h * d, (h + 1) * d)
            v_new[bi, h] = u_ref[bi, rows, cols] - ws_qs[bi, h][:c]
        for bi, h in chains:
            cols = slice(h * d, (h + 1) * d)
            acols = slice(h * c, (h + 1) * c)
            vn = _mxu(v_new[bi, h])
            o = ws_qs[bi, h][c:] + jnp.dot(attn_ref[bi, rows, acols], vn, preferred_element_type=F32)
            gl = egl_ref[bi, pl.ds(ic * DN_HEADS + h, 1), :]
            state_ref[bi * DN_HEADS + h] = s_old[bi, h] * gl + _tn(kd_ref[bi, rows, cols], vn)
            ms = jnp.mean(o * o, axis=-1, keepdims=True)
            z = z_ref[bi, rows, cols].astype(F32)
            o_ref[bi, rows, cols] = (o * lax.rsqrt(ms + RMS_EPS) * nw * _silu(z)).astype(o_ref.dtype)
        return carry

    lax.fori_loop(0, u_ref.shape[1] // c, chunk, 0)


def _dn_prep_body(q_ref, k_ref, v_ref, gb_ref, u_ref, w_ref, qd_ref, kd_ref, attn_ref, egl_ref):
    c = DN_CHUNK
    d = DN_HEAD_DIM
    ri = lax.broadcasted_iota(I32, (c, c), 0)
    ci = lax.broadcasted_iota(I32, (c, c), 1)
    causal = ri >= ci
    strict = ri > ci
    tri = causal.astype(F32)
    heads = range(DN_HEADS)

    def chunk(ic, carry):
        rows = pl.ds(pl.multiple_of(ic * c, c), c)
        gbt = gb_ref[0, rows, :]
        g_hi, g_mid, g_lo = _split3(gbt)
        gcol = (jnp.dot(tri, g_hi, preferred_element_type=F32)
                + jnp.dot(tri, g_mid, preferred_element_type=F32)
                + jnp.dot(tri, g_lo, preferred_element_type=F32))
        grow = gcol.T
        g_last_col = grow[DN_HEADS:2 * DN_HEADS, c - 1:c]
        egl_ref[0, pl.ds(pl.multiple_of(ic * DN_HEADS, DN_HEADS), DN_HEADS), :] = jnp.broadcast_to(
            jnp.exp(g_last_col), (DN_HEADS, d))
        a_mat, x, p = {}, {}, {}
        for h in heads:
            cols = slice(h * d, (h + 1) * d)
            qh = q_ref[0, rows, cols]
            kh = k_ref[0, rows, cols]
            qf, kf = qh.astype(F32), kh.astype(F32)
            beta_c = gbt[:, h:h + 1]
            gc_c = gcol[:, DN_HEADS + h:DN_HEADS + h + 1]
            gc_r = grow[DN_HEADS + h:DN_HEADS + h + 1, :]
            decay = jnp.exp(jnp.where(causal, gc_c - gc_r, -jnp.inf))
            eg_c = jnp.exp(gc_c)
            k_beta = kf * beta_c
            a_mat[h] = jnp.where(strict, _nt(_mxu(k_beta), kh) * decay, 0.0)
            attn_ref[0, rows, h * c:(h + 1) * c] = _mxu(jnp.where(causal, _nt(qh, kh) * decay, 0.0))
            qd_ref[0, rows, cols] = _mxu(qf * eg_c)
            kd_ref[0, rows, cols] = _mxu(kf * jnp.exp(gc_r[:, c - 1:c] - gc_c))
            x[h] = jnp.concatenate([v_ref[0, rows, cols].astype(F32) * beta_c, k_beta * eg_c], axis=-1)
        for h in heads:
            x[h] = x[h] - jnp.dot(a_mat[h], x[h], preferred_element_type=F32)
            p[h] = a_mat[h]
        for _ in range(int(math.log2(c)) - 1):
            for h in heads:
                p[h] = jnp.dot(p[h], p[h], preferred_element_type=F32)
            for h in heads:
                x[h] = x[h] + jnp.dot(p[h], x[h], preferred_element_type=F32)
        for h in heads:
            cols = slice(h * d, (h + 1) * d)
            u_ref[0, rows, cols] = x[h][:, :d]
            w_ref[0, rows, cols] = _mxu(x[h][:, d:])
        return carry

    lax.fori_loop(0, q_ref.shape[1] // c, chunk, 0)


def _dn_delta(q, k, v, proj, gb, norm_w, layer):
    b, t, w = q.shape
    rs = DN_ROWS
    cps = rs // DN_CHUNK
    aw = DN_HEADS * DN_CHUNK
    spec = pl.BlockSpec((1, rs, w), lambda bi, i: (bi, i, 0))
    full = jax.ShapeDtypeStruct((b, t, w), F32)
    mxu_only = jax.ShapeDtypeStruct((b, t, w), BF16)
    u, wy, qd, kd, attn, egl = pl.pallas_call(
        _dn_prep_body,
        grid=(b, t // rs),
        in_specs=[spec, spec, spec, pl.BlockSpec((1, rs, LANES), lambda bi, i: (bi, i, 0))],
        out_specs=[spec, spec, spec, spec,
                   pl.BlockSpec((1, rs, aw), lambda bi, i: (bi, i, 0)),
                   pl.BlockSpec((1, cps * DN_HEADS, DN_HEAD_DIM), lambda bi, i: (bi, i, 0))],
        out_shape=[full, mxu_only, mxu_only, mxu_only,
                   jax.ShapeDtypeStruct((b, t, aw), BF16),
                   jax.ShapeDtypeStruct((b, t // DN_CHUNK * DN_HEADS, DN_HEAD_DIM), F32)],
        compiler_params=_cparams("parallel", "parallel"),
        name="dn_prep",
    )(q, k, v, gb)
    bspec = pl.BlockSpec((b, rs, w), lambda i: (0, i, 0))
    return pl.pallas_call(
        _delta_body,
        grid=(t // rs,),
        in_specs=[bspec, bspec, bspec, bspec,
                  pl.BlockSpec((b, rs, aw), lambda i: (0, i, 0)),
                  pl.BlockSpec((b, cps * DN_HEADS, DN_HEAD_DIM), lambda i: (0, i, 0)),
                  pl.BlockSpec((b, rs, w), lambda i: (0, i, 3)),
                  pl.BlockSpec((None, 1, DN_HEAD_DIM), lambda i: (layer, 0, 0))],
        out_specs=bspec,
        out_shape=mxu_only,
        scratch_shapes=[pltpu.VMEM((b * DN_HEADS, DN_HEAD_DIM, DN_HEAD_DIM), F32)],
        compiler_params=_cparams("arbitrary"),
        name="dn_scan",
    )(u, wy, qd, kd, attn, egl, proj, norm_w)


def _proj_ln_body(a_ref, w_ref, h_ref, g_ref, b_ref, o_ref):
    y = jnp.dot(_mxu(a_ref[...]), _mxu(w_ref[...]), preferred_element_type=F32)
    o_ref[...] = _layer_norm(ALPHA * h_ref[...] + y, g_ref[...], b_ref[...])


def _proj_res_ln(a, w, h, g, b, w_layer, ln_layer, name):
    m, k = a.shape
    n = h.shape[1]
    row = pl.BlockSpec((None, 1, n), lambda i: (ln_layer, 0, 0))
    return pl.pallas_call(
        _proj_ln_body,
        grid=(m // LN_TM,),
        in_specs=[pl.BlockSpec((LN_TM, k), lambda i: (i, 0)),
                  pl.BlockSpec((None, k, n), lambda i: (w_layer, 0, 0)),
                  pl.BlockSpec((LN_TM, n), lambda i: (i, 0)), row, row],
        out_specs=pl.BlockSpec((LN_TM, n), lambda i: (i, 0)),
        out_shape=jax.ShapeDtypeStruct((m, n), F32),
        compiler_params=_cparams("parallel"),
        name=name,
    )(a, w, h, g, b)


def _mm_nt_body(w_ref, x_ref, o_ref):
    o_ref[...] = _nt(_mxu(w_ref[...]), _mxu(x_ref[...])).astype(o_ref.dtype)


def _matmul_nt(w_t, x, name, out_dtype=F32):
    n, k = w_t.shape
    m = x.shape[0]
    return pl.pallas_call(
        _mm_nt_body,
        grid=(n // MM_TN, m // MM_TM),
        in_specs=[pl.BlockSpec((MM_TN, k), lambda j, i: (j, 0)),
                  pl.BlockSpec((MM_TM, k), lambda j, i: (i, 0))],
        out_specs=pl.BlockSpec((MM_TN, MM_TM), lambda j, i: (j, i)),
        out_shape=jax.ShapeDtypeStruct((n, m), out_dtype),
        compiler_params=_cparams("parallel", "parallel"),
        name=name,
    )(w_t, x)


def _attn_body(q_ref, k_ref, vt_ref, lam_ref, sw_ref, o_ref, m_ref, l_ref, acc_ref, sta_ref, stb_ref, *, lam_init):
    i = pl.program_id(2)
    t = ATT_TK
    q = q_ref[0].astype(F32) * (DA_HEAD_DIM ** -0.5 * math.log2(math.e))
    lane = lax.broadcasted_iota(I32, q.shape, 1)
    q_maps = (_mxu(jnp.where(lane < DA_HEAD_DIM, q, 0.0)), _mxu(jnp.where(lane >= DA_HEAD_DIM, q, 0.0)))
    m_ref[...] = jnp.full_like(m_ref, -jnp.inf)
    l_ref[...] = jnp.zeros_like(l_ref)
    acc_ref[...] = jnp.zeros_like(acc_ref)

    maps = range(2)
    st_bufs = (sta_ref, stb_ref)

    def scores(j, buf, q0=0):
        k = k_ref[0, pl.ds(pl.multiple_of(j * t, t), t), :]
        for s in maps:
            st_bufs[buf][s, :, q0:] = _nt(k, q_maps[s][q0:])

    def accumulate(j, buf, q0=0, diagonal=False):
        vt = vt_ref[:, pl.ds(pl.multiple_of(j * t, t), t)]
        vt = jnp.concatenate([vt, jnp.ones((BF16_SUBLANES, t), BF16)], axis=0)
        st = [st_bufs[buf][s, :, q0:] for s in maps]
        if diagonal:
            ri = lax.broadcasted_iota(I32, st[0].shape, 0)
            ci = lax.broadcasted_iota(I32, st[0].shape, 1)
            st = [jnp.where(ri <= ci, x, -jnp.inf) for x in st]
        m_old = [m_ref[s, :, q0:] for s in maps]
        m_new = [jnp.maximum(m_old[s], jnp.max(st[s], axis=0, keepdims=True)) for s in maps]
        p = [_mxu(jnp.exp2(st[s] - m_new[s])) for s in maps]
        pv = [jnp.dot(vt, p[s], preferred_element_type=F32) for s in maps]
        for s in maps:
            corr = jnp.exp2(m_old[s] - m_new[s])
            l_ref[s, :, q0:] = corr * l_ref[s, :, q0:] + pv[s][DA_V_DIM:DA_V_DIM + 1]
            acc_ref[s, :, q0:] = corr * acc_ref[s, :, q0:] + pv[s][:DA_V_DIM]
            m_ref[s, :, q0:] = m_new[s]

    scores(0, 0)

    def two_tiles(u, carry):
        j = 2 * u
        scores(j + 1, 1)
        accumulate(j, 0)
        scores(j + 2, 0)
        accumulate(j + 1, 1)
        return carry

    lax.fori_loop(0, i, two_tiles, 0)
    scores(2 * i + 1, 1, q0=t)
    accumulate(2 * i, 0, diagonal=True)
    accumulate(2 * i + 1, 1, q0=t, diagonal=True)

    lp = lam_ref[...]
    lam = (jnp.exp(jnp.sum(lp[0:1] * lp[1:2], axis=-1, keepdims=True))
           - jnp.exp(jnp.sum(lp[2:3] * lp[3:4], axis=-1, keepdims=True)) + lam_init)
    ot = acc_ref[0] / l_ref[0] - lam * (acc_ref[1] / l_ref[1])
    ms = jnp.mean(ot * ot, axis=0, keepdims=True)
    ot = ot * lax.rsqrt(ms + RMS_EPS) * sw_ref[...] * (1.0 - lam_init)
    o_ref[0] = ot.T.astype(o_ref.dtype)


def _diff_attention(qp, kp, vt, lam_p, subln_w, layer, lam_init):
    b, t, w = qp.shape
    tq, tk = 2 * ATT_TK, ATT_TK
    return pl.pallas_call(
        functools.partial(_attn_body, lam_init=lam_init),
        grid=(b, DA_HEADS, t // tq),
        in_specs=[pl.BlockSpec((1, tq, DA_V_DIM), lambda bi, h, i: (bi, i, h)),
                  pl.BlockSpec((1, t, DA_V_DIM), lambda bi, h, i: (bi, 0, h)),
                  pl.BlockSpec((DA_V_DIM, t), lambda bi, h, i: (h, bi)),
                  pl.BlockSpec((None, 4, DA_HEAD_DIM), lambda bi, h, i: (layer, 0, 0)),
                  pl.BlockSpec((None, DA_V_DIM, 1), lambda bi, h, i: (layer, 0, 0))],
        out_specs=pl.BlockSpec((1, tq, DA_V_DIM), lambda bi, h, i: (bi, i, h)),
        out_shape=jax.ShapeDtypeStruct((b, t, w), BF16),
        scratch_shapes=[pltpu.VMEM((2, 1, tq), F32), pltpu.VMEM((2, 1, tq), F32),
                        pltpu.VMEM((2, DA_V_DIM, tq), F32),
                        pltpu.VMEM((2, tk, tq), F32), pltpu.VMEM((2, tk, tq), F32)],
        compiler_params=_cparams("parallel", "parallel", "parallel"),
        name="diff_attn",
    )(qp, kp, vt, lam_p, subln_w)


def _router_body(h_ref, w_ref, info_ref, gate_ref, cnt_ref, carry_ref):
    tm = h_ref.shape[0]

    @pl.when(pl.program_id(0) == 0)
    def _():
        carry_ref[...] = jnp.zeros_like(carry_ref)

    h = h_ref[...]
    w = w_ref[...]
    h_hi = h.astype(jnp.bfloat16)
    h_lo = (h - h_hi.astype(F32)).astype(jnp.bfloat16)
    w_hi = w.astype(jnp.bfloat16)
    w_lo = (w - w_hi.astype(F32)).astype(jnp.bfloat16)
    logits = (jnp.dot(h_hi, w_hi, preferred_element_type=F32) + jnp.dot(h_lo, w_hi, preferred_element_type=F32)
              + jnp.dot(h_hi, w_lo, preferred_element_type=F32))
    lane = lax.broadcasted_iota(I32, logits.shape, 1)
    big = jnp.int32(LANES)

    def first_max(vals):
        mx = jnp.max(vals, axis=-1, keepdims=True)
        idx = jnp.min(jnp.where(vals == mx, lane, big), axis=-1, keepdims=True)
        return mx, idx

    gl = jnp.where(lane < N_GROUPS, logits, -jnp.inf)
    gmax, gidx = first_max(gl)
    p_group = 1.0 / jnp.sum(jnp.exp(gl - gmax), axis=-1, keepdims=True)
    lo = N_GROUPS + gidx * EXPERTS_PER_GROUP
    el = jnp.where((lane >= lo) & (lane < lo + EXPERTS_PER_GROUP), logits, -jnp.inf)
    m1, i1 = first_max(el)
    m2, i2 = first_max(jnp.where(lane == i1, -jnp.inf, el))
    e2 = jnp.exp(m2 - m1)
    gate1 = p_group / (1.0 + e2)
    gate2 = p_group * e2 / (1.0 + e2)

    oh = ((lane == i1) | (lane == i2)).astype(F32)
    ri = lax.broadcasted_iota(I32, (tm, tm), 0)
    ci = lax.broadcasted_iota(I32, (tm, tm), 1)
    before = jnp.dot((ci < ri).astype(F32), oh, preferred_element_type=F32) + carry_ref[0:1, :]
    rank1 = jnp.sum(jnp.where(lane == i1, before, 0.0), axis=-1, keepdims=True).astype(I32)
    rank2 = jnp.sum(jnp.where(lane == i2, before, 0.0), axis=-1, keepdims=True).astype(I32)
    total = carry_ref[0:1, :] + jnp.sum(oh, axis=0, keepdims=True)
    carry_ref[0:1, :] = total

    info = jnp.where(lane == 0, i1 - N_GROUPS,
                     jnp.where(lane == 1, i2 - N_GROUPS,
                               jnp.where(lane == 2, rank1, jnp.where(lane == 3, rank2, 0))))
    info_ref[...] = info.T[:SUBLANES]
    gate_ref[...] = jnp.where(lane == 0, gate1, jnp.where(lane == 1, gate2, 0.0))
    cnt_ref[...] = jnp.broadcast_to(total, cnt_ref.shape).astype(I32)


def _moe_router(h, w_route):
    m, k = h.shape
    tm = ROUTE_TM
    return pl.pallas_call(
        _router_body,
        grid=(m // tm,),
        in_specs=[pl.BlockSpec((tm, k), lambda i: (i, 0)),
                  pl.BlockSpec((k, LANES), lambda i: (0, 0))],
        out_specs=[pl.BlockSpec((SUBLANES, tm), lambda i: (0, i)),
                   pl.BlockSpec((tm, LANES), lambda i: (i, 0)),
                   pl.BlockSpec((SUBLANES, LANES), lambda i: (0, 0))],
        out_shape=[jax.ShapeDtypeStruct((SUBLANES, m), I32),
                   jax.ShapeDtypeStruct((m, LANES), F32),
                   jax.ShapeDtypeStruct((SUBLANES, LANES), I32)],
        scratch_shapes=[pltpu.VMEM((SUBLANES, LANES), F32)],
        compiler_params=_cparams("arbitrary"),
        name="moe_router",
    )(h, w_route)


def _pos_body(start_ref, info_ref, pos_ref):
    info = info_ref[...]
    base = jnp.zeros_like(info)
    for e in range(N_EXPERTS):
        base = jnp.where(info == e, start_ref[e], base)
    pos_ref[...] = base + pltpu.roll(info, SUBLANES - TOP_K, axis=0)


def _moe_positions(info, pad_start):
    return pl.pallas_call(
        _pos_body,
        grid_spec=pltpu.PrefetchScalarGridSpec(
            num_scalar_prefetch=1, grid=(1,),
            in_specs=[pl.BlockSpec(info.shape, lambda i, *_: (0, 0))],
            out_specs=pl.BlockSpec(info.shape, lambda i, *_: (0, 0))),
        out_shape=jax.ShapeDtypeStruct(info.shape, I32),
        compiler_params=_cparams("arbitrary"),
        name="moe_positions",
    )(pad_start, info)


ZERO_ROWS = MOE_TB // 2


def _dispatch_body(pos0_ref, pos1_ref, zlo_ref, zhi_ref, x_ref, xs_hbm, zero_ref, sem, zsem):
    i = pl.program_id(0)
    tm = x_ref.shape[0]

    def zero_pieces(act):
        def per_expert(e, carry):
            lo = zlo_ref[e]
            hi = zhi_ref[e]
            n = hi - lo
            size = ZERO_ROWS
            end = hi
            while size >= SUBLANES:
                take = (n & size) != 0

                @pl.when(take)
                def _(end=end, size=size):
                    off = pl.multiple_of(end - size, SUBLANES)
                    act(pltpu.make_async_copy(zero_ref.at[pl.ds(0, size), :], xs_hbm.at[pl.ds(off, size), :], zsem))

                end = end - jnp.where(take, size, 0)
                size //= 2
            for r in range(SUBLANES - 1):
                @pl.when(r < (n & (SUBLANES - 1)))
                def _(r=r):
                    act(pltpu.make_async_copy(zero_ref.at[pl.ds(0, 1), :], xs_hbm.at[pl.ds(lo + r, 1), :], zsem))
            return carry

        lax.fori_loop(0, N_EXPERTS, per_expert, 0)

        def trailing(p, carry):
            off = pl.multiple_of(p * ZERO_ROWS, ZERO_ROWS)
            act(pltpu.make_async_copy(zero_ref, xs_hbm.at[pl.ds(off, ZERO_ROWS), :], zsem))
            return carry

        lax.fori_loop(zlo_ref[N_EXPERTS] // ZERO_ROWS, xs_hbm.shape[0] // ZERO_ROWS, trailing, 0)

    @pl.when(i == 0)
    def _():
        zero_ref[...] = jnp.zeros_like(zero_ref)
        zero_pieces(lambda cp: cp.start())

    def issue(r, carry):
        t = i * tm + r
        src = x_ref.at[pl.ds(r, 1), :]
        pltpu.make_async_copy(src, xs_hbm.at[pl.ds(pos0_ref[t], 1), :], sem).start()
        pltpu.make_async_copy(src, xs_hbm.at[pl.ds(pos1_ref[t], 1), :], sem).start()
        return carry

    lax.fori_loop(0, tm, issue, 0, unroll=8)
    for _ in range(TOP_K):
        pltpu.make_async_copy(x_ref, xs_hbm.at[pl.ds(0, tm), :], sem).wait()

    @pl.when(i == 0)
    def _():
        zero_pieces(lambda cp: cp.wait())


def _moe_dispatch(x, pos0, pos1, zlo, zhi, n_rows):
    m, d = x.shape
    tm = DISP_TM
    return pl.pallas_call(
        _dispatch_body,
        grid_spec=pltpu.PrefetchScalarGridSpec(
            num_scalar_prefetch=4, grid=(m // tm,),
            in_specs=[pl.BlockSpec((tm, d), lambda i, *_: (i, 0))],
            out_specs=pl.BlockSpec(memory_space=pl.ANY),
            scratch_shapes=[pltpu.VMEM((ZERO_ROWS, d), F32), pltpu.SemaphoreType.DMA(()),
                            pltpu.SemaphoreType.DMA(())]),
        out_shape=jax.ShapeDtypeStruct((n_rows, d), F32),
        compiler_params=_cparams("arbitrary"),
        name="moe_dispatch",
    )(pos0, pos1, zlo, zhi, x)


def _expert_body(eid_ref, nused_ref, x_ref, w13_ref, w2_ref, o_ref):
    ff = w2_ref.shape[0]

    used = pl.program_id(0) < nused_ref[0]

    @pl.when(used)
    def _():
        hcat = jnp.dot(x_ref[...], w13_ref[...], preferred_element_type=F32)
        hid = _silu(hcat[:, :ff]) * hcat[:, ff:]
        o_ref[...] = jnp.dot(hid, w2_ref[...], preferred_element_type=F32)

    @pl.when(jnp.logical_not(used))
    def _():
        o_ref[...] = jnp.zeros_like(o_ref)


def _moe_experts(xs, block_eid, n_used, w13, w2, layer):
    n_rows, d = xs.shape
    tb = MOE_TB
    ff = w2.shape[2]

    def row_block(b, eid, nused):
        return (jnp.minimum(b, nused[0] - 1), 0)

    return pl.pallas_call(
        _expert_body,
        grid_spec=pltpu.PrefetchScalarGridSpec(
            num_scalar_prefetch=2, grid=(n_rows // tb,),
            in_specs=[pl.BlockSpec((tb, d), row_block),
                      pl.BlockSpec((None, None, d, 2 * ff), lambda b, eid, nused: (layer, eid[b], 0, 0)),
                      pl.BlockSpec((None, None, ff, d), lambda b, eid, nused: (layer, eid[b], 0, 0))],
            out_specs=pl.BlockSpec((tb, d), lambda b, eid, nused: (b, 0))),
        out_shape=jax.ShapeDtypeStruct((n_rows, d), F32),
        compiler_params=_cparams("arbitrary"),
        name="moe_experts",
    )(block_eid, n_used, xs, w13, w2)


def _combine_body(pos0_ref, pos1_ref, ys_hbm, h_ref, gate_ref, g_ref, b_ref, o_ref, y0_ref, y1_ref, sem0, sem1):
    i = pl.program_id(0)
    tm = h_ref.shape[0]
    slot = i % 2

    def gather(tile, dst):
        def issue(r, carry):
            t = tile * tm + r
            pltpu.make_async_copy(ys_hbm.at[pl.ds(pos0_ref[t], 1), :], y0_ref.at[dst, pl.ds(r, 1), :],
                                  sem0.at[dst]).start()
            pltpu.make_async_copy(ys_hbm.at[pl.ds(pos1_ref[t], 1), :], y1_ref.at[dst, pl.ds(r, 1), :],
                                  sem1.at[dst]).start()
            return carry

        lax.fori_loop(0, tm, issue, 0, unroll=8)

    @pl.when(i == 0)
    def _():
        gather(0, 0)

    @pl.when(i + 1 < pl.num_programs(0))
    def _():
        gather(i + 1, 1 - slot)

    pltpu.make_async_copy(ys_hbm.at[pl.ds(0, tm), :], y0_ref.at[slot], sem0.at[slot]).wait()
    pltpu.make_async_copy(ys_hbm.at[pl.ds(0, tm), :], y1_ref.at[slot], sem1.at[slot]).wait()
    gates = gate_ref[...]
    ffn = gates[:, 0:1] * y0_ref[slot] + gates[:, 1:2] * y1_ref[slot]
    o_ref[...] = _layer_norm(ALPHA * h_ref[...] + ffn, g_ref[...], b_ref[...])


def _moe_combine(ys, pos0, pos1, h, gates, g, b, layer):
    m, d = h.shape
    tm = COMB_TM
    row = pl.BlockSpec((None, 1, d), lambda i, *_: (layer, 0, 0))
    return pl.pallas_call(
        _combine_body,
        grid_spec=pltpu.PrefetchScalarGridSpec(
            num_scalar_prefetch=2, grid=(m // tm,),
            in_specs=[pl.BlockSpec(memory_space=pl.ANY),
                      pl.BlockSpec((tm, d), lambda i, *_: (i, 0)),
                      pl.BlockSpec((tm, LANES), lambda i, *_: (i, 0)), row, row],
            out_specs=pl.BlockSpec((tm, d), lambda i, *_: (i, 0)),
            scratch_shapes=[pltpu.VMEM((2, tm, d), F32), pltpu.VMEM((2, tm, d), F32),
                            pltpu.SemaphoreType.DMA((2,)), pltpu.SemaphoreType.DMA((2,))]),
        out_shape=jax.ShapeDtypeStruct((m, d), F32),
        compiler_params=_cparams("arbitrary"),
        name="moe_combine",
    )(pos0, pos1, ys, h, gates, g, b)


def _moe_layer(h, w_group, w_expert, w13, w2, ln_g, ln_b, layer):
    n, d = h.shape
    tb = MOE_TB
    w_route = jnp.pad(jnp.concatenate([w_group, w_expert], axis=1), ((0, 0), (0, LANES - N_GROUPS - N_EXPERTS)))
    info, gates, cnt = _moe_router(h, w_route)
    counts = cnt[0, N_GROUPS:N_GROUPS + N_EXPERTS]
    padded = (counts + tb - 1) // tb * tb
    pad_end = jnp.cumsum(padded)
    pad_start = pad_end - padded
    pos = _moe_positions(info, pad_start)
    pos0, pos1 = pos[0], pos[1]
    n_blocks = (n * TOP_K + N_EXPERTS * (tb - 1) + tb - 1) // tb
    n_used = (pad_end[-1] // tb).astype(I32)
    blk = jnp.minimum(jnp.arange(n_blocks, dtype=I32), n_used - 1)
    block_eid = jnp.minimum(jnp.sum(pad_end[None, :] <= (blk * tb)[:, None], axis=1), N_EXPERTS - 1).astype(I32)
    zlo = jnp.concatenate([pad_start + counts, pad_end[-1:]])
    xs = _moe_dispatch(h, pos0, pos1, zlo, pad_end, n_blocks * tb)
    ys = _moe_experts(xs, block_eid, n_used.reshape(1), w13, w2, layer)
    return _moe_combine(ys, pos0, pos1, h, gates, ln_g, ln_b, layer)


def _lambda_init(layer_idx):
    return 0.8 - 0.6 * math.exp(-0.3 * layer_idx)


def kernel(x, a_w_in, a_conv_w, a_a_log, a_dt_bias, a_norm_w, a_w_out, kv_w, b_w_q, b_lambda, b_subln_w, b_w_out,
           ln_mix_g, ln_mix_b, ln_ffn_g, ln_ffn_b, moe_w_group, moe_w_expert, moe_w13, moe_w2):
    b, t, d = x.shape
    n = b * t
    dn_w = DN_HEADS * DN_HEAD_DIM
    h = x.reshape(n, d)

    def per_layer_rows(p):
        return p.reshape(p.shape[0], 1, p.shape[1])

    a_norm_w = per_layer_rows(a_norm_w)
    b_subln_w = b_subln_w.reshape(b_subln_w.shape + (1,))
    ln_mix_g, ln_mix_b = per_layer_rows(ln_mix_g), per_layer_rows(ln_mix_b)
    ln_ffn_g, ln_ffn_b = per_layer_rows(ln_ffn_g), per_layer_rows(ln_ffn_b)
    da_w = DA_HEADS * DA_V_DIM
    w_in, w_out_a = _mxu(a_w_in), _mxu(a_w_out)
    w_k, w_vt = _mxu(kv_w[:, :da_w]).reshape(1, d, da_w), _mxu(kv_w[:, da_w:].T)
    w_q, w_out_b = _mxu(b_w_q), _mxu(b_w_out)
    kp = vt = None
    for layer in range(DEPTH):
        if layer < N_A_LAYERS:
            proj = _matmul(h, w_in, layer, 4 * dn_w, "dn_in_proj", BF16).reshape(b, t, 4 * dn_w)
            gb = _dn_gates(h, a_w_in[layer, :, 4 * dn_w:], a_a_log[layer], a_dt_bias[layer]).reshape(b, t, LANES)
            q, k, v = _dn_conv(proj, a_conv_w, layer)
            mix_in = _dn_delta(q, k, v, proj, gb, a_norm_w, layer).reshape(n, dn_w)
            h = _proj_res_ln(mix_in, w_out_a, h, ln_mix_g, ln_mix_b, layer, layer, "dn_out_ln")
        else:
            j = layer - N_A_LAYERS
            if j == 0:
                kp = _matmul(h, w_k, 0, da_w, "k_proj", BF16).reshape(b, t, da_w)
                vt = _matmul_nt(w_vt, h, "v_proj", BF16)
            qp = _matmul(h, w_q, j, da_w, "q_proj", BF16).reshape(b, t, da_w)
            mix_in = _diff_attention(qp, kp, vt, b_lambda, b_subln_w, j, _lambda_init(layer)).reshape(n, da_w)
            h = _proj_res_ln(mix_in, w_out_b, h, ln_mix_g, ln_mix_b, j, layer, "da_out_ln")
        h = _moe_layer(h, moe_w_group[layer], moe_w_expert[layer], moe_w13, moe_w2, ln_ffn_g, ln_ffn_b, layer)
    return h.reshape(b, t, d)
```

```python
import functools
import math

import jax
import jax.numpy as jnp
from jax import lax
from jax.experimental import pallas as pl
from jax.experimental.pallas import tpu as pltpu

F32 = jnp.float32
BF16 = jnp.bfloat16
I32 = jnp.int32

DEPTH = 4
N_A_LAYERS = DEPTH // 2
DN_HEADS = 8
DN_HEAD_DIM = 128
CONV_K = 4
DN_CHUNK = 64
DA_HEADS = 8
DA_HEAD_DIM = 64
DA_V_DIM = 2 * DA_HEAD_DIM
N_GROUPS = 4
EXPERTS_PER_GROUP = 8
N_EXPERTS = N_GROUPS * EXPERTS_PER_GROUP
TOP_K = 2
ALPHA = (2 * DEPTH) ** 0.25
LN_EPS = 1e-5
RMS_EPS = 1e-6

LANES = 128
SUBLANES = 8
BF16_SUBLANES = 16
VMEM_LIMIT_BYTES = 56 * 1024 * 1024

MM_TM = 2048
MM_TN = 512
LN_TM = 512
DN_ROWS = 256
ATT_TK = 512
ATT_HEADS = 2
ROUTE_TM = 512
MOE_TB = 256
DISP_TM = 512
COMB_TM = 256


def _cparams(*sem):
    return pltpu.CompilerParams(dimension_semantics=sem, vmem_limit_bytes=VMEM_LIMIT_BYTES)


def _silu(x):
    return x * (1.0 / (1.0 + jnp.exp(-x)))


def _layer_norm(x, g, b):
    mu = jnp.mean(x, axis=-1, keepdims=True)
    xc = x - mu
    var = jnp.mean(xc * xc, axis=-1, keepdims=True)
    return xc * lax.rsqrt(var + LN_EPS) * g + b


def _mxu(x):
    return x.astype(BF16)


def _mm_body(x_ref, w_ref, o_ref):
    o_ref[...] = jnp.dot(_mxu(x_ref[...]), _mxu(w_ref[...]), preferred_element_type=F32).astype(o_ref.dtype)


def _matmul(x, w, layer, n_out, name, out_dtype=F32, col0=0):
    m, k = x.shape
    j0 = col0 // MM_TN
    return pl.pallas_call(
        _mm_body,
        grid=(m // MM_TM, n_out // MM_TN),
        in_specs=[pl.BlockSpec((MM_TM, k), lambda i, j: (i, 0)),
                  pl.BlockSpec((None, k, MM_TN), lambda i, j: (layer, 0, j0 + j))],
        out_specs=pl.BlockSpec((MM_TM, MM_TN), lambda i, j: (i, j)),
        out_shape=jax.ShapeDtypeStruct((m, n_out), out_dtype),
        compiler_params=_cparams("parallel", "parallel"),
        name=name,
    )(x, w)


def _gates_body(x_ref, w_ref, alog_ref, dtb_ref, o_ref):
    logit = jnp.dot(x_ref[...], w_ref[...], preferred_element_type=F32)
    lane = lax.broadcasted_iota(I32, logit.shape, 1)
    beta = 1.0 / (1.0 + jnp.exp(-logit))
    sp_in = logit + dtb_ref[...]
    softplus = jnp.maximum(sp_in, 0.0) + jnp.log1p(jnp.exp(-jnp.abs(sp_in)))
    g = -jnp.exp(alog_ref[...]) * softplus
    o_ref[...] = jnp.where(lane < DN_HEADS, beta, g)


def _dn_gates(x, w_small, a_log, dt_bias):
    m, k = x.shape
    pad = LANES - 2 * DN_HEADS
    w = jnp.pad(w_small, ((0, 0), (0, pad)))
    alog = jnp.pad(a_log, (DN_HEADS, pad)).reshape(1, LANES)
    dtb = jnp.pad(dt_bias, (DN_HEADS, pad)).reshape(1, LANES)
    return pl.pallas_call(
        _gates_body,
        grid=(m // MM_TM,),
        in_specs=[pl.BlockSpec((MM_TM, k), lambda i: (i, 0)),
                  pl.BlockSpec((k, LANES), lambda i: (0, 0)),
                  pl.BlockSpec((1, LANES), lambda i: (0, 0)),
                  pl.BlockSpec((1, LANES), lambda i: (0, 0))],
        out_specs=pl.BlockSpec((MM_TM, LANES), lambda i: (i, 0)),
        out_shape=jax.ShapeDtypeStruct((m, LANES), F32),
        compiler_params=_cparams("parallel"),
        name="dn_gates",
    )(x, w, alog, dtb)


def _qkv_proj_body(x_ref, w_ref, cw_ref, qkv_ref, halo_ref, *, tiles_per_seq, tiles_per_part):
    i, j = pl.program_id(0), pl.program_id(1)
    proj = jnp.dot(_mxu(x_ref[...]), _mxu(w_ref[...]), preferred_element_type=F32)
    tm, tn = proj.shape
    prev = jnp.where(i % tiles_per_seq == 0, 0.0, halo_ref[j])
    halo_ref[j] = proj[tm - SUBLANES:]
    xe = jnp.concatenate([prev, proj], axis=0)
    acc = proj * cw_ref[CONV_K - 1:CONV_K, :]
    for tap in range(CONV_K - 1):
        shifted = pltpu.roll(xe, CONV_K - 1 - tap, axis=0)[SUBLANES:]
        acc = acc + shifted * cw_ref[tap:tap + 1, :]
    y = _silu(acc)
    normed = []
    for h in range(tn // DN_HEAD_DIM):
        yh = y[:, h * DN_HEAD_DIM:(h + 1) * DN_HEAD_DIM]
        ss = jnp.sum(yh * yh, axis=-1, keepdims=True)
        normed.append(yh * lax.rsqrt(ss + RMS_EPS))
    normed = jnp.concatenate(normed, axis=-1)
    q_scale = jnp.where(j < tiles_per_part, DN_HEAD_DIM ** -0.5, 1.0)
    qkv_ref[...] = jnp.where(j < 2 * tiles_per_part, normed * q_scale, y).astype(qkv_ref.dtype)


def _dn_qkv_proj(x, w_in, conv_w, layer, seq_len):
    m, k = x.shape
    width = DN_HEADS * DN_HEAD_DIM
    tm, tn = MM_TM, MM_TN
    conv_tiles = 3 * width // tn
    body = functools.partial(_qkv_proj_body, tiles_per_seq=seq_len // tm, tiles_per_part=width // tn)
    return pl.pallas_call(
        body,
        grid=(m // tm, conv_tiles),
        in_specs=[pl.BlockSpec((tm, k), lambda i, j: (i, 0)),
                  pl.BlockSpec((None, k, tn), lambda i, j: (layer, 0, j)),
                  pl.BlockSpec((None, CONV_K, tn), lambda i, j: (layer, 0, j))],
        out_specs=pl.BlockSpec((tm, tn), lambda i, j: (i, j)),
        out_shape=jax.ShapeDtypeStruct((m, 3 * width), BF16),
        scratch_shapes=[pltpu.VMEM((conv_tiles, SUBLANES, tn), F32)],
        compiler_params=_cparams("arbitrary", "arbitrary"),
        name="dn_qkv_proj",
    )(x, w_in, conv_w)


def _split3(x):
    hi = x.astype(jnp.bfloat16).astype(F32)
    r = x - hi
    mid = r.astype(jnp.bfloat16).astype(F32)
    return hi, mid, r - mid


def _nt(a, b):
    return lax.dot_general(a, b, (((1,), (1,)), ((), ())), preferred_element_type=F32)


def _tn(a, b):
    return lax.dot_general(a, b, (((0,), (0,)), ((), ())), preferred_element_type=F32)


def _delta_body(u_ref, w_ref, qd_ref, kd_ref, attn_ref, egl_ref, z_ref, nw_ref, o_ref, state_ref):
    c = DN_CHUNK
    d = DN_HEAD_DIM

    @pl.when(pl.program_id(0) == 0)
    def _():
        state_ref[...] = jnp.zeros_like(state_ref)

    nw = nw_ref[...]
    nb = u_ref.shape[0]
    chains = [(bi, h) for bi in range(nb) for h in range(DN_HEADS)]

    def chunk(ic, carry):
        rows = pl.ds(pl.multiple_of(ic * c, c), c)
        s_old, ws_qs = {}, {}
        for bi, h in chains:
            cols = slice(h * d, (h + 1) * d)
            s_old[bi, h] = state_ref[bi * DN_HEADS + h]
            lhs = jnp.concatenate([w_ref[bi, rows, cols], qd_ref[bi, rows, cols]], axis=0)
            ws_qs[bi, h] = jnp.dot(lhs, _mxu(s_old[bi, h]), preferred_element_type=F32)
        v_new = {}
        for bi, h in chains:
            cols = slice(h * d, (h + 1) * d)
            v_new[bi, h] = u_ref[bi, rows, cols] - ws_qs[bi, h][:c]
        for bi, h in chains:
            cols = slice(h * d, (h + 1) * d)
            acols = slice(h * c, (h + 1) * c)
            vn = _mxu(v_new[bi, h])
            o = ws_qs[bi, h][c:] + jnp.dot(attn_ref[bi, rows, acols], vn, preferred_element_type=F32)
            gl = egl_ref[bi, pl.ds(ic * DN_HEADS + h, 1), :]
            state_ref[bi * DN_HEADS + h] = s_old[bi, h] * gl + _tn(kd_ref[bi, rows, cols], vn)
            ms = jnp.mean(o * o, axis=-1, keepdims=True)
            z = z_ref[bi, rows, cols].astype(F32)
            o_ref[bi, rows, cols] = (o * lax.rsqrt(ms + RMS_EPS) * nw * _silu(z)).astype(o_ref.dtype)
        return carry

    lax.fori_loop(0, u_ref.shape[1] // c, chunk, 0)


def _dn_prep_body(q_ref, k_ref, v_ref, gb_ref, u_ref, w_ref, qd_ref, kd_ref, attn_ref, egl_ref):
    c = DN_CHUNK
    d = DN_HEAD_DIM
    ri = lax.broadcasted_iota(I32, (c, c), 0)
    ci = lax.broadcasted_iota(I32, (c, c), 1)
    causal = ri >= ci
    strict = ri > ci
    tri = causal.astype(F32)
    heads = range(DN_HEADS)

    def chunk(ic, carry):
        rows = pl.ds(pl.multiple_of(ic * c, c), c)
        gbt = gb_ref[0, rows, :]
        g_hi, g_mid, g_lo = _split3(gbt)
        gcol = (jnp.dot(tri, g_hi, preferred_element_type=F32)
                + jnp.dot(tri, g_mid, preferred_element_type=F32)
                + jnp.dot(tri, g_lo, preferred_element_type=F32))
        grow = gcol.T
        g_last_col = grow[DN_HEADS:2 * DN_HEADS, c - 1:c]
        egl_ref[0, pl.ds(pl.multiple_of(ic * DN_HEADS, DN_HEADS), DN_HEADS), :] = jnp.broadcast_to(
            jnp.exp(g_last_col), (DN_HEADS, d))
        a_mat, x, p = {}, {}, {}
        for h in heads:
            cols = slice(h * d, (h + 1) * d)
            qh = q_ref[0, rows, cols]
            kh = k_ref[0, rows, cols]
            qf, kf = qh.astype(F32), kh.astype(F32)
            beta_c = gbt[:, h:h + 1]
            gc_c = gcol[:, DN_HEADS + h:DN_HEADS + h + 1]
            gc_r = grow[DN_HEADS + h:DN_HEADS + h + 1, :]
            decay = jnp.exp(jnp.where(causal, gc_c - gc_r, -jnp.inf))
            eg_c = jnp.exp(gc_c)
            k_beta = kf * beta_c
            a_mat[h] = jnp.where(strict, _nt(_mxu(k_beta), kh) * decay, 0.0)
            attn_ref[0, rows, h * c:(h + 1) * c] = _mxu(jnp.where(causal, _nt(qh, kh) * decay, 0.0))
            qd_ref[0, rows, cols] = _mxu(qf * eg_c)
            kd_ref[0, rows, cols] = _mxu(kf * jnp.exp(gc_r[:, c - 1:c] - gc_c))
            x[h] = jnp.concatenate([v_ref[0, rows, cols].astype(F32) * beta_c, k_beta * eg_c], axis=-1)
        for h in heads:
            x[h] = x[h] - jnp.dot(a_mat[h], x[h], preferred_element_type=F32)
            p[h] = a_mat[h]
        for _ in range(int(math.log2(c)) - 1):
            for h in heads:
                p[h] = jnp.dot(p[h], p[h], preferred_element_type=F32)
            for h in heads:
                x[h] = x[h] + jnp.dot(p[h], x[h], preferred_element_type=F32)
        for h in heads:
            cols = slice(h * d, (h + 1) * d)
            u_ref[0, rows, cols] = x[h][:, :d]
            w_ref[0, rows, cols] = _mxu(x[h][:, d:])
        return carry

    lax.fori_loop(0, q_ref.shape[1] // c, chunk, 0)


def _dn_delta(qkv, z, gb, norm_w, layer):
    b, t, w = z.shape
    rs = DN_ROWS
    cps = rs // DN_CHUNK
    aw = DN_HEADS * DN_CHUNK
    spec = pl.BlockSpec((1, rs, w), lambda bi, i: (bi, i, 0))
    full = jax.ShapeDtypeStruct((b, t, w), F32)
    mxu_only = jax.ShapeDtypeStruct((b, t, w), BF16)
    u, wy, qd, kd, attn, egl = pl.pallas_call(
        _dn_prep_body,
        grid=(b, t // rs),
        in_specs=[pl.BlockSpec((1, rs, w), lambda bi, i: (bi, i, 0)),
                  pl.BlockSpec((1, rs, w), lambda bi, i: (bi, i, 1)),
                  pl.BlockSpec((1, rs, w), lambda bi, i: (bi, i, 2)),
                  pl.BlockSpec((1, rs, LANES), lambda bi, i: (bi, i, 0))],
        out_specs=[spec, spec, spec, spec,
                   pl.BlockSpec((1, rs, aw), lambda bi, i: (bi, i, 0)),
                   pl.BlockSpec((1, cps * DN_HEADS, DN_HEAD_DIM), lambda bi, i: (bi, i, 0))],
        out_shape=[full, mxu_only, mxu_only, mxu_only,
                   jax.ShapeDtypeStruct((b, t, aw), BF16),
                   jax.ShapeDtypeStruct((b, t // DN_CHUNK * DN_HEADS, DN_HEAD_DIM), F32)],
        compiler_params=_cparams("parallel", "parallel"),
        name="dn_prep",
    )(qkv, qkv, qkv, gb)
    bspec = pl.BlockSpec((b, rs, w), lambda i: (0, i, 0))
    return pl.pallas_call(
        _delta_body,
        grid=(t // rs,),
        in_specs=[bspec, bspec, bspec, bspec,
                  pl.BlockSpec((b, rs, aw), lambda i: (0, i, 0)),
                  pl.BlockSpec((b, cps * DN_HEADS, DN_HEAD_DIM), lambda i: (0, i, 0)),
                  bspec,
                  pl.BlockSpec((None, 1, DN_HEAD_DIM), lambda i: (layer, 0, 0))],
        out_specs=bspec,
        out_shape=mxu_only,
        scratch_shapes=[pltpu.VMEM((b * DN_HEADS, DN_HEAD_DIM, DN_HEAD_DIM), F32)],
        compiler_params=_cparams("arbitrary"),
        name="dn_scan",
    )(u, wy, qd, kd, attn, egl, z, norm_w)


def _proj_ln_body(a_ref, w_ref, h_ref, g_ref, b_ref, o_ref):
    y = jnp.dot(_mxu(a_ref[...]), _mxu(w_ref[...]), preferred_element_type=F32)
    o_ref[...] = _layer_norm(ALPHA * h_ref[...] + y, g_ref[...], b_ref[...])


def _proj_res_ln(a, w, h, g, b, w_layer, ln_layer, name):
    m, k = a.shape
    n = h.shape[1]
    row = pl.BlockSpec((None, 1, n), lambda i: (ln_layer, 0, 0))
    return pl.pallas_call(
        _proj_ln_body,
        grid=(m // LN_TM,),
        in_specs=[pl.BlockSpec((LN_TM, k), lambda i: (i, 0)),
                  pl.BlockSpec((None, k, n), lambda i: (w_layer, 0, 0)),
                  pl.BlockSpec((LN_TM, n), lambda i: (i, 0)), row, row],
        out_specs=pl.BlockSpec((LN_TM, n), lambda i: (i, 0)),
        out_shape=jax.ShapeDtypeStruct((m, n), F32),
        compiler_params=_cparams("parallel"),
        name=name,
    )(a, w, h, g, b)


def _mm_nt_body(w_ref, x_ref, o_ref):
    o_ref[...] = _nt(_mxu(w_ref[...]), _mxu(x_ref[...])).astype(o_ref.dtype)


def _matmul_nt(w_t, x, name, out_dtype=F32):
    n, k = w_t.shape
    m = x.shape[0]
    return pl.pallas_call(
        _mm_nt_body,
        grid=(n // MM_TN, m // MM_TM),
        in_specs=[pl.BlockSpec((MM_TN, k), lambda j, i: (j, 0)),
                  pl.BlockSpec((MM_TM, k), lambda j, i: (i, 0))],
        out_specs=pl.BlockSpec((MM_TN, MM_TM), lambda j, i: (j, i)),
        out_shape=jax.ShapeDtypeStruct((n, m), out_dtype),
        compiler_params=_cparams("parallel", "parallel"),
        name=name,
    )(w_t, x)


def _attn_body(q_ref, k_ref, vt_ref, lam_ref, sw_ref, o_ref, m_ref, l_ref, acc_ref, sta_ref, stb_ref, *, lam_init):
    i = pl.program_id(2)
    t = ATT_TK
    dv = DA_V_DIM
    heads = range(ATT_HEADS)
    maps = [(hh, s) for hh in heads for s in range(2)]
    q_maps = []
    for hh in heads:
        q = q_ref[0, :, hh * dv:(hh + 1) * dv].astype(F32) * (DA_HEAD_DIM ** -0.5 * math.log2(math.e))
        lane = lax.broadcasted_iota(I32, q.shape, 1)
        q_maps += [_mxu(jnp.where(lane < DA_HEAD_DIM, q, 0.0)), _mxu(jnp.where(lane >= DA_HEAD_DIM, q, 0.0))]
    m_ref[...] = jnp.full_like(m_ref, -jnp.inf)
    l_ref[...] = jnp.zeros_like(l_ref)
    acc_ref[...] = jnp.zeros_like(acc_ref)

    st_bufs = (sta_ref, stb_ref)

    def scores(j, buf, q0=0):
        rows = pl.ds(pl.multiple_of(j * t, t), t)
        for hh, s in maps:
            k = k_ref[0, rows, hh * dv:(hh + 1) * dv]
            st_bufs[buf][2 * hh + s, :, q0:] = _nt(k, q_maps[2 * hh + s][q0:])

    def accumulate(j, buf, q0=0, diagonal=False):
        rows = pl.ds(pl.multiple_of(j * t, t), t)
        ones = jnp.ones((BF16_SUBLANES, t), BF16)
        vt = [jnp.concatenate([vt_ref[hh * dv:(hh + 1) * dv, rows], ones], axis=0) for hh in heads]
        idx = [2 * hh + s for hh, s in maps]
        st = [st_bufs[buf][x, :, q0:] for x in idx]
        if diagonal:
            ri = lax.broadcasted_iota(I32, st[0].shape, 0)
            ci = lax.broadcasted_iota(I32, st[0].shape, 1)
            st = [jnp.where(ri <= ci, x, -jnp.inf) for x in st]
        m_old = [m_ref[x, :, q0:] for x in idx]
        m_new = [jnp.maximum(mo, jnp.max(x, axis=0, keepdims=True)) for mo, x in zip(m_old, st)]
        p = [_mxu(jnp.exp2(x - mn)) for x, mn in zip(st, m_new)]
        pv = [jnp.dot(vt[x // 2], px, preferred_element_type=F32) for x, px in zip(idx, p)]
        for n, x in enumerate(idx):
            corr = jnp.exp2(m_old[n] - m_new[n])
            l_ref[x, :, q0:] = corr * l_ref[x, :, q0:] + pv[n][dv:dv + 1]
            acc_ref[x, :, q0:] = corr * acc_ref[x, :, q0:] + pv[n][:dv]
            m_ref[x, :, q0:] = m_new[n]

    scores(0, 0)

    def two_tiles(u, carry):
        j = 2 * u
        scores(j + 1, 1)
        accumulate(j, 0)
        scores(j + 2, 0)
        accumulate(j + 1, 1)
        return carry

    lax.fori_loop(0, i, two_tiles, 0)
    scores(2 * i + 1, 1, q0=t)
    accumulate(2 * i, 0, diagonal=True)
    accumulate(2 * i + 1, 1, q0=t, diagonal=True)

    lp = lam_ref[...]
    lam = (jnp.exp(jnp.sum(lp[0:1] * lp[1:2], axis=-1, keepdims=True))
           - jnp.exp(jnp.sum(lp[2:3] * lp[3:4], axis=-1, keepdims=True)) + lam_init)
    for hh in heads:
        ot = acc_ref[2 * hh] / l_ref[2 * hh] - lam * (acc_ref[2 * hh + 1] / l_ref[2 * hh + 1])
        ms = jnp.mean(ot * ot, axis=0, keepdims=True)
        ot = ot * lax.rsqrt(ms + RMS_EPS) * sw_ref[...] * (1.0 - lam_init)
        o_ref[0, :, hh * dv:(hh + 1) * dv] = ot.T.astype(o_ref.dtype)


def _diff_attention(qp, kp, vt, lam_p, subln_w, layer, lam_init):
    b, t, w = qp.shape
    tq, tk = 2 * ATT_TK, ATT_TK
    hw = ATT_HEADS * DA_V_DIM
    return pl.pallas_call(
        functools.partial(_attn_body, lam_init=lam_init),
        grid=(b, DA_HEADS // ATT_HEADS, t // tq),
        in_specs=[pl.BlockSpec((1, tq, hw), lambda bi, h, i: (bi, i, h)),
                  pl.BlockSpec((1, t, hw), lambda bi, h, i: (bi, 0, h)),
                  pl.BlockSpec((hw, t), lambda bi, h, i: (h, bi)),
                  pl.BlockSpec((None, 4, DA_HEAD_DIM), lambda bi, h, i: (layer, 0, 0)),
                  pl.BlockSpec((None, DA_V_DIM, 1), lambda bi, h, i: (layer, 0, 0))],
        out_specs=pl.BlockSpec((1, tq, hw), lambda bi, h, i: (bi, i, h)),
        out_shape=jax.ShapeDtypeStruct((b, t, w), BF16),
        scratch_shapes=[pltpu.VMEM((2 * ATT_HEADS, 1, tq), F32), pltpu.VMEM((2 * ATT_HEADS, 1, tq), F32),
                        pltpu.VMEM((2 * ATT_HEADS, DA_V_DIM, tq), F32),
                        pltpu.VMEM((2 * ATT_HEADS, tk, tq), F32), pltpu.VMEM((2 * ATT_HEADS, tk, tq), F32)],
        compiler_params=_cparams("parallel", "parallel", "parallel"),
        name="diff_attn",
    )(qp, kp, vt, lam_p, subln_w)


def _router_body(h_ref, w_ref, info_ref, gate_ref, cnt_ref, carry_ref):
    tm = h_ref.shape[0]

    @pl.when(pl.program_id(0) == 0)
    def _():
        carry_ref[...] = jnp.zeros_like(carry_ref)

    h = h_ref[...]
    w = w_ref[...]
    h_hi = h.astype(jnp.bfloat16)
    h_lo = (h - h_hi.astype(F32)).astype(jnp.bfloat16)
    w_hi = w.astype(jnp.bfloat16)
    w_lo = (w - w_hi.astype(F32)).astype(jnp.bfloat16)
    logits = (jnp.dot(h_hi, w_hi, preferred_element_type=F32) + jnp.dot(h_lo, w_hi, preferred_element_type=F32)
              + jnp.dot(h_hi, w_lo, preferred_element_type=F32))
    lane = lax.broadcasted_iota(I32, logits.shape, 1)
    big = jnp.int32(LANES)

    def first_max(vals):
        mx = jnp.max(vals, axis=-1, keepdims=True)
        idx = jnp.min(jnp.where(vals == mx, lane, big), axis=-1, keepdims=True)
        return mx, idx

    gl = jnp.where(lane < N_GROUPS, logits, -jnp.inf)
    gmax, gidx = first_max(gl)
    p_group = 1.0 / jnp.sum(jnp.exp(gl - gmax), axis=-1, keepdims=True)
    lo = N_GROUPS + gidx * EXPERTS_PER_GROUP
    el = jnp.where((lane >= lo) & (lane < lo + EXPERTS_PER_GROUP), logits, -jnp.inf)
    m1, i1 = first_max(el)
    m2, i2 = first_max(jnp.where(lane == i1, -jnp.inf, el))
    e2 = jnp.exp(m2 - m1)
    gate1 = p_group / (1.0 + e2)
    gate2 = p_group * e2 / (1.0 + e2)

    oh = ((lane == i1) | (lane == i2)).astype(F32)
    ri = lax.broadcasted_iota(I32, (tm, tm), 0)
    ci = lax.broadcasted_iota(I32, (tm, tm), 1)
    before = jnp.dot((ci < ri).astype(F32), oh, preferred_element_type=F32) + carry_ref[0:1, :]
    rank1 = jnp.sum(jnp.where(lane == i1, before, 0.0), axis=-1, keepdims=True).astype(I32)
    rank2 = jnp.sum(jnp.where(lane == i2, before, 0.0), axis=-1, keepdims=True).astype(I32)
    total = carry_ref[0:1, :] + jnp.sum(oh, axis=0, keepdims=True)
    carry_ref[0:1, :] = total

    info = jnp.where(lane == 0, i1 - N_GROUPS,
                     jnp.where(lane == 1, i2 - N_GROUPS,
                               jnp.where(lane == 2, rank1, jnp.where(lane == 3, rank2, 0))))
    info_ref[...] = info.T[:SUBLANES]
    gate_ref[...] = jnp.where(lane == 0, gate1, jnp.where(lane == 1, gate2, 0.0))
    cnt_ref[...] = jnp.broadcast_to(total, cnt_ref.shape).astype(I32)


def _moe_router(h, w_route):
    m, k = h.shape
    tm = ROUTE_TM
    return pl.pallas_call(
        _router_body,
        grid=(m // tm,),
        in_specs=[pl.BlockSpec((tm, k), lambda i: (i, 0)),
                  pl.BlockSpec((k, LANES), lambda i: (0, 0))],
        out_specs=[pl.BlockSpec((SUBLANES, tm), lambda i: (0, i)),
                   pl.BlockSpec((tm, LANES), lambda i: (i, 0)),
                   pl.BlockSpec((SUBLANES, LANES), lambda i: (0, 0))],
        out_shape=[jax.ShapeDtypeStruct((SUBLANES, m), I32),
                   jax.ShapeDtypeStruct((m, LANES), F32),
                   jax.ShapeDtypeStruct((SUBLANES, LANES), I32)],
        scratch_shapes=[pltpu.VMEM((SUBLANES, LANES), F32)],
        compiler_params=_cparams("arbitrary"),
        name="moe_router",
    )(h, w_route)


def _pos_body(start_ref, info_ref, pos_ref):
    info = info_ref[...]
    base = jnp.zeros_like(info)
    for e in range(N_EXPERTS):
        base = jnp.where(info == e, start_ref[e], base)
    pos_ref[...] = base + pltpu.roll(info, SUBLANES - TOP_K, axis=0)


def _moe_positions(info, pad_start):
    return pl.pallas_call(
        _pos_body,
        grid_spec=pltpu.PrefetchScalarGridSpec(
            num_scalar_prefetch=1, grid=(1,),
            in_specs=[pl.BlockSpec(info.shape, lambda i, *_: (0, 0))],
            out_specs=pl.BlockSpec(info.shape, lambda i, *_: (0, 0))),
        out_shape=jax.ShapeDtypeStruct(info.shape, I32),
        compiler_params=_cparams("arbitrary"),
        name="moe_positions",
    )(pad_start, info)


ZERO_ROWS = MOE_TB // 2


def _dispatch_body(pos0_ref, pos1_ref, zlo_ref, zhi_ref, x_ref, xs_hbm, zero_ref, sem, zsem):
    i = pl.program_id(0)
    tm = x_ref.shape[0]

    def zero_pieces(act):
        def per_expert(e, carry):
            lo = zlo_ref[e]
            hi = zhi_ref[e]
            n = hi - lo
            size = ZERO_ROWS
            end = hi
            while size >= SUBLANES:
                take = (n & size) != 0

                @pl.when(take)
                def _(end=end, size=size):
                    off = pl.multiple_of(end - size, SUBLANES)
                    act(pltpu.make_async_copy(zero_ref.at[pl.ds(0, size), :], xs_hbm.at[pl.ds(off, size), :], zsem))

                end = end - jnp.where(take, size, 0)
                size //= 2
            for r in range(SUBLANES - 1):
                @pl.when(r < (n & (SUBLANES - 1)))
                def _(r=r):
                    act(pltpu.make_async_copy(zero_ref.at[pl.ds(0, 1), :], xs_hbm.at[pl.ds(lo + r, 1), :], zsem))
            return carry

        lax.fori_loop(0, N_EXPERTS, per_expert, 0)

        def trailing(p, carry):
            off = pl.multiple_of(p * ZERO_ROWS, ZERO_ROWS)
            act(pltpu.make_async_copy(zero_ref, xs_hbm.at[pl.ds(off, ZERO_ROWS), :], zsem))
            return carry

        lax.fori_loop(zlo_ref[N_EXPERTS] // ZERO_ROWS, xs_hbm.shape[0] // ZERO_ROWS, trailing, 0)

    @pl.when(i == 0)
    def _():
        zero_ref[...] = jnp.zeros_like(zero_ref)
        zero_pieces(lambda cp: cp.start())

    def issue(r, carry):
        t = i * tm + r
        src = x_ref.at[pl.ds(r, 1), :]
        pltpu.make_async_copy(src, xs_hbm.at[pl.ds(pos0_ref[t], 1), :], sem).start()
        pltpu.make_async_copy(src, xs_hbm.at[pl.ds(pos1_ref[t], 1), :], sem).start()
        return carry

    lax.fori_loop(0, tm, issue, 0, unroll=8)
    for _ in range(TOP_K):
        pltpu.make_async_copy(x_ref, xs_hbm.at[pl.ds(0, tm), :], sem).wait()

    @pl.when(i == 0)
    def _():
        zero_pieces(lambda cp: cp.wait())


def _moe_dispatch(x, pos0, pos1, zlo, zhi, n_rows):
    m, d = x.shape
    tm = DISP_TM
    return pl.pallas_call(
        _dispatch_body,
        grid_spec=pltpu.PrefetchScalarGridSpec(
            num_scalar_prefetch=4, grid=(m // tm,),
            in_specs=[pl.BlockSpec((tm, d), lambda i, *_: (i, 0))],
            out_specs=pl.BlockSpec(memory_space=pl.ANY),
            scratch_shapes=[pltpu.VMEM((ZERO_ROWS, d), F32), pltpu.SemaphoreType.DMA(()),
                            pltpu.SemaphoreType.DMA(())]),
        out_shape=jax.ShapeDtypeStruct((n_rows, d), F32),
        compiler_params=_cparams("arbitrary"),
        name="moe_dispatch",
    )(pos0, pos1, zlo, zhi, x)


def _expert_body(eid_ref, nused_ref, x_ref, w13_ref, w2_ref, o_ref):
    ff = w2_ref.shape[0]

    used = pl.program_id(0) < nused_ref[0]

    @pl.when(used)
    def _():
        hcat = jnp.dot(x_ref[...], w13_ref[...], preferred_element_type=F32)
        hid = _silu(hcat[:, :ff]) * hcat[:, ff:]
        o_ref[...] = jnp.dot(hid, w2_ref[...], preferred_element_type=F32)

    @pl.when(jnp.logical_not(used))
    def _():
        o_ref[...] = jnp.zeros_like(o_ref)


def _moe_experts(xs, block_eid, n_used, w13, w2, layer):
    n_rows, d = xs.shape
    tb = MOE_TB
    ff = w2.shape[2]

    def row_block(b, eid, nused):
        return (jnp.minimum(b, nused[0] - 1), 0)

    return pl.pallas_call(
        _expert_body,
        grid_spec=pltpu.PrefetchScalarGridSpec(
            num_scalar_prefetch=2, grid=(n_rows // tb,),
            in_specs=[pl.BlockSpec((tb, d), row_block),
                      pl.BlockSpec((None, None, d, 2 * ff), lambda b, eid, nused: (layer, eid[b], 0, 0)),
                      pl.BlockSpec((None, None, ff, d), lambda b, eid, nused: (layer, eid[b], 0, 0))],
            out_specs=pl.BlockSpec((tb, d), lambda b, eid, nused: (b, 0))),
        out_shape=jax.ShapeDtypeStruct((n_rows, d), F32),
        compiler_params=_cparams("arbitrary"),
        name="moe_experts",
    )(block_eid, n_used, xs, w13, w2)


def _combine_body(pos0_ref, pos1_ref, ys_hbm, h_ref, gate_ref, g_ref, b_ref, o_ref, y0_ref, y1_ref, sem0, sem1):
    i = pl.program_id(0)
    tm = h_ref.shape[0]
    slot = i % 2

    def gather(tile, dst):
        def issue(r, carry):
            t = tile * tm + r
            pltpu.make_async_copy(ys_hbm.at[pl.ds(pos0_ref[t], 1), :], y0_ref.at[dst, pl.ds(r, 1), :],
                                  sem0.at[dst]).start()
            pltpu.make_async_copy(ys_hbm.at[pl.ds(pos1_ref[t], 1), :], y1_ref.at[dst, pl.ds(r, 1), :],
                                  sem1.at[dst]).start()
            return carry

        lax.fori_loop(0, tm, issue, 0, unroll=8)

    @pl.when(i == 0)
    def _():
        gather(0, 0)

    @pl.when(i + 1 < pl.num_programs(0))
    def _():
        gather(i + 1, 1 - slot)

    pltpu.make_async_copy(ys_hbm.at[pl.ds(0, tm), :], y0_ref.at[slot], sem0.at[slot]).wait()
    pltpu.make_async_copy(ys_hbm.at[pl.ds(0, tm), :], y1_ref.at[slot], sem1.at[slot]).wait()
    gates = gate_ref[...]
    ffn = gates[:, 0:1] * y0_ref[slot] + gates[:, 1:2] * y1_ref[slot]
    o_ref[...] = _layer_norm(ALPHA * h_ref[...] + ffn, g_ref[...], b_ref[...])


def _moe_combine(ys, pos0, pos1, h, gates, g, b, layer):
    m, d = h.shape
    tm = COMB_TM
    row = pl.BlockSpec((None, 1, d), lambda i, *_: (layer, 0, 0))
    return pl.pallas_call(
        _combine_body,
        grid_spec=pltpu.PrefetchScalarGridSpec(
            num_scalar_prefetch=2, grid=(m // tm,),
            in_specs=[pl.BlockSpec(memory_space=pl.ANY),
                      pl.BlockSpec((tm, d), lambda i, *_: (i, 0)),
                      pl.BlockSpec((tm, LANES), lambda i, *_: (i, 0)), row, row],
            out_specs=pl.BlockSpec((tm, d), lambda i, *_: (i, 0)),
            scratch_shapes=[pltpu.VMEM((2, tm, d), F32), pltpu.VMEM((2, tm, d), F32),
                            pltpu.SemaphoreType.DMA((2,)), pltpu.SemaphoreType.DMA((2,))]),
        out_shape=jax.ShapeDtypeStruct((m, d), F32),
        compiler_params=_cparams("arbitrary"),
        name="moe_combine",
    )(pos0, pos1, ys, h, gates, g, b)


def _moe_layer(h, w_group, w_expert, w13, w2, ln_g, ln_b, layer):
    n, d = h.shape
    tb = MOE_TB
    w_route = jnp.pad(jnp.concatenate([w_group, w_expert], axis=1), ((0, 0), (0, LANES - N_GROUPS - N_EXPERTS)))
    info, gates, cnt = _moe_router(h, w_route)
    counts = cnt[0, N_GROUPS:N_GROUPS + N_EXPERTS]
    padded = (counts + tb - 1) // tb * tb
    pad_end = jnp.cumsum(padded)
    pad_start = pad_end - padded
    pos = _moe_positions(info, pad_start)
    pos0, pos1 = pos[0], pos[1]
    n_blocks = (n * TOP_K + N_EXPERTS * (tb - 1) + tb - 1) // tb
    n_used = (pad_end[-1] // tb).astype(I32)
    blk = jnp.minimum(jnp.arange(n_blocks, dtype=I32), n_used - 1)
    block_eid = jnp.minimum(jnp.sum(pad_end[None, :] <= (blk * tb)[:, None], axis=1), N_EXPERTS - 1).astype(I32)
    zlo = jnp.concatenate([pad_start + counts, pad_end[-1:]])
    xs = _moe_dispatch(h, pos0, pos1, zlo, pad_end, n_blocks * tb)
    ys = _moe_experts(xs, block_eid, n_used.reshape(1), w13, w2, layer)
    return _moe_combine(ys, pos0, pos1, h, gates, ln_g, ln_b, layer)


def _lambda_init(layer_idx):
    return 0.8 - 0.6 * math.exp(-0.3 * layer_idx)


def kernel(x, a_w_in, a_conv_w, a_a_log, a_dt_bias, a_norm_w, a_w_out, kv_w, b_w_q, b_lambda, b_subln_w, b_w_out,
           ln_mix_g, ln_mix_b, ln_ffn_g, ln_ffn_b, moe_w_group, moe_w_expert, moe_w13, moe_w2):
    b, t, d = x.shape
    n = b * t
    dn_w = DN_HEADS * DN_HEAD_DIM
    h = x.reshape(n, d)

    def per_layer_rows(p):
        return p.reshape(p.shape[0], 1, p.shape[1])

    a_norm_w = per_layer_rows(a_norm_w)
    b_subln_w = b_subln_w.reshape(b_subln_w.shape + (1,))
    ln_mix_g, ln_mix_b = per_layer_rows(ln_mix_g), per_layer_rows(ln_mix_b)
    ln_ffn_g, ln_ffn_b = per_layer_rows(ln_ffn_g), per_layer_rows(ln_ffn_b)
    da_w = DA_HEADS * DA_V_DIM
    w_in, w_out_a = _mxu(a_w_in), _mxu(a_w_out)
    w_k, w_vt = _mxu(kv_w[:, :da_w]).reshape(1, d, da_w), _mxu(kv_w[:, da_w:].T)
    w_q, w_out_b = _mxu(b_w_q), _mxu(b_w_out)
    kp = vt = None
    for layer in range(DEPTH):
        if layer < N_A_LAYERS:
            qkv = _dn_qkv_proj(h, w_in, a_conv_w, layer, t)
            z = _matmul(h, w_in, layer, dn_w, "dn_z_proj", BF16, col0=3 * dn_w)
            gb = _dn_gates(h, a_w_in[layer, :, 4 * dn_w:], a_a_log[layer], a_dt_bias[layer]).reshape(b, t, LANES)
            mix_in = _dn_delta(qkv.reshape(b, t, 3 * dn_w), z.reshape(b, t, dn_w), gb, a_norm_w, layer).reshape(n, dn_w)
            h = _proj_res_ln(mix_in, w_out_a, h, ln_mix_g, ln_mix_b, layer, layer, "dn_out_ln")
        else:
            j = layer - N_A_LAYERS
            if j == 0:
                kp = _matmul(h, w_k, 0, da_w, "k_proj", BF16).reshape(b, t, da_w)
                vt = _matmul_nt(w_vt, h, "v_proj", BF16)
            qp = _matmul(h, w_q, j, da_w, "q_proj", BF16).reshape(b, t, da_w)
            mix_in = _diff_attention(qp, kp, vt, b_lambda, b_subln_w, j, _lambda_init(layer)).reshape(n, da_w)
            h = _proj_res_ln(mix_in, w_out_b, h, ln_mix_g, ln_mix_b, j, layer, "da_out_ln")
        h = _moe_layer(h, moe_w_group[layer], moe_w_expert[layer], moe_w13, moe_w2, ln_ffn_g, ln_ffn_b, layer)
    return h.reshape(b, t, d)
```

```python
import functools
import math

import jax
import jax.numpy as jnp
from jax import lax
from jax.experimental import pallas as pl
from jax.experimental.pallas import tpu as pltpu

F32 = jnp.float32
BF16 = jnp.bfloat16
I32 = jnp.int32

DEPTH = 4
N_A_LAYERS = DEPTH // 2
DN_HEADS = 8
DN_HEAD_DIM = 128
CONV_K = 4
DN_CHUNK = 64
DA_HEADS = 8
DA_HEAD_DIM = 64
DA_V_DIM = 2 * DA_HEAD_DIM
N_GROUPS = 4
EXPERTS_PER_GROUP = 8
N_EXPERTS = N_GROUPS * EXPERTS_PER_GROUP
TOP_K = 2
ALPHA = (2 * DEPTH) ** 0.25
LN_EPS = 1e-5
RMS_EPS = 1e-6

LANES = 128
SUBLANES = 8
BF16_SUBLANES = 16
VMEM_LIMIT_BYTES = 56 * 1024 * 1024

MM_TM = 2048
MM_TN = 512
LN_TM = 512
DN_ROWS = 256
ATT_TK = 512
ATT_HEADS = 2
ROUTE_TM = 512
MOE_TB = 256
DISP_TM = 512
COMB_TM = 256


def _cparams(*sem):
    return pltpu.CompilerParams(dimension_semantics=sem, vmem_limit_bytes=VMEM_LIMIT_BYTES)


def _silu(x):
    return x * (1.0 / (1.0 + jnp.exp(-x)))


def _layer_norm(x, g, b):
    mu = jnp.mean(x, axis=-1, keepdims=True)
    xc = x - mu
    var = jnp.mean(xc * xc, axis=-1, keepdims=True)
    return xc * lax.rsqrt(var + LN_EPS) * g + b


def _mxu(x):
    return x.astype(BF16)


def _mm_body(x_ref, w_ref, o_ref):
    o_ref[...] = jnp.dot(_mxu(x_ref[...]), _mxu(w_ref[...]), preferred_element_type=F32).astype(o_ref.dtype)


def _matmul(x, w, layer, n_out, name, out_dtype=F32, col0=0):
    m, k = x.shape
    j0 = col0 // MM_TN
    return pl.pallas_call(
        _mm_body,
        grid=(m // MM_TM, n_out // MM_TN),
        in_specs=[pl.BlockSpec((MM_TM, k), lambda i, j: (i, 0)),
                  pl.BlockSpec((None, k, MM_TN), lambda i, j: (layer, 0, j0 + j))],
        out_specs=pl.BlockSpec((MM_TM, MM_TN), lambda i, j: (i, j)),
        out_shape=jax.ShapeDtypeStruct((m, n_out), out_dtype),
        compiler_params=_cparams("parallel", "parallel"),
        name=name,
    )(x, w)


def _gates_body(x_ref, w_ref, alog_ref, dtb_ref, o_ref):
    logit = jnp.dot(_mxu(x_ref[...]), _mxu(w_ref[...]), preferred_element_type=F32)
    lane = lax.broadcasted_iota(I32, logit.shape, 1)
    beta = 1.0 / (1.0 + jnp.exp(-logit))
    sp_in = logit + dtb_ref[...]
    softplus = jnp.maximum(sp_in, 0.0) + jnp.log1p(jnp.exp(-jnp.abs(sp_in)))
    g = -jnp.exp(alog_ref[...]) * softplus
    o_ref[...] = jnp.where(lane < DN_HEADS, beta, g)


def _dn_gates(x, w_small, a_log, dt_bias):
    m, k = x.shape
    pad = LANES - 2 * DN_HEADS
    w = jnp.pad(w_small, ((0, 0), (0, pad)))
    alog = jnp.pad(a_log, (DN_HEADS, pad)).reshape(1, LANES)
    dtb = jnp.pad(dt_bias, (DN_HEADS, pad)).reshape(1, LANES)
    return pl.pallas_call(
        _gates_body,
        grid=(m // MM_TM,),
        in_specs=[pl.BlockSpec((MM_TM, k), lambda i: (i, 0)),
                  pl.BlockSpec((k, LANES), lambda i: (0, 0)),
                  pl.BlockSpec((1, LANES), lambda i: (0, 0)),
                  pl.BlockSpec((1, LANES), lambda i: (0, 0))],
        out_specs=pl.BlockSpec((MM_TM, LANES), lambda i: (i, 0)),
        out_shape=jax.ShapeDtypeStruct((m, LANES), F32),
        compiler_params=_cparams("parallel"),
        name="dn_gates",
    )(x, w, alog, dtb)


def _qkv_proj_body(x_ref, w_ref, cw_ref, qkv_ref, halo_ref, *, tiles_per_seq, tiles_per_part):
    i, j = pl.program_id(0), pl.program_id(1)
    proj = jnp.dot(_mxu(x_ref[...]), _mxu(w_ref[...]), preferred_element_type=F32)
    tm, tn = proj.shape
    prev = jnp.where(i % tiles_per_seq == 0, 0.0, halo_ref[j])
    halo_ref[j] = proj[tm - SUBLANES:]
    xe = jnp.concatenate([prev, proj], axis=0)
    acc = proj * cw_ref[CONV_K - 1:CONV_K, :]
    for tap in range(CONV_K - 1):
        shifted = pltpu.roll(xe, CONV_K - 1 - tap, axis=0)[SUBLANES:]
        acc = acc + shifted * cw_ref[tap:tap + 1, :]
    y = _silu(acc)
    normed = []
    for h in range(tn // DN_HEAD_DIM):
        yh = y[:, h * DN_HEAD_DIM:(h + 1) * DN_HEAD_DIM]
        ss = jnp.sum(yh * yh, axis=-1, keepdims=True)
        normed.append(yh * lax.rsqrt(ss + RMS_EPS))
    normed = jnp.concatenate(normed, axis=-1)
    q_scale = jnp.where(j < tiles_per_part, DN_HEAD_DIM ** -0.5, 1.0)
    qkv_ref[...] = jnp.where(j < 2 * tiles_per_part, normed * q_scale, y).astype(qkv_ref.dtype)


def _dn_qkv_proj(x, w_in, conv_w, layer, seq_len):
    m, k = x.shape
    width = DN_HEADS * DN_HEAD_DIM
    tm, tn = MM_TM, MM_TN
    conv_tiles = 3 * width // tn
    body = functools.partial(_qkv_proj_body, tiles_per_seq=seq_len // tm, tiles_per_part=width // tn)
    return pl.pallas_call(
        body,
        grid=(m // tm, conv_tiles),
        in_specs=[pl.BlockSpec((tm, k), lambda i, j: (i, 0)),
                  pl.BlockSpec((None, k, tn), lambda i, j: (layer, 0, j)),
                  pl.BlockSpec((None, CONV_K, tn), lambda i, j: (layer, 0, j))],
        out_specs=pl.BlockSpec((tm, tn), lambda i, j: (i, j)),
        out_shape=jax.ShapeDtypeStruct((m, 3 * width), BF16),
        scratch_shapes=[pltpu.VMEM((conv_tiles, SUBLANES, tn), F32)],
        compiler_params=_cparams("arbitrary", "arbitrary"),
        name="dn_qkv_proj",
    )(x, w_in, conv_w)


def _split3(x):
    hi = x.astype(jnp.bfloat16).astype(F32)
    r = x - hi
    mid = r.astype(jnp.bfloat16).astype(F32)
    return hi, mid, r - mid


def _nt(a, b):
    return lax.dot_general(a, b, (((1,), (1,)), ((), ())), preferred_element_type=F32)


def _tn(a, b):
    return lax.dot_general(a, b, (((0,), (0,)), ((), ())), preferred_element_type=F32)


def _delta_body(u_ref, w_ref, qd_ref, kd_ref, attn_ref, egl_ref, z_ref, nw_ref, o_ref, state_ref):
    c = DN_CHUNK
    d = DN_HEAD_DIM

    @pl.when(pl.program_id(0) == 0)
    def _():
        state_ref[...] = jnp.zeros_like(state_ref)

    nw = nw_ref[...]
    nb = u_ref.shape[0]
    chains = [(bi, h) for bi in range(nb) for h in range(DN_HEADS)]

    def chunk(ic, carry):
        rows = pl.ds(pl.multiple_of(ic * c, c), c)
        s_old, ws_qs = {}, {}
        for bi, h in chains:
            cols = slice(h * d, (h + 1) * d)
            s_old[bi, h] = state_ref[bi * DN_HEADS + h]
            lhs = jnp.concatenate([w_ref[bi, rows, cols], qd_ref[bi, rows, cols]], axis=0)
            ws_qs[bi, h] = jnp.dot(lhs, _mxu(s_old[bi, h]), preferred_element_type=F32)
        v_new = {}
        for bi, h in chains:
            cols = slice(h * d, (h + 1) * d)
            v_new[bi, h] = u_ref[bi, rows, cols] - ws_qs[bi, h][:c]
        for bi, h in chains:
            cols = slice(h * d, (h + 1) * d)
            acols = slice(h * c, (h + 1) * c)
            vn = _mxu(v_new[bi, h])
            o = ws_qs[bi, h][c:] + jnp.dot(attn_ref[bi, rows, acols], vn, preferred_element_type=F32)
            gl = egl_ref[bi, pl.ds(ic * DN_HEADS + h, 1), :]
            state_ref[bi * DN_HEADS + h] = s_old[bi, h] * gl + _tn(kd_ref[bi, rows, cols], vn)
            ms = jnp.mean(o * o, axis=-1, keepdims=True)
            z = z_ref[bi, rows, cols].astype(F32)
            o_ref[bi, rows, cols] = (o * lax.rsqrt(ms + RMS_EPS) * nw * _silu(z)).astype(o_ref.dtype)
        return carry

    lax.fori_loop(0, u_ref.shape[1] // c, chunk, 0)


def _dn_prep_body(q_ref, k_ref, v_ref, gb_ref, u_ref, w_ref, qd_ref, kd_ref, attn_ref, egl_ref):
    c = DN_CHUNK
    d = DN_HEAD_DIM
    ri = lax.broadcasted_iota(I32, (c, c), 0)
    ci = lax.broadcasted_iota(I32, (c, c), 1)
    causal = ri >= ci
    strict = ri > ci
    tri = causal.astype(F32)
    heads = range(DN_HEADS)

    def chunk(ic, carry):
        rows = pl.ds(pl.multiple_of(ic * c, c), c)
        gbt = gb_ref[0, rows, :]
        g_hi, g_mid, g_lo = _split3(gbt)
        gcol = (jnp.dot(tri, g_hi, preferred_element_type=F32)
                + jnp.dot(tri, g_mid, preferred_element_type=F32)
                + jnp.dot(tri, g_lo, preferred_element_type=F32))
        grow = gcol.T
        g_last_col = grow[DN_HEADS:2 * DN_HEADS, c - 1:c]
        egl_ref[0, pl.ds(pl.multiple_of(ic * DN_HEADS, DN_HEADS), DN_HEADS), :] = jnp.broadcast_to(
            jnp.exp(g_last_col), (DN_HEADS, d))
        a_mat, x, p = {}, {}, {}
        for h in heads:
            cols = slice(h * d, (h + 1) * d)
            qh = q_ref[0, rows, cols]
            kh = k_ref[0, rows, cols]
            qf, kf = qh.astype(F32), kh.astype(F32)
            beta_c = gbt[:, h:h + 1]
            gc_c = gcol[:, DN_HEADS + h:DN_HEADS + h + 1]
            gc_r = grow[DN_HEADS + h:DN_HEADS + h + 1, :]
            decay = jnp.exp(jnp.where(causal, gc_c - gc_r, -jnp.inf))
            eg_c = jnp.exp(gc_c)
            k_beta = kf * beta_c
            a_mat[h] = jnp.where(strict, _nt(_mxu(k_beta), kh) * decay, 0.0)
            attn_ref[0, rows, h * c:(h + 1) * c] = _mxu(jnp.where(causal, _nt(qh, kh) * decay, 0.0))
            qd_ref[0, rows, cols] = _mxu(qf * eg_c)
            kd_ref[0, rows, cols] = _mxu(kf * jnp.exp(gc_r[:, c - 1:c] - gc_c))
            x[h] = jnp.concatenate([v_ref[0, rows, cols].astype(F32) * beta_c, k_beta * eg_c], axis=-1)
        for h in heads:
            x[h] = x[h] - jnp.dot(a_mat[h], x[h], preferred_element_type=F32)
            p[h] = a_mat[h]
        for _ in range(int(math.log2(c)) - 1):
            for h in heads:
                p[h] = jnp.dot(p[h], p[h], preferred_element_type=F32)
            for h in heads:
                x[h] = x[h] + jnp.dot(p[h], x[h], preferred_element_type=F32)
        for h in heads:
            cols = slice(h * d, (h + 1) * d)
            u_ref[0, rows, cols] = x[h][:, :d]
            w_ref[0, rows, cols] = _mxu(x[h][:, d:])
        return carry

    lax.fori_loop(0, q_ref.shape[1] // c, chunk, 0)


def _dn_delta(qkv, z, gb, norm_w, layer):
    b, t, w = z.shape
    rs = DN_ROWS
    cps = rs // DN_CHUNK
    aw = DN_HEADS * DN_CHUNK
    spec = pl.BlockSpec((1, rs, w), lambda bi, i: (bi, i, 0))
    full = jax.ShapeDtypeStruct((b, t, w), F32)
    mxu_only = jax.ShapeDtypeStruct((b, t, w), BF16)
    u, wy, qd, kd, attn, egl = pl.pallas_call(
        _dn_prep_body,
        grid=(b, t // rs),
        in_specs=[pl.BlockSpec((1, rs, w), lambda bi, i: (bi, i, 0)),
                  pl.BlockSpec((1, rs, w), lambda bi, i: (bi, i, 1)),
                  pl.BlockSpec((1, rs, w), lambda bi, i: (bi, i, 2)),
                  pl.BlockSpec((1, rs, LANES), lambda bi, i: (bi, i, 0))],
        out_specs=[spec, spec, spec, spec,
                   pl.BlockSpec((1, rs, aw), lambda bi, i: (bi, i, 0)),
                   pl.BlockSpec((1, cps * DN_HEADS, DN_HEAD_DIM), lambda bi, i: (bi, i, 0))],
        out_shape=[full, mxu_only, mxu_only, mxu_only,
                   jax.ShapeDtypeStruct((b, t, aw), BF16),
                   jax.ShapeDtypeStruct((b, t // DN_CHUNK * DN_HEADS, DN_HEAD_DIM), F32)],
        compiler_params=_cparams("parallel", "parallel"),
        name="dn_prep",
    )(qkv, qkv, qkv, gb)
    bspec = pl.BlockSpec((b, rs, w), lambda i: (0, i, 0))
    return pl.pallas_call(
        _delta_body,
        grid=(t // rs,),
        in_specs=[bspec, bspec, bspec, bspec,
                  pl.BlockSpec((b, rs, aw), lambda i: (0, i, 0)),
                  pl.BlockSpec((b, cps * DN_HEADS, DN_HEAD_DIM), lambda i: (0, i, 0)),
                  bspec,
                  pl.BlockSpec((None, 1, DN_HEAD_DIM), lambda i: (layer, 0, 0))],
        out_specs=bspec,
        out_shape=mxu_only,
        scratch_shapes=[pltpu.VMEM((b * DN_HEADS, DN_HEAD_DIM, DN_HEAD_DIM), F32)],
        compiler_params=_cparams("arbitrary"),
        name="dn_scan",
    )(u, wy, qd, kd, attn, egl, z, norm_w)


def _proj_ln_body(a_ref, w_ref, h_ref, g_ref, b_ref, o_ref):
    y = jnp.dot(_mxu(a_ref[...]), _mxu(w_ref[...]), preferred_element_type=F32)
    o_ref[...] = _layer_norm(ALPHA * h_ref[...] + y, g_ref[...], b_ref[...])


def _proj_res_ln(a, w, h, g, b, w_layer, ln_layer, name):
    m, k = a.shape
    n = h.shape[1]
    row = pl.BlockSpec((None, 1, n), lambda i: (ln_layer, 0, 0))
    return pl.pallas_call(
        _proj_ln_body,
        grid=(m // LN_TM,),
        in_specs=[pl.BlockSpec((LN_TM, k), lambda i: (i, 0)),
                  pl.BlockSpec((None, k, n), lambda i: (w_layer, 0, 0)),
                  pl.BlockSpec((LN_TM, n), lambda i: (i, 0)), row, row],
        out_specs=pl.BlockSpec((LN_TM, n), lambda i: (i, 0)),
        out_shape=jax.ShapeDtypeStruct((m, n), F32),
        compiler_params=_cparams("parallel"),
        name=name,
    )(a, w, h, g, b)


def _mm_nt_body(w_ref, x_ref, o_ref):
    o_ref[...] = _nt(_mxu(w_ref[...]), _mxu(x_ref[...])).astype(o_ref.dtype)


def _matmul_nt(w_t, x, name, out_dtype=F32):
    n, k = w_t.shape
    m = x.shape[0]
    return pl.pallas_call(
        _mm_nt_body,
        grid=(n // MM_TN, m // MM_TM),
        in_specs=[pl.BlockSpec((MM_TN, k), lambda j, i: (j, 0)),
                  pl.BlockSpec((MM_TM, k), lambda j, i: (i, 0))],
        out_specs=pl.BlockSpec((MM_TN, MM_TM), lambda j, i: (j, i)),
        out_shape=jax.ShapeDtypeStruct((n, m), out_dtype),
        compiler_params=_cparams("parallel", "parallel"),
        name=name,
    )(w_t, x)


def _attn_body(q_ref, k_ref, vt_ref, lam_ref, sw_ref, o_ref, m_ref, l_ref, acc_ref, sta_ref, stb_ref, *, lam_init):
    i = pl.program_id(2)
    t = ATT_TK
    dv = DA_V_DIM
    heads = range(ATT_HEADS)
    maps = [(hh, s) for hh in heads for s in range(2)]
    q_maps = []
    for hh in heads:
        q = q_ref[0, :, hh * dv:(hh + 1) * dv].astype(F32) * (DA_HEAD_DIM ** -0.5 * math.log2(math.e))
        lane = lax.broadcasted_iota(I32, q.shape, 1)
        q_maps += [_mxu(jnp.where(lane < DA_HEAD_DIM, q, 0.0)), _mxu(jnp.where(lane >= DA_HEAD_DIM, q, 0.0))]
    m_ref[...] = jnp.full_like(m_ref, -jnp.inf)
    l_ref[...] = jnp.zeros_like(l_ref)
    acc_ref[...] = jnp.zeros_like(acc_ref)

    st_bufs = (sta_ref, stb_ref)

    def scores(j, buf, q0=0):
        rows = pl.ds(pl.multiple_of(j * t, t), t)
        for hh, s in maps:
            k = k_ref[0, rows, hh * dv:(hh + 1) * dv]
            st_bufs[buf][2 * hh + s, :, q0:] = _nt(k, q_maps[2 * hh + s][q0:])

    def accumulate(j, buf, q0=0, diagonal=False):
        rows = pl.ds(pl.multiple_of(j * t, t), t)
        ones = jnp.ones((BF16_SUBLANES, t), BF16)
        vt = [jnp.concatenate([vt_ref[hh * dv:(hh + 1) * dv, rows], ones], axis=0) for hh in heads]
        idx = [2 * hh + s for hh, s in maps]
        st = [st_bufs[buf][x, :, q0:] for x in idx]
        if diagonal:
            ri = lax.broadcasted_iota(I32, st[0].shape, 0)
            ci = lax.broadcasted_iota(I32, st[0].shape, 1)
            st = [jnp.where(ri <= ci, x, -jnp.inf) for x in st]
        m_old = [m_ref[x, :, q0:] for x in idx]
        m_new = [jnp.maximum(mo, jnp.max(x, axis=0, keepdims=True)) for mo, x in zip(m_old, st)]
        p = [_mxu(jnp.exp2(x - mn)) for x, mn in zip(st, m_new)]
        pv = [jnp.dot(vt[x // 2], px, preferred_element_type=F32) for x, px in zip(idx, p)]
        for n, x in enumerate(idx):
            corr = jnp.exp2(m_old[n] - m_new[n])
            l_ref[x, :, q0:] = corr * l_ref[x, :, q0:] + pv[n][dv:dv + 1]
            acc_ref[x, :, q0:] = corr * acc_ref[x, :, q0:] + pv[n][:dv]
            m_ref[x, :, q0:] = m_new[n]

    scores(0, 0)

    def two_tiles(u, carry):
        j = 2 * u
        scores(j + 1, 1)
        accumulate(j, 0)
        scores(j + 2, 0)
        accumulate(j + 1, 1)
        return carry

    lax.fori_loop(0, i, two_tiles, 0)
    scores(2 * i + 1, 1, q0=t)
    accumulate(2 * i, 0, diagonal=True)
    accumulate(2 * i + 1, 1, q0=t, diagonal=True)

    lp = lam_ref[...]
    lam = (jnp.exp(jnp.sum(lp[0:1] * lp[1:2], axis=-1, keepdims=True))
           - jnp.exp(jnp.sum(lp[2:3] * lp[3:4], axis=-1, keepdims=True)) + lam_init)
    for hh in heads:
        ot = acc_ref[2 * hh] / l_ref[2 * hh] - lam * (acc_ref[2 * hh + 1] / l_ref[2 * hh + 1])
        ms = jnp.mean(ot * ot, axis=0, keepdims=True)
        ot = ot * lax.rsqrt(ms + RMS_EPS) * sw_ref[...] * (1.0 - lam_init)
        o_ref[0, :, hh * dv:(hh + 1) * dv] = ot.T.astype(o_ref.dtype)


def _diff_attention(qp, kp, vt, lam_p, subln_w, layer, lam_init):
    b, t, w = qp.shape
    tq, tk = 2 * ATT_TK, ATT_TK
    hw = ATT_HEADS * DA_V_DIM
    return pl.pallas_call(
        functools.partial(_attn_body, lam_init=lam_init),
        grid=(b, DA_HEADS // ATT_HEADS, t // tq),
        in_specs=[pl.BlockSpec((1, tq, hw), lambda bi, h, i: (bi, i, h)),
                  pl.BlockSpec((1, t, hw), lambda bi, h, i: (bi, 0, h)),
                  pl.BlockSpec((hw, t), lambda bi, h, i: (h, bi)),
                  pl.BlockSpec((None, 4, DA_HEAD_DIM), lambda bi, h, i: (layer, 0, 0)),
                  pl.BlockSpec((None, DA_V_DIM, 1), lambda bi, h, i: (layer, 0, 0))],
        out_specs=pl.BlockSpec((1, tq, hw), lambda bi, h, i: (bi, i, h)),
        out_shape=jax.ShapeDtypeStruct((b, t, w), BF16),
        scratch_shapes=[pltpu.VMEM((2 * ATT_HEADS, 1, tq), F32), pltpu.VMEM((2 * ATT_HEADS, 1, tq), F32),
                        pltpu.VMEM((2 * ATT_HEADS, DA_V_DIM, tq), F32),
                        pltpu.VMEM((2 * ATT_HEADS, tk, tq), F32), pltpu.VMEM((2 * ATT_HEADS, tk, tq), F32)],
        compiler_params=_cparams("parallel", "parallel", "parallel"),
        name="diff_attn",
    )(qp, kp, vt, lam_p, subln_w)


def _router_body(h_ref, w_ref, info_ref, gate_ref, cnt_ref, carry_ref):
    tm = h_ref.shape[0]

    @pl.when(pl.program_id(0) == 0)
    def _():
        carry_ref[...] = jnp.zeros_like(carry_ref)

    h = h_ref[...]
    w = w_ref[...]
    h_hi = h.astype(jnp.bfloat16)
    h_lo = (h - h_hi.astype(F32)).astype(jnp.bfloat16)
    w_hi = w.astype(jnp.bfloat16)
    w_lo = (w - w_hi.astype(F32)).astype(jnp.bfloat16)
    logits = (jnp.dot(h_hi, w_hi, preferred_element_type=F32) + jnp.dot(h_lo, w_hi, preferred_element_type=F32)
              + jnp.dot(h_hi, w_lo, preferred_element_type=F32))
    lane = lax.broadcasted_iota(I32, logits.shape, 1)
    big = jnp.int32(LANES)

    def first_max(vals):
        mx = jnp.max(vals, axis=-1, keepdims=True)
        idx = jnp.min(jnp.where(vals == mx, lane, big), axis=-1, keepdims=True)
        return mx, idx

    gl = jnp.where(lane < N_GROUPS, logits, -jnp.inf)
    gmax, gidx = first_max(gl)
    p_group = 1.0 / jnp.sum(jnp.exp(gl - gmax), axis=-1, keepdims=True)
    lo = N_GROUPS + gidx * EXPERTS_PER_GROUP
    el = jnp.where((lane >= lo) & (lane < lo + EXPERTS_PER_GROUP), logits, -jnp.inf)
    m1, i1 = first_max(el)
    m2, i2 = first_max(jnp.where(lane == i1, -jnp.inf, el))
    e2 = jnp.exp(m2 - m1)
    gate1 = p_group / (1.0 + e2)
    gate2 = p_group * e2 / (1.0 + e2)

    oh = ((lane == i1) | (lane == i2)).astype(F32)
    ri = lax.broadcasted_iota(I32, (tm, tm), 0)
    ci = lax.broadcasted_iota(I32, (tm, tm), 1)
    before = jnp.dot((ci < ri).astype(F32), oh, preferred_element_type=F32) + carry_ref[0:1, :]
    rank1 = jnp.sum(jnp.where(lane == i1, before, 0.0), axis=-1, keepdims=True).astype(I32)
    rank2 = jnp.sum(jnp.where(lane == i2, before, 0.0), axis=-1, keepdims=True).astype(I32)
    total = carry_ref[0:1, :] + jnp.sum(oh, axis=0, keepdims=True)
    carry_ref[0:1, :] = total

    info = jnp.where(lane == 0, i1 - N_GROUPS,
                     jnp.where(lane == 1, i2 - N_GROUPS,
                               jnp.where(lane == 2, rank1, jnp.where(lane == 3, rank2, 0))))
    info_ref[...] = info.T[:SUBLANES]
    gate_ref[...] = jnp.where(lane == 0, gate1, jnp.where(lane == 1, gate2, 0.0))
    cnt_ref[...] = jnp.broadcast_to(total, cnt_ref.shape).astype(I32)


def _moe_router(h, w_route):
    m, k = h.shape
    tm = ROUTE_TM
    return pl.pallas_call(
        _router_body,
        grid=(m // tm,),
        in_specs=[pl.BlockSpec((tm, k), lambda i: (i, 0)),
                  pl.BlockSpec((k, LANES), lambda i: (0, 0))],
        out_specs=[pl.BlockSpec((SUBLANES, tm), lambda i: (0, i)),
                   pl.BlockSpec((tm, LANES), lambda i: (i, 0)),
                   pl.BlockSpec((SUBLANES, LANES), lambda i: (0, 0))],
        out_shape=[jax.ShapeDtypeStruct((SUBLANES, m), I32),
                   jax.ShapeDtypeStruct((m, LANES), F32),
                   jax.ShapeDtypeStruct((SUBLANES, LANES), I32)],
        scratch_shapes=[pltpu.VMEM((SUBLANES, LANES), F32)],
        compiler_params=_cparams("arbitrary"),
        name="moe_router",
    )(h, w_route)


def _pos_body(start_ref, info_ref, pos_ref):
    info = info_ref[...]
    base = jnp.zeros_like(info)
    for e in range(N_EXPERTS):
        base = jnp.where(info == e, start_ref[e], base)
    pos_ref[...] = base + pltpu.roll(info, SUBLANES - TOP_K, axis=0)


def _moe_positions(info, pad_start):
    return pl.pallas_call(
        _pos_body,
        grid_spec=pltpu.PrefetchScalarGridSpec(
            num_scalar_prefetch=1, grid=(1,),
            in_specs=[pl.BlockSpec(info.shape, lambda i, *_: (0, 0))],
            out_specs=pl.BlockSpec(info.shape, lambda i, *_: (0, 0))),
        out_shape=jax.ShapeDtypeStruct(info.shape, I32),
        compiler_params=_cparams("arbitrary"),
        name="moe_positions",
    )(pad_start, info)


ZERO_ROWS = MOE_TB // 2


def _dispatch_body(pos0_ref, pos1_ref, zlo_ref, zhi_ref, x_ref, xs_hbm, zero_ref, sem, zsem):
    i = pl.program_id(0)
    tm = x_ref.shape[0]

    def zero_pieces(act):
        def per_expert(e, carry):
            lo = zlo_ref[e]
            hi = zhi_ref[e]
            n = hi - lo
            size = ZERO_ROWS
            end = hi
            while size >= SUBLANES:
                take = (n & size) != 0

                @pl.when(take)
                def _(end=end, size=size):
                    off = pl.multiple_of(end - size, SUBLANES)
                    act(pltpu.make_async_copy(zero_ref.at[pl.ds(0, size), :], xs_hbm.at[pl.ds(off, size), :], zsem))

                end = end - jnp.where(take, size, 0)
                size //= 2
            for r in range(SUBLANES - 1):
                @pl.when(r < (n & (SUBLANES - 1)))
                def _(r=r):
                    act(pltpu.make_async_copy(zero_ref.at[pl.ds(0, 1), :], xs_hbm.at[pl.ds(lo + r, 1), :], zsem))
            return carry

        lax.fori_loop(0, N_EXPERTS, per_expert, 0)

        def trailing(p, carry):
            off = pl.multiple_of(p * ZERO_ROWS, ZERO_ROWS)
            act(pltpu.make_async_copy(zero_ref, xs_hbm.at[pl.ds(off, ZERO_ROWS), :], zsem))
            return carry

        lax.fori_loop(zlo_ref[N_EXPERTS] // ZERO_ROWS, xs_hbm.shape[0] // ZERO_ROWS, trailing, 0)

    @pl.when(i == 0)
    def _():
        zero_ref[...] = jnp.zeros_like(zero_ref)
        zero_pieces(lambda cp: cp.start())

    def issue(r, carry):
        t = i * tm + r
        src = x_ref.at[pl.ds(r, 1), :]
        pltpu.make_async_copy(src, xs_hbm.at[pl.ds(pos0_ref[t], 1), :], sem).start()
        pltpu.make_async_copy(src, xs_hbm.at[pl.ds(pos1_ref[t], 1), :], sem).start()
        return carry

    lax.fori_loop(0, tm, issue, 0, unroll=8)
    for _ in range(TOP_K):
        pltpu.make_async_copy(x_ref, xs_hbm.at[pl.ds(0, tm), :], sem).wait()

    @pl.when(i == 0)
    def _():
        zero_pieces(lambda cp: cp.wait())


def _moe_dispatch(x, pos0, pos1, zlo, zhi, n_rows):
    m, d = x.shape
    tm = DISP_TM
    return pl.pallas_call(
        _dispatch_body,
        grid_spec=pltpu.PrefetchScalarGridSpec(
            num_scalar_prefetch=4, grid=(m // tm,),
            in_specs=[pl.BlockSpec((tm, d), lambda i, *_: (i, 0))],
            out_specs=pl.BlockSpec(memory_space=pl.ANY),
            scratch_shapes=[pltpu.VMEM((ZERO_ROWS, d), F32), pltpu.SemaphoreType.DMA(()),
                            pltpu.SemaphoreType.DMA(())]),
        out_shape=jax.ShapeDtypeStruct((n_rows, d), F32),
        compiler_params=_cparams("arbitrary"),
        name="moe_dispatch",
    )(pos0, pos1, zlo, zhi, x)


def _expert_body(eid_ref, nused_ref, x_ref, w13_ref, w2_ref, o_ref):
    ff = w2_ref.shape[0]

    used = pl.program_id(0) < nused_ref[0]

    @pl.when(used)
    def _():
        hcat = jnp.dot(x_ref[...], w13_ref[...], preferred_element_type=F32)
        hid = _silu(hcat[:, :ff]) * hcat[:, ff:]
        o_ref[...] = jnp.dot(hid, w2_ref[...], preferred_element_type=F32)

    @pl.when(jnp.logical_not(used))
    def _():
        o_ref[...] = jnp.zeros_like(o_ref)


def _moe_experts(xs, block_eid, n_used, w13, w2, layer):
    n_rows, d = xs.shape
    tb = MOE_TB
    ff = w2.shape[2]

    def row_block(b, eid, nused):
        return (jnp.minimum(b, nused[0] - 1), 0)

    return pl.pallas_call(
        _expert_body,
        grid_spec=pltpu.PrefetchScalarGridSpec(
            num_scalar_prefetch=2, grid=(n_rows // tb,),
            in_specs=[pl.BlockSpec((tb, d), row_block),
                      pl.BlockSpec((None, None, d, 2 * ff), lambda b, eid, nused: (layer, eid[b], 0, 0)),
                      pl.BlockSpec((None, None, ff, d), lambda b, eid, nused: (layer, eid[b], 0, 0))],
            out_specs=pl.BlockSpec((tb, d), lambda b, eid, nused: (b, 0))),
        out_shape=jax.ShapeDtypeStruct((n_rows, d), F32),
        compiler_params=_cparams("arbitrary"),
        name="moe_experts",
    )(block_eid, n_used, xs, w13, w2)


def _combine_body(pos0_ref, pos1_ref, ys_hbm, h_ref, gate_ref, g_ref, b_ref, o_ref, ob_ref, y0_ref, y1_ref, sem0, sem1):
    i = pl.program_id(0)
    tm = h_ref.shape[0]
    slot = i % 2

    def gather(tile, dst):
        def issue(r, carry):
            t = tile * tm + r
            pltpu.make_async_copy(ys_hbm.at[pl.ds(pos0_ref[t], 1), :], y0_ref.at[dst, pl.ds(r, 1), :],
                                  sem0.at[dst]).start()
            pltpu.make_async_copy(ys_hbm.at[pl.ds(pos1_ref[t], 1), :], y1_ref.at[dst, pl.ds(r, 1), :],
                                  sem1.at[dst]).start()
            return carry

        lax.fori_loop(0, tm, issue, 0, unroll=8)

    @pl.when(i == 0)
    def _():
        gather(0, 0)

    @pl.when(i + 1 < pl.num_programs(0))
    def _():
        gather(i + 1, 1 - slot)

    pltpu.make_async_copy(ys_hbm.at[pl.ds(0, tm), :], y0_ref.at[slot], sem0.at[slot]).wait()
    pltpu.make_async_copy(ys_hbm.at[pl.ds(0, tm), :], y1_ref.at[slot], sem1.at[slot]).wait()
    gates = gate_ref[...]
    ffn = gates[:, 0:1] * y0_ref[slot] + gates[:, 1:2] * y1_ref[slot]
    out = _layer_norm(ALPHA * h_ref[...] + ffn, g_ref[...], b_ref[...])
    o_ref[...] = out
    ob_ref[...] = _mxu(out)


def _moe_combine(ys, pos0, pos1, h, gates, g, b, layer):
    m, d = h.shape
    tm = COMB_TM
    row = pl.BlockSpec((None, 1, d), lambda i, *_: (layer, 0, 0))
    return pl.pallas_call(
        _combine_body,
        grid_spec=pltpu.PrefetchScalarGridSpec(
            num_scalar_prefetch=2, grid=(m // tm,),
            in_specs=[pl.BlockSpec(memory_space=pl.ANY),
                      pl.BlockSpec((tm, d), lambda i, *_: (i, 0)),
                      pl.BlockSpec((tm, LANES), lambda i, *_: (i, 0)), row, row],
            out_specs=[pl.BlockSpec((tm, d), lambda i, *_: (i, 0)), pl.BlockSpec((tm, d), lambda i, *_: (i, 0))],
            scratch_shapes=[pltpu.VMEM((2, tm, d), F32), pltpu.VMEM((2, tm, d), F32),
                            pltpu.SemaphoreType.DMA((2,)), pltpu.SemaphoreType.DMA((2,))]),
        out_shape=[jax.ShapeDtypeStruct((m, d), F32), jax.ShapeDtypeStruct((m, d), BF16)],
        compiler_params=_cparams("arbitrary"),
        name="moe_combine",
    )(pos0, pos1, ys, h, gates, g, b)


def _moe_layer(h, w_group, w_expert, w13, w2, ln_g, ln_b, layer):
    n, d = h.shape
    tb = MOE_TB
    w_route = jnp.pad(jnp.concatenate([w_group, w_expert], axis=1), ((0, 0), (0, LANES - N_GROUPS - N_EXPERTS)))
    info, gates, cnt = _moe_router(h, w_route)
    counts = cnt[0, N_GROUPS:N_GROUPS + N_EXPERTS]
    padded = (counts + tb - 1) // tb * tb
    pad_end = jnp.cumsum(padded)
    pad_start = pad_end - padded
    pos = _moe_positions(info, pad_start)
    pos0, pos1 = pos[0], pos[1]
    n_blocks = (n * TOP_K + N_EXPERTS * (tb - 1) + tb - 1) // tb
    n_used = (pad_end[-1] // tb).astype(I32)
    blk = jnp.minimum(jnp.arange(n_blocks, dtype=I32), n_used - 1)
    block_eid = jnp.minimum(jnp.sum(pad_end[None, :] <= (blk * tb)[:, None], axis=1), N_EXPERTS - 1).astype(I32)
    zlo = jnp.concatenate([pad_start + counts, pad_end[-1:]])
    xs = _moe_dispatch(h, pos0, pos1, zlo, pad_end, n_blocks * tb)
    ys = _moe_experts(xs, block_eid, n_used.reshape(1), w13, w2, layer)
    return _moe_combine(ys, pos0, pos1, h, gates, ln_g, ln_b, layer)


def _lambda_init(layer_idx):
    return 0.8 - 0.6 * math.exp(-0.3 * layer_idx)


def kernel(x, a_w_in, a_conv_w, a_a_log, a_dt_bias, a_norm_w, a_w_out, kv_w, b_w_q, b_lambda, b_subln_w, b_w_out,
           ln_mix_g, ln_mix_b, ln_ffn_g, ln_ffn_b, moe_w_group, moe_w_expert, moe_w13, moe_w2):
    b, t, d = x.shape
    n = b * t
    dn_w = DN_HEADS * DN_HEAD_DIM
    h = x.reshape(n, d)

    def per_layer_rows(p):
        return p.reshape(p.shape[0], 1, p.shape[1])

    a_norm_w = per_layer_rows(a_norm_w)
    b_subln_w = b_subln_w.reshape(b_subln_w.shape + (1,))
    ln_mix_g, ln_mix_b = per_layer_rows(ln_mix_g), per_layer_rows(ln_mix_b)
    ln_ffn_g, ln_ffn_b = per_layer_rows(ln_ffn_g), per_layer_rows(ln_ffn_b)
    da_w = DA_HEADS * DA_V_DIM
    w_in, w_out_a = _mxu(a_w_in), _mxu(a_w_out)
    w_k, w_vt = _mxu(kv_w[:, :da_w]).reshape(1, d, da_w), _mxu(kv_w[:, da_w:].T)
    w_q, w_out_b = _mxu(b_w_q), _mxu(b_w_out)
    kp = vt = None
    hx = h
    for layer in range(DEPTH):
        if layer < N_A_LAYERS:
            qkv = _dn_qkv_proj(hx, w_in, a_conv_w, layer, t)
            z = _matmul(hx, w_in, layer, dn_w, "dn_z_proj", BF16, col0=3 * dn_w)
            gb = _dn_gates(hx, a_w_in[layer, :, 4 * dn_w:], a_a_log[layer], a_dt_bias[layer]).reshape(b, t, LANES)
            mix_in = _dn_delta(qkv.reshape(b, t, 3 * dn_w), z.reshape(b, t, dn_w), gb, a_norm_w, layer).reshape(n, dn_w)
            h = _proj_res_ln(mix_in, w_out_a, h, ln_mix_g, ln_mix_b, layer, layer, "dn_out_ln")
        else:
            j = layer - N_A_LAYERS
            if j == 0:
                kp = _matmul(hx, w_k, 0, da_w, "k_proj", BF16).reshape(b, t, da_w)
                vt = _matmul_nt(w_vt, hx, "v_proj", BF16)
            qp = _matmul(hx, w_q, j, da_w, "q_proj", BF16).reshape(b, t, da_w)
            mix_in = _diff_attention(qp, kp, vt, b_lambda, b_subln_w, j, _lambda_init(layer)).reshape(n, da_w)
            h = _proj_res_ln(mix_in, w_out_b, h, ln_mix_g, ln_mix_b, j, layer, "da_out_ln")
        h, hx = _moe_layer(h, moe_w_group[layer], moe_w_expert[layer], moe_w13, moe_w2, ln_ffn_g, ln_ffn_b, layer)
    return h.reshape(b, t, d)
```

```python
import functools
import math

import jax
import jax.numpy as jnp
from jax import lax
from jax.experimental import pallas as pl
from jax.experimental.pallas import tpu as pltpu

F32 = jnp.float32
BF16 = jnp.bfloat16
I32 = jnp.int32

DEPTH = 4
N_A_LAYERS = DEPTH // 2
DN_HEADS = 8
DN_HEAD_DIM = 128
CONV_K = 4
DN_CHUNK = 64
DA_HEADS = 8
DA_HEAD_DIM = 64
DA_V_DIM = 2 * DA_HEAD_DIM
N_GROUPS = 4
EXPERTS_PER_GROUP = 8
N_EXPERTS = N_GROUPS * EXPERTS_PER_GROUP
TOP_K = 2
ALPHA = (2 * DEPTH) ** 0.25
LN_EPS = 1e-5
RMS_EPS = 1e-6

LANES = 128
SUBLANES = 8
BF16_SUBLANES = 16
VMEM_LIMIT_BYTES = 56 * 1024 * 1024

MM_TM = 2048
MM_TN = 512
LN_TM = 512
DN_ROWS = 256
ATT_TK = 512
ATT_HEADS = 2
ROUTE_TM = 512
MOE_TB = 256
DISP_TM = 512
COMB_TM = 256


def _cparams(*sem):
    return pltpu.CompilerParams(dimension_semantics=sem, vmem_limit_bytes=VMEM_LIMIT_BYTES)


def _silu(x):
    return x * (1.0 / (1.0 + jnp.exp(-x)))


def _layer_norm(x, g, b):
    mu = jnp.mean(x, axis=-1, keepdims=True)
    xc = x - mu
    var = jnp.mean(xc * xc, axis=-1, keepdims=True)
    return xc * lax.rsqrt(var + LN_EPS) * g + b


def _mxu(x):
    return x.astype(BF16)


def _mm_body(x_ref, w_ref, o_ref):
    o_ref[...] = jnp.dot(_mxu(x_ref[...]), _mxu(w_ref[...]), preferred_element_type=F32).astype(o_ref.dtype)


def _matmul(x, w, layer, n_out, name, out_dtype=F32, col0=0):
    m, k = x.shape
    j0 = col0 // MM_TN
    return pl.pallas_call(
        _mm_body,
        grid=(m // MM_TM, n_out // MM_TN),
        in_specs=[pl.BlockSpec((MM_TM, k), lambda i, j: (i, 0)),
                  pl.BlockSpec((None, k, MM_TN), lambda i, j: (layer, 0, j0 + j))],
        out_specs=pl.BlockSpec((MM_TM, MM_TN), lambda i, j: (i, j)),
        out_shape=jax.ShapeDtypeStruct((m, n_out), out_dtype),
        compiler_params=_cparams("parallel", "parallel"),
        name=name,
    )(x, w)


def _gates_body(x_ref, w_ref, alog_ref, dtb_ref, o_ref):
    logit = jnp.dot(_mxu(x_ref[...]), _mxu(w_ref[...]), preferred_element_type=F32)
    lane = lax.broadcasted_iota(I32, logit.shape, 1)
    beta = 1.0 / (1.0 + jnp.exp(-logit))
    sp_in = logit + dtb_ref[...]
    softplus = jnp.maximum(sp_in, 0.0) + jnp.log1p(jnp.exp(-jnp.abs(sp_in)))
    g = -jnp.exp(alog_ref[...]) * softplus
    o_ref[...] = jnp.where(lane < DN_HEADS, beta, g)


def _dn_gates(x, w_small, a_log, dt_bias):
    m, k = x.shape
    pad = LANES - 2 * DN_HEADS
    w = jnp.pad(w_small, ((0, 0), (0, pad)))
    alog = jnp.pad(a_log, (DN_HEADS, pad)).reshape(1, LANES)
    dtb = jnp.pad(dt_bias, (DN_HEADS, pad)).reshape(1, LANES)
    return pl.pallas_call(
        _gates_body,
        grid=(m // MM_TM,),
        in_specs=[pl.BlockSpec((MM_TM, k), lambda i: (i, 0)),
                  pl.BlockSpec((k, LANES), lambda i: (0, 0)),
                  pl.BlockSpec((1, LANES), lambda i: (0, 0)),
                  pl.BlockSpec((1, LANES), lambda i: (0, 0))],
        out_specs=pl.BlockSpec((MM_TM, LANES), lambda i: (i, 0)),
        out_shape=jax.ShapeDtypeStruct((m, LANES), F32),
        compiler_params=_cparams("parallel"),
        name="dn_gates",
    )(x, w, alog, dtb)


def _qkv_proj_body(x_ref, w_ref, cw_ref, qkv_ref, halo_ref, *, tiles_per_seq, tiles_per_part):
    i, j = pl.program_id(0), pl.program_id(1)
    proj = jnp.dot(_mxu(x_ref[...]), _mxu(w_ref[...]), preferred_element_type=F32)
    tm, tn = proj.shape
    prev = jnp.where(i % tiles_per_seq == 0, 0.0, halo_ref[j])
    halo_ref[j] = proj[tm - SUBLANES:]
    xe = jnp.concatenate([prev, proj], axis=0)
    acc = proj * cw_ref[CONV_K - 1:CONV_K, :]
    for tap in range(CONV_K - 1):
        shifted = pltpu.roll(xe, CONV_K - 1 - tap, axis=0)[SUBLANES:]
        acc = acc + shifted * cw_ref[tap:tap + 1, :]
    y = _silu(acc)
    normed = []
    for h in range(tn // DN_HEAD_DIM):
        yh = y[:, h * DN_HEAD_DIM:(h + 1) * DN_HEAD_DIM]
        ss = jnp.sum(yh * yh, axis=-1, keepdims=True)
        normed.append(yh * lax.rsqrt(ss + RMS_EPS))
    normed = jnp.concatenate(normed, axis=-1)
    q_scale = jnp.where(j < tiles_per_part, DN_HEAD_DIM ** -0.5, 1.0)
    qkv_ref[...] = jnp.where(j < 2 * tiles_per_part, normed * q_scale, y).astype(qkv_ref.dtype)


def _dn_qkv_proj(x, w_in, conv_w, layer, seq_len):
    m, k = x.shape
    width = DN_HEADS * DN_HEAD_DIM
    tm, tn = MM_TM, MM_TN
    conv_tiles = 3 * width // tn
    body = functools.partial(_qkv_proj_body, tiles_per_seq=seq_len // tm, tiles_per_part=width // tn)
    return pl.pallas_call(
        body,
        grid=(m // tm, conv_tiles),
        in_specs=[pl.BlockSpec((tm, k), lambda i, j: (i, 0)),
                  pl.BlockSpec((None, k, tn), lambda i, j: (layer, 0, j)),
                  pl.BlockSpec((None, CONV_K, tn), lambda i, j: (layer, 0, j))],
        out_specs=pl.BlockSpec((tm, tn), lambda i, j: (i, j)),
        out_shape=jax.ShapeDtypeStruct((m, 3 * width), BF16),
        scratch_shapes=[pltpu.VMEM((conv_tiles, SUBLANES, tn), F32)],
        compiler_params=_cparams("arbitrary", "arbitrary"),
        name="dn_qkv_proj",
    )(x, w_in, conv_w)


def _split3(x):
    hi = x.astype(jnp.bfloat16).astype(F32)
    r = x - hi
    mid = r.astype(jnp.bfloat16).astype(F32)
    return hi, mid, r - mid


def _nt(a, b):
    return lax.dot_general(a, b, (((1,), (1,)), ((), ())), preferred_element_type=F32)


def _tn(a, b):
    return lax.dot_general(a, b, (((0,), (0,)), ((), ())), preferred_element_type=F32)


def _delta_body(u_ref, w_ref, qd_ref, kd_ref, attn_ref, egl_ref, z_ref, nw_ref, o_ref, state_ref):
    c = DN_CHUNK
    d = DN_HEAD_DIM

    @pl.when(pl.program_id(0) == 0)
    def _():
        state_ref[...] = jnp.zeros_like(state_ref)

    nw = nw_ref[...]
    nb = u_ref.shape[0]
    chains = [(bi, h) for bi in range(nb) for h in range(DN_HEADS)]

    def chunk(ic, carry):
        rows = pl.ds(pl.multiple_of(ic * c, c), c)
        s_old, ws_qs = {}, {}
        for bi, h in chains:
            cols = slice(h * d, (h + 1) * d)
            s_old[bi, h] = state_ref[bi * DN_HEADS + h]
            lhs = jnp.concatenate([w_ref[bi, rows, cols], qd_ref[bi, rows, cols]], axis=0)
            ws_qs[bi, h] = jnp.dot(lhs, _mxu(s_old[bi, h]), preferred_element_type=F32)
        v_new = {}
        for bi, h in chains:
            cols = slice(h * d, (h + 1) * d)
            v_new[bi, h] = u_ref[bi, rows, cols] - ws_qs[bi, h][:c]
        for bi, h in chains:
            cols = slice(h * d, (h + 1) * d)
            acols = slice(h * c, (h + 1) * c)
            vn = _mxu(v_new[bi, h])
            o = ws_qs[bi, h][c:] + jnp.dot(attn_ref[bi, rows, acols], vn, preferred_element_type=F32)
            gl = egl_ref[bi, pl.ds(ic * DN_HEADS + h, 1), :]
            state_ref[bi * DN_HEADS + h] = s_old[bi, h] * gl + _tn(kd_ref[bi, rows, cols], vn)
            ms = jnp.mean(o * o, axis=-1, keepdims=True)
            z = z_ref[bi, rows, cols].astype(F32)
            o_ref[bi, rows, cols] = (o * lax.rsqrt(ms + RMS_EPS) * nw * _silu(z)).astype(o_ref.dtype)
        return carry

    lax.fori_loop(0, u_ref.shape[1] // c, chunk, 0)


def _dn_prep_body(q_ref, k_ref, v_ref, gb_ref, u_ref, w_ref, qd_ref, kd_ref, attn_ref, egl_ref):
    c = DN_CHUNK
    d = DN_HEAD_DIM
    ri = lax.broadcasted_iota(I32, (c, c), 0)
    ci = lax.broadcasted_iota(I32, (c, c), 1)
    causal = ri >= ci
    strict = ri > ci
    tri = causal.astype(F32)
    eye = (ri == ci).astype(F32)
    base = 4
    base_mask = strict & (ri // base == ci // base)
    pair_masks = [(ri // (2 * s) == ci // (2 * s)) & ((ri // s) % 2 == 1) & ((ci // s) % 2 == 0)
                  for s in (base << n for n in range(int(math.log2(c // base))))]
    heads = range(DN_HEADS)

    def chunk(ic, carry):
        rows = pl.ds(pl.multiple_of(ic * c, c), c)
        gbt = gb_ref[0, rows, :]
        g_hi, g_mid, g_lo = _split3(gbt)
        gcol = (jnp.dot(tri, g_hi, preferred_element_type=F32)
                + jnp.dot(tri, g_mid, preferred_element_type=F32)
                + jnp.dot(tri, g_lo, preferred_element_type=F32))
        grow = gcol.T
        g_last_col = grow[DN_HEADS:2 * DN_HEADS, c - 1:c]
        egl_ref[0, pl.ds(pl.multiple_of(ic * DN_HEADS, DN_HEADS), DN_HEADS), :] = jnp.broadcast_to(
            jnp.exp(g_last_col), (DN_HEADS, d))
        a_mat, x, p = {}, {}, {}
        for h in heads:
            cols = slice(h * d, (h + 1) * d)
            qh = q_ref[0, rows, cols]
            kh = k_ref[0, rows, cols]
            qf, kf = qh.astype(F32), kh.astype(F32)
            beta_c = gbt[:, h:h + 1]
            gc_c = gcol[:, DN_HEADS + h:DN_HEADS + h + 1]
            gc_r = grow[DN_HEADS + h:DN_HEADS + h + 1, :]
            decay = jnp.exp(jnp.where(causal, gc_c - gc_r, -jnp.inf))
            eg_c = jnp.exp(gc_c)
            k_beta = kf * beta_c
            a_mat[h] = jnp.where(strict, _nt(_mxu(k_beta), kh) * decay, 0.0)
            attn_ref[0, rows, h * c:(h + 1) * c] = _mxu(jnp.where(causal, _nt(qh, kh) * decay, 0.0))
            qd_ref[0, rows, cols] = _mxu(qf * eg_c)
            kd_ref[0, rows, cols] = _mxu(kf * jnp.exp(gc_r[:, c - 1:c] - gc_c))
            x[h] = jnp.concatenate([v_ref[0, rows, cols].astype(F32) * beta_c, k_beta * eg_c], axis=-1)
        t_inv = {}
        for h in heads:
            a0 = jnp.where(base_mask, a_mat[h], 0.0)
            p[h] = jnp.dot(a0, a0, preferred_element_type=F32)
            t_inv[h] = eye - a0 + p[h]
        for h in heads:
            a0 = jnp.where(base_mask, a_mat[h], 0.0)
            t_inv[h] = t_inv[h] - jnp.dot(a0, p[h], preferred_element_type=F32)
        for pair_mask in pair_masks:
            for h in heads:
                p[h] = jnp.dot(jnp.where(pair_mask, a_mat[h], 0.0), t_inv[h], preferred_element_type=F32)
            for h in heads:
                t_inv[h] = t_inv[h] - jnp.dot(t_inv[h], p[h], preferred_element_type=F32)
        for h in heads:
            x[h] = x[h] + jnp.dot(t_inv[h] - eye, x[h], preferred_element_type=F32)
        for h in heads:
            cols = slice(h * d, (h + 1) * d)
            u_ref[0, rows, cols] = x[h][:, :d]
            w_ref[0, rows, cols] = _mxu(x[h][:, d:])
        return carry

    lax.fori_loop(0, q_ref.shape[1] // c, chunk, 0)


def _dn_delta(qkv, z, gb, norm_w, layer):
    b, t, w = z.shape
    rs = DN_ROWS
    cps = rs // DN_CHUNK
    aw = DN_HEADS * DN_CHUNK
    spec = pl.BlockSpec((1, rs, w), lambda bi, i: (bi, i, 0))
    full = jax.ShapeDtypeStruct((b, t, w), F32)
    mxu_only = jax.ShapeDtypeStruct((b, t, w), BF16)
    u, wy, qd, kd, attn, egl = pl.pallas_call(
        _dn_prep_body,
        grid=(b, t // rs),
        in_specs=[pl.BlockSpec((1, rs, w), lambda bi, i: (bi, i, 0)),
                  pl.BlockSpec((1, rs, w), lambda bi, i: (bi, i, 1)),
                  pl.BlockSpec((1, rs, w), lambda bi, i: (bi, i, 2)),
                  pl.BlockSpec((1, rs, LANES), lambda bi, i: (bi, i, 0))],
        out_specs=[spec, spec, spec, spec,
                   pl.BlockSpec((1, rs, aw), lambda bi, i: (bi, i, 0)),
                   pl.BlockSpec((1, cps * DN_HEADS, DN_HEAD_DIM), lambda bi, i: (bi, i, 0))],
        out_shape=[full, mxu_only, mxu_only, mxu_only,
                   jax.ShapeDtypeStruct((b, t, aw), BF16),
                   jax.ShapeDtypeStruct((b, t // DN_CHUNK * DN_HEADS, DN_HEAD_DIM), F32)],
        compiler_params=_cparams("parallel", "parallel"),
        name="dn_prep",
    )(qkv, qkv, qkv, gb)
    bspec = pl.BlockSpec((b, rs, w), lambda i: (0, i, 0))
    return pl.pallas_call(
        _delta_body,
        grid=(t // rs,),
        in_specs=[bspec, bspec, bspec, bspec,
                  pl.BlockSpec((b, rs, aw), lambda i: (0, i, 0)),
                  pl.BlockSpec((b, cps * DN_HEADS, DN_HEAD_DIM), lambda i: (0, i, 0)),
                  bspec,
                  pl.BlockSpec((None, 1, DN_HEAD_DIM), lambda i: (layer, 0, 0))],
        out_specs=bspec,
        out_shape=mxu_only,
        scratch_shapes=[pltpu.VMEM((b * DN_HEADS, DN_HEAD_DIM, DN_HEAD_DIM), F32)],
        compiler_params=_cparams("arbitrary"),
        name="dn_scan",
    )(u, wy, qd, kd, attn, egl, z, norm_w)


def _proj_ln_body(a_ref, w_ref, h_ref, g_ref, b_ref, o_ref):
    y = jnp.dot(_mxu(a_ref[...]), _mxu(w_ref[...]), preferred_element_type=F32)
    o_ref[...] = _layer_norm(ALPHA * h_ref[...] + y, g_ref[...], b_ref[...])


def _proj_res_ln(a, w, h, g, b, w_layer, ln_layer, name):
    m, k = a.shape
    n = h.shape[1]
    row = pl.BlockSpec((None, 1, n), lambda i: (ln_layer, 0, 0))
    return pl.pallas_call(
        _proj_ln_body,
        grid=(m // LN_TM,),
        in_specs=[pl.BlockSpec((LN_TM, k), lambda i: (i, 0)),
                  pl.BlockSpec((None, k, n), lambda i: (w_layer, 0, 0)),
                  pl.BlockSpec((LN_TM, n), lambda i: (i, 0)), row, row],
        out_specs=pl.BlockSpec((LN_TM, n), lambda i: (i, 0)),
        out_shape=jax.ShapeDtypeStruct((m, n), F32),
        compiler_params=_cparams("parallel"),
        name=name,
    )(a, w, h, g, b)


def _mm_nt_body(w_ref, x_ref, o_ref):
    o_ref[...] = _nt(_mxu(w_ref[...]), _mxu(x_ref[...])).astype(o_ref.dtype)


def _matmul_nt(w_t, x, name, out_dtype=F32):
    n, k = w_t.shape
    m = x.shape[0]
    return pl.pallas_call(
        _mm_nt_body,
        grid=(n // MM_TN, m // MM_TM),
        in_specs=[pl.BlockSpec((MM_TN, k), lambda j, i: (j, 0)),
                  pl.BlockSpec((MM_TM, k), lambda j, i: (i, 0))],
        out_specs=pl.BlockSpec((MM_TN, MM_TM), lambda j, i: (j, i)),
        out_shape=jax.ShapeDtypeStruct((n, m), out_dtype),
        compiler_params=_cparams("parallel", "parallel"),
        name=name,
    )(w_t, x)


def _attn_body(q_ref, k_ref, vt_ref, lam_ref, sw_ref, o_ref, m_ref, l_ref, acc_ref, sta_ref, stb_ref, *, lam_init):
    i = pl.program_id(2)
    t = ATT_TK
    dv = DA_V_DIM
    heads = range(ATT_HEADS)
    maps = [(hh, s) for hh in heads for s in range(2)]
    q_maps = []
    for hh in heads:
        q = q_ref[0, :, hh * dv:(hh + 1) * dv].astype(F32) * (DA_HEAD_DIM ** -0.5 * math.log2(math.e))
        lane = lax.broadcasted_iota(I32, q.shape, 1)
        q_maps += [_mxu(jnp.where(lane < DA_HEAD_DIM, q, 0.0)), _mxu(jnp.where(lane >= DA_HEAD_DIM, q, 0.0))]
    m_ref[...] = jnp.full_like(m_ref, -jnp.inf)
    l_ref[...] = jnp.zeros_like(l_ref)
    acc_ref[...] = jnp.zeros_like(acc_ref)

    st_bufs = (sta_ref, stb_ref)

    def scores(j, buf, q0=0):
        rows = pl.ds(pl.multiple_of(j * t, t), t)
        for hh, s in maps:
            k = k_ref[0, rows, hh * dv:(hh + 1) * dv]
            st_bufs[buf][2 * hh + s, :, q0:] = _nt(k, q_maps[2 * hh + s][q0:])

    def accumulate(j, buf, q0=0, diagonal=False):
        rows = pl.ds(pl.multiple_of(j * t, t), t)
        ones = jnp.ones((BF16_SUBLANES, t), BF16)
        vt = [jnp.concatenate([vt_ref[hh * dv:(hh + 1) * dv, rows], ones], axis=0) for hh in heads]
        idx = [2 * hh + s for hh, s in maps]
        st = [st_bufs[buf][x, :, q0:] for x in idx]
        if diagonal:
            ri = lax.broadcasted_iota(I32, st[0].shape, 0)
            ci = lax.broadcasted_iota(I32, st[0].shape, 1)
            st = [jnp.where(ri <= ci, x, -jnp.inf) for x in st]
        m_old = [m_ref[x, :, q0:] for x in idx]
        m_new = [jnp.maximum(mo, jnp.max(x, axis=0, keepdims=True)) for mo, x in zip(m_old, st)]
        p = [_mxu(jnp.exp2(x - mn)) for x, mn in zip(st, m_new)]
        pv = [jnp.dot(vt[x // 2], px, preferred_element_type=F32) for x, px in zip(idx, p)]
        for n, x in enumerate(idx):
            corr = jnp.exp2(m_old[n] - m_new[n])
            l_ref[x, :, q0:] = corr * l_ref[x, :, q0:] + pv[n][dv:dv + 1]
            acc_ref[x, :, q0:] = corr * acc_ref[x, :, q0:] + pv[n][:dv]
            m_ref[x, :, q0:] = m_new[n]

    scores(0, 0)

    def two_tiles(u, carry):
        j = 2 * u
        scores(j + 1, 1)
        accumulate(j, 0)
        scores(j + 2, 0)
        accumulate(j + 1, 1)
        return carry

    lax.fori_loop(0, i, two_tiles, 0)
    scores(2 * i + 1, 1, q0=t)
    accumulate(2 * i, 0, diagonal=True)
    accumulate(2 * i + 1, 1, q0=t, diagonal=True)

    lp = lam_ref[...]
    lam = (jnp.exp(jnp.sum(lp[0:1] * lp[1:2], axis=-1, keepdims=True))
           - jnp.exp(jnp.sum(lp[2:3] * lp[3:4], axis=-1, keepdims=True)) + lam_init)
    for hh in heads:
        ot = acc_ref[2 * hh] / l_ref[2 * hh] - lam * (acc_ref[2 * hh + 1] / l_ref[2 * hh + 1])
        ms = jnp.mean(ot * ot, axis=0, keepdims=True)
        ot = ot * lax.rsqrt(ms + RMS_EPS) * sw_ref[...] * (1.0 - lam_init)
        o_ref[0, :, hh * dv:(hh + 1) * dv] = ot.T.astype(o_ref.dtype)


def _diff_attention(qp, kp, vt, lam_p, subln_w, layer, lam_init):
    b, t, w = qp.shape
    tq, tk = 2 * ATT_TK, ATT_TK
    hw = ATT_HEADS * DA_V_DIM
    return pl.pallas_call(
        functools.partial(_attn_body, lam_init=lam_init),
        grid=(b, DA_HEADS // ATT_HEADS, t // tq),
        in_specs=[pl.BlockSpec((1, tq, hw), lambda bi, h, i: (bi, i, h)),
                  pl.BlockSpec((1, t, hw), lambda bi, h, i: (bi, 0, h)),
                  pl.BlockSpec((hw, t), lambda bi, h, i: (h, bi)),
                  pl.BlockSpec((None, 4, DA_HEAD_DIM), lambda bi, h, i: (layer, 0, 0)),
                  pl.BlockSpec((None, DA_V_DIM, 1), lambda bi, h, i: (layer, 0, 0))],
        out_specs=pl.BlockSpec((1, tq, hw), lambda bi, h, i: (bi, i, h)),
        out_shape=jax.ShapeDtypeStruct((b, t, w), BF16),
        scratch_shapes=[pltpu.VMEM((2 * ATT_HEADS, 1, tq), F32), pltpu.VMEM((2 * ATT_HEADS, 1, tq), F32),
                        pltpu.VMEM((2 * ATT_HEADS, DA_V_DIM, tq), F32),
                        pltpu.VMEM((2 * ATT_HEADS, tk, tq), F32), pltpu.VMEM((2 * ATT_HEADS, tk, tq), F32)],
        compiler_params=_cparams("parallel", "parallel", "parallel"),
        name="diff_attn",
    )(qp, kp, vt, lam_p, subln_w)


def _router_body(h_ref, w_ref, info_ref, gate_ref, cnt_ref, carry_ref):
    tm = h_ref.shape[0]

    @pl.when(pl.program_id(0) == 0)
    def _():
        carry_ref[...] = jnp.zeros_like(carry_ref)

    h = h_ref[...]
    w = w_ref[...]
    h_hi = h.astype(jnp.bfloat16)
    h_lo = (h - h_hi.astype(F32)).astype(jnp.bfloat16)
    w_hi = w.astype(jnp.bfloat16)
    w_lo = (w - w_hi.astype(F32)).astype(jnp.bfloat16)
    logits = (jnp.dot(h_hi, w_hi, preferred_element_type=F32) + jnp.dot(h_lo, w_hi, preferred_element_type=F32)
              + jnp.dot(h_hi, w_lo, preferred_element_type=F32))
    lane = lax.broadcasted_iota(I32, logits.shape, 1)
    big = jnp.int32(LANES)

    def first_max(vals):
        mx = jnp.max(vals, axis=-1, keepdims=True)
        idx = jnp.min(jnp.where(vals == mx, lane, big), axis=-1, keepdims=True)
        return mx, idx

    gl = jnp.where(lane < N_GROUPS, logits, -jnp.inf)
    gmax, gidx = first_max(gl)
    p_group = 1.0 / jnp.sum(jnp.exp(gl - gmax), axis=-1, keepdims=True)
    lo = N_GROUPS + gidx * EXPERTS_PER_GROUP
    el = jnp.where((lane >= lo) & (lane < lo + EXPERTS_PER_GROUP), logits, -jnp.inf)
    m1, i1 = first_max(el)
    m2, i2 = first_max(jnp.where(lane == i1, -jnp.inf, el))
    e2 = jnp.exp(m2 - m1)
    gate1 = p_group / (1.0 + e2)
    gate2 = p_group * e2 / (1.0 + e2)

    oh = ((lane == i1) | (lane == i2)).astype(F32)
    ri = lax.broadcasted_iota(I32, (tm, tm), 0)
    ci = lax.broadcasted_iota(I32, (tm, tm), 1)
    before = jnp.dot((ci < ri).astype(F32), oh, preferred_element_type=F32) + carry_ref[0:1, :]
    rank1 = jnp.sum(jnp.where(lane == i1, before, 0.0), axis=-1, keepdims=True).astype(I32)
    rank2 = jnp.sum(jnp.where(lane == i2, before, 0.0), axis=-1, keepdims=True).astype(I32)
    total = carry_ref[0:1, :] + jnp.sum(oh, axis=0, keepdims=True)
    carry_ref[0:1, :] = total

    info = jnp.where(lane == 0, i1 - N_GROUPS,
                     jnp.where(lane == 1, i2 - N_GROUPS,
                               jnp.where(lane == 2, rank1, jnp.where(lane == 3, rank2, 0))))
    info_ref[...] = info.T[:SUBLANES]
    gate_ref[...] = jnp.where(lane == 0, gate1, jnp.where(lane == 1, gate2, 0.0))
    cnt_ref[...] = jnp.broadcast_to(total, cnt_ref.shape).astype(I32)


def _moe_router(h, w_route):
    m, k = h.shape
    tm = ROUTE_TM
    return pl.pallas_call(
        _router_body,
        grid=(m // tm,),
        in_specs=[pl.BlockSpec((tm, k), lambda i: (i, 0)),
                  pl.BlockSpec((k, LANES), lambda i: (0, 0))],
        out_specs=[pl.BlockSpec((SUBLANES, tm), lambda i: (0, i)),
                   pl.BlockSpec((tm, LANES), lambda i: (i, 0)),
                   pl.BlockSpec((SUBLANES, LANES), lambda i: (0, 0))],
        out_shape=[jax.ShapeDtypeStruct((SUBLANES, m), I32),
                   jax.ShapeDtypeStruct((m, LANES), F32),
                   jax.ShapeDtypeStruct((SUBLANES, LANES), I32)],
        scratch_shapes=[pltpu.VMEM((SUBLANES, LANES), F32)],
        compiler_params=_cparams("arbitrary"),
        name="moe_router",
    )(h, w_route)


def _pos_body(start_ref, info_ref, pos_ref):
    info = info_ref[...]
    base = jnp.zeros_like(info)
    for e in range(N_EXPERTS):
        base = jnp.where(info == e, start_ref[e], base)
    pos_ref[...] = base + pltpu.roll(info, SUBLANES - TOP_K, axis=0)


def _moe_positions(info, pad_start):
    return pl.pallas_call(
        _pos_body,
        grid_spec=pltpu.PrefetchScalarGridSpec(
            num_scalar_prefetch=1, grid=(1,),
            in_specs=[pl.BlockSpec(info.shape, lambda i, *_: (0, 0))],
            out_specs=pl.BlockSpec(info.shape, lambda i, *_: (0, 0))),
        out_shape=jax.ShapeDtypeStruct(info.shape, I32),
        compiler_params=_cparams("arbitrary"),
        name="moe_positions",
    )(pad_start, info)


ZERO_ROWS = MOE_TB // 2


def _dispatch_body(pos0_ref, pos1_ref, zlo_ref, zhi_ref, x_ref, xs_hbm, zero_ref, sem, zsem):
    i = pl.program_id(0)
    tm = x_ref.shape[0]

    def zero_pieces(act):
        def per_expert(e, carry):
            lo = zlo_ref[e]
            hi = zhi_ref[e]
            n = hi - lo
            size = ZERO_ROWS
            end = hi
            while size >= SUBLANES:
                take = (n & size) != 0

                @pl.when(take)
                def _(end=end, size=size):
                    off = pl.multiple_of(end - size, SUBLANES)
                    act(pltpu.make_async_copy(zero_ref.at[pl.ds(0, size), :], xs_hbm.at[pl.ds(off, size), :], zsem))

                end = end - jnp.where(take, size, 0)
                size //= 2
            for r in range(SUBLANES - 1):
                @pl.when(r < (n & (SUBLANES - 1)))
                def _(r=r):
                    act(pltpu.make_async_copy(zero_ref.at[pl.ds(0, 1), :], xs_hbm.at[pl.ds(lo + r, 1), :], zsem))
            return carry

        lax.fori_loop(0, N_EXPERTS, per_expert, 0)

        def trailing(p, carry):
            off = pl.multiple_of(p * ZERO_ROWS, ZERO_ROWS)
            act(pltpu.make_async_copy(zero_ref, xs_hbm.at[pl.ds(off, ZERO_ROWS), :], zsem))
            return carry

        lax.fori_loop(zlo_ref[N_EXPERTS] // ZERO_ROWS, xs_hbm.shape[0] // ZERO_ROWS, trailing, 0)

    @pl.when(i == 0)
    def _():
        zero_ref[...] = jnp.zeros_like(zero_ref)
        zero_pieces(lambda cp: cp.start())

    def issue(r, carry):
        t = i * tm + r
        src = x_ref.at[pl.ds(r, 1), :]
        pltpu.make_async_copy(src, xs_hbm.at[pl.ds(pos0_ref[t], 1), :], sem).start()
        pltpu.make_async_copy(src, xs_hbm.at[pl.ds(pos1_ref[t], 1), :], sem).start()
        return carry

    lax.fori_loop(0, tm, issue, 0, unroll=8)
    for _ in range(TOP_K):
        pltpu.make_async_copy(x_ref, xs_hbm.at[pl.ds(0, tm), :], sem).wait()

    @pl.when(i == 0)
    def _():
        zero_pieces(lambda cp: cp.wait())


def _moe_dispatch(x, pos0, pos1, zlo, zhi, n_rows):
    m, d = x.shape
    tm = DISP_TM
    return pl.pallas_call(
        _dispatch_body,
        grid_spec=pltpu.PrefetchScalarGridSpec(
            num_scalar_prefetch=4, grid=(m // tm,),
            in_specs=[pl.BlockSpec((tm, d), lambda i, *_: (i, 0))],
            out_specs=pl.BlockSpec(memory_space=pl.ANY),
            scratch_shapes=[pltpu.VMEM((ZERO_ROWS, d), F32), pltpu.SemaphoreType.DMA(()),
                            pltpu.SemaphoreType.DMA(())]),
        out_shape=jax.ShapeDtypeStruct((n_rows, d), F32),
        compiler_params=_cparams("arbitrary"),
        name="moe_dispatch",
    )(pos0, pos1, zlo, zhi, x)


def _expert_body(eid_ref, nused_ref, x_ref, w13_ref, w2_ref, o_ref):
    ff = w2_ref.shape[0]

    used = pl.program_id(0) < nused_ref[0]

    @pl.when(used)
    def _():
        hcat = jnp.dot(x_ref[...], w13_ref[...], preferred_element_type=F32)
        hid = _silu(hcat[:, :ff]) * hcat[:, ff:]
        o_ref[...] = jnp.dot(hid, w2_ref[...], preferred_element_type=F32)

    @pl.when(jnp.logical_not(used))
    def _():
        o_ref[...] = jnp.zeros_like(o_ref)


def _moe_experts(xs, block_eid, n_used, w13, w2, layer):
    n_rows, d = xs.shape
    tb = MOE_TB
    ff = w2.shape[2]

    def row_block(b, eid, nused):
        return (jnp.minimum(b, nused[0] - 1), 0)

    return pl.pallas_call(
        _expert_body,
        grid_spec=pltpu.PrefetchScalarGridSpec(
            num_scalar_prefetch=2, grid=(n_rows // tb,),
            in_specs=[pl.BlockSpec((tb, d), row_block),
                      pl.BlockSpec((None, None, d, 2 * ff), lambda b, eid, nused: (layer, eid[b], 0, 0)),
                      pl.BlockSpec((None, None, ff, d), lambda b, eid, nused: (layer, eid[b], 0, 0))],
            out_specs=pl.BlockSpec((tb, d), lambda b, eid, nused: (b, 0))),
        out_shape=jax.ShapeDtypeStruct((n_rows, d), F32),
        compiler_params=_cparams("arbitrary"),
        name="moe_experts",
    )(block_eid, n_used, xs, w13, w2)


def _combine_body(pos0_ref, pos1_ref, ys_hbm, h_ref, gate_ref, g_ref, b_ref, o_ref, ob_ref, y0_ref, y1_ref, sem0, sem1):
    i = pl.program_id(0)
    tm = h_ref.shape[0]
    slot = i % 2

    def gather(tile, dst):
        def issue(r, carry):
            t = tile * tm + r
            pltpu.make_async_copy(ys_hbm.at[pl.ds(pos0_ref[t], 1), :], y0_ref.at[dst, pl.ds(r, 1), :],
                                  sem0.at[dst]).start()
            pltpu.make_async_copy(ys_hbm.at[pl.ds(pos1_ref[t], 1), :], y1_ref.at[dst, pl.ds(r, 1), :],
                                  sem1.at[dst]).start()
            return carry

        lax.fori_loop(0, tm, issue, 0, unroll=8)

    @pl.when(i == 0)
    def _():
        gather(0, 0)

    @pl.when(i + 1 < pl.num_programs(0))
    def _():
        gather(i + 1, 1 - slot)

    pltpu.make_async_copy(ys_hbm.at[pl.ds(0, tm), :], y0_ref.at[slot], sem0.at[slot]).wait()
    pltpu.make_async_copy(ys_hbm.at[pl.ds(0, tm), :], y1_ref.at[slot], sem1.at[slot]).wait()
    gates = gate_ref[...]
    ffn = gates[:, 0:1] * y0_ref[slot] + gates[:, 1:2] * y1_ref[slot]
    out = _layer_norm(ALPHA * h_ref[...] + ffn, g_ref[...], b_ref[...])
    o_ref[...] = out
    ob_ref[...] = _mxu(out)


def _moe_combine(ys, pos0, pos1, h, gates, g, b, layer):
    m, d = h.shape
    tm = COMB_TM
    row = pl.BlockSpec((None, 1, d), lambda i, *_: (layer, 0, 0))
    return pl.pallas_call(
        _combine_body,
        grid_spec=pltpu.PrefetchScalarGridSpec(
            num_scalar_prefetch=2, grid=(m // tm,),
            in_specs=[pl.BlockSpec(memory_space=pl.ANY),
                      pl.BlockSpec((tm, d), lambda i, *_: (i, 0)),
                      pl.BlockSpec((tm, LANES), lambda i, *_: (i, 0)), row, row],
            out_specs=[pl.BlockSpec((tm, d), lambda i, *_: (i, 0)), pl.BlockSpec((tm, d), lambda i, *_: (i, 0))],
            scratch_shapes=[pltpu.VMEM((2, tm, d), F32), pltpu.VMEM((2, tm, d), F32),
                            pltpu.SemaphoreType.DMA((2,)), pltpu.SemaphoreType.DMA((2,))]),
        out_shape=[jax.ShapeDtypeStruct((m, d), F32), jax.ShapeDtypeStruct((m, d), BF16)],
        compiler_params=_cparams("arbitrary"),
        name="moe_combine",
    )(pos0, pos1, ys, h, gates, g, b)


def _moe_layer(h, w_group, w_expert, w13, w2, ln_g, ln_b, layer):
    n, d = h.shape
    tb = MOE_TB
    w_route = jnp.pad(jnp.concatenate([w_group, w_expert], axis=1), ((0, 0), (0, LANES - N_GROUPS - N_EXPERTS)))
    info, gates, cnt = _moe_router(h, w_route)
    counts = cnt[0, N_GROUPS:N_GROUPS + N_EXPERTS]
    padded = (counts + tb - 1) // tb * tb
    pad_end = jnp.cumsum(padded)
    pad_start = pad_end - padded
    pos = _moe_positions(info, pad_start)
    pos0, pos1 = pos[0], pos[1]
    n_blocks = (n * TOP_K + N_EXPERTS * (tb - 1) + tb - 1) // tb
    n_used = (pad_end[-1] // tb).astype(I32)
    blk = jnp.minimum(jnp.arange(n_blocks, dtype=I32), n_used - 1)
    block_eid = jnp.minimum(jnp.sum(pad_end[None, :] <= (blk * tb)[:, None], axis=1), N_EXPERTS - 1).astype(I32)
    zlo = jnp.concatenate([pad_start + counts, pad_end[-1:]])
    xs = _moe_dispatch(h, pos0, pos1, zlo, pad_end, n_blocks * tb)
    ys = _moe_experts(xs, block_eid, n_used.reshape(1), w13, w2, layer)
    return _moe_combine(ys, pos0, pos1, h, gates, ln_g, ln_b, layer)


def _lambda_init(layer_idx):
    return 0.8 - 0.6 * math.exp(-0.3 * layer_idx)


def kernel(x, a_w_in, a_conv_w, a_a_log, a_dt_bias, a_norm_w, a_w_out, kv_w, b_w_q, b_lambda, b_subln_w, b_w_out,
           ln_mix_g, ln_mix_b, ln_ffn_g, ln_ffn_b, moe_w_group, moe_w_expert, moe_w13, moe_w2):
    b, t, d = x.shape
    n = b * t
    dn_w = DN_HEADS * DN_HEAD_DIM
    h = x.reshape(n, d)

    def per_layer_rows(p):
        return p.reshape(p.shape[0], 1, p.shape[1])

    a_norm_w = per_layer_rows(a_norm_w)
    b_subln_w = b_subln_w.reshape(b_subln_w.shape + (1,))
    ln_mix_g, ln_mix_b = per_layer_rows(ln_mix_g), per_layer_rows(ln_mix_b)
    ln_ffn_g, ln_ffn_b = per_layer_rows(ln_ffn_g), per_layer_rows(ln_ffn_b)
    da_w = DA_HEADS * DA_V_DIM
    w_in, w_out_a = _mxu(a_w_in), _mxu(a_w_out)
    w_k, w_vt = _mxu(kv_w[:, :da_w]).reshape(1, d, da_w), _mxu(kv_w[:, da_w:].T)
    w_q, w_out_b = _mxu(b_w_q), _mxu(b_w_out)
    kp = vt = None
    hx = h
    for layer in range(DEPTH):
        if layer < N_A_LAYERS:
            qkv = _dn_qkv_proj(hx, w_in, a_conv_w, layer, t)
            z = _matmul(hx, w_in, layer, dn_w, "dn_z_proj", BF16, col0=3 * dn_w)
            gb = _dn_gates(hx, a_w_in[layer, :, 4 * dn_w:], a_a_log[layer], a_dt_bias[layer]).reshape(b, t, LANES)
            mix_in = _dn_delta(qkv.reshape(b, t, 3 * dn_w), z.reshape(b, t, dn_w), gb, a_norm_w, layer).reshape(n, dn_w)
            h = _proj_res_ln(mix_in, w_out_a, h, ln_mix_g, ln_mix_b, layer, layer, "dn_out_ln")
        else:
            j = layer - N_A_LAYERS
            if j == 0:
                kp = _matmul(hx, w_k, 0, da_w, "k_proj", BF16).reshape(b, t, da_w)
                vt = _matmul_nt(w_vt, hx, "v_proj", BF16)
            qp = _matmul(hx, w_q, j, da_w, "q_proj", BF16).reshape(b, t, da_w)
            mix_in = _diff_attention(qp, kp, vt, b_lambda, b_subln_w, j, _lambda_init(layer)).reshape(n, da_w)
            h = _proj_res_ln(mix_in, w_out_b, h, ln_mix_g, ln_mix_b, j, layer, "da_out_ln")
        h, hx = _moe_layer(h, moe_w_group[layer], moe_w_expert[layer], moe_w13, moe_w2, ln_ffn_g, ln_ffn_b, layer)
    return h.reshape(b, t, d)
```

```python
import functools
import math

import jax
import jax.numpy as jnp
from jax import lax
from jax.experimental import pallas as pl
from jax.experimental.pallas import tpu as pltpu

F32 = jnp.float32
BF16 = jnp.bfloat16
I32 = jnp.int32

DEPTH = 4
N_A_LAYERS = DEPTH // 2
DN_HEADS = 8
DN_HEAD_DIM = 128
CONV_K = 4
DN_CHUNK = 64
DA_HEADS = 8
DA_HEAD_DIM = 64
DA_V_DIM = 2 * DA_HEAD_DIM
N_GROUPS = 4
EXPERTS_PER_GROUP = 8
N_EXPERTS = N_GROUPS * EXPERTS_PER_GROUP
TOP_K = 2
ALPHA = (2 * DEPTH) ** 0.25
LN_EPS = 1e-5
RMS_EPS = 1e-6

LANES = 128
SUBLANES = 8
BF16_SUBLANES = 16
VMEM_LIMIT_BYTES = 56 * 1024 * 1024

MM_TM = 2048
MM_TN = 512
LN_TM = 512
DN_ROWS = 256
PREP_CHUNKS = 2
ATT_TK = 512
ATT_HEADS = 2
ROUTE_TM = 512
MOE_TB = 256
DISP_TM = 512
COMB_TM = 256


def _cparams(*sem):
    return pltpu.CompilerParams(dimension_semantics=sem, vmem_limit_bytes=VMEM_LIMIT_BYTES)


def _silu(x):
    return x * (1.0 / (1.0 + jnp.exp(-x)))


def _layer_norm(x, g, b):
    mu = jnp.mean(x, axis=-1, keepdims=True)
    xc = x - mu
    var = jnp.mean(xc * xc, axis=-1, keepdims=True)
    return xc * lax.rsqrt(var + LN_EPS) * g + b


def _mxu(x):
    return x.astype(BF16)


def _mm_body(x_ref, w_ref, o_ref):
    o_ref[...] = jnp.dot(_mxu(x_ref[...]), _mxu(w_ref[...]), preferred_element_type=F32).astype(o_ref.dtype)


def _matmul(x, w, layer, n_out, name, out_dtype=F32, col0=0):
    m, k = x.shape
    j0 = col0 // MM_TN
    return pl.pallas_call(
        _mm_body,
        grid=(m // MM_TM, n_out // MM_TN),
        in_specs=[pl.BlockSpec((MM_TM, k), lambda i, j: (i, 0)),
                  pl.BlockSpec((None, k, MM_TN), lambda i, j: (layer, 0, j0 + j))],
        out_specs=pl.BlockSpec((MM_TM, MM_TN), lambda i, j: (i, j)),
        out_shape=jax.ShapeDtypeStruct((m, n_out), out_dtype),
        compiler_params=_cparams("parallel", "parallel"),
        name=name,
    )(x, w)


def _gates_body(x_ref, w_ref, alog_ref, dtb_ref, o_ref):
    logit = jnp.dot(_mxu(x_ref[...]), _mxu(w_ref[...]), preferred_element_type=F32)
    lane = lax.broadcasted_iota(I32, logit.shape, 1)
    beta = 1.0 / (1.0 + jnp.exp(-logit))
    sp_in = logit + dtb_ref[...]
    softplus = jnp.maximum(sp_in, 0.0) + jnp.log1p(jnp.exp(-jnp.abs(sp_in)))
    g = -jnp.exp(alog_ref[...]) * softplus
    o_ref[...] = jnp.where(lane < DN_HEADS, beta, g)


def _dn_gates(x, w_small, a_log, dt_bias):
    m, k = x.shape
    pad = LANES - 2 * DN_HEADS
    w = jnp.pad(w_small, ((0, 0), (0, pad)))
    alog = jnp.pad(a_log, (DN_HEADS, pad)).reshape(1, LANES)
    dtb = jnp.pad(dt_bias, (DN_HEADS, pad)).reshape(1, LANES)
    return pl.pallas_call(
        _gates_body,
        grid=(m // MM_TM,),
        in_specs=[pl.BlockSpec((MM_TM, k), lambda i: (i, 0)),
                  pl.BlockSpec((k, LANES), lambda i: (0, 0)),
                  pl.BlockSpec((1, LANES), lambda i: (0, 0)),
                  pl.BlockSpec((1, LANES), lambda i: (0, 0))],
        out_specs=pl.BlockSpec((MM_TM, LANES), lambda i: (i, 0)),
        out_shape=jax.ShapeDtypeStruct((m, LANES), F32),
        compiler_params=_cparams("parallel"),
        name="dn_gates",
    )(x, w, alog, dtb)


def _qkv_proj_body(x_ref, w_ref, cw_ref, qkv_ref, halo_ref, *, tiles_per_seq, tiles_per_part):
    i, j = pl.program_id(0), pl.program_id(1)
    proj = jnp.dot(_mxu(x_ref[...]), _mxu(w_ref[...]), preferred_element_type=F32)
    tm, tn = proj.shape
    prev = jnp.where(i % tiles_per_seq == 0, 0.0, halo_ref[j])
    halo_ref[j] = proj[tm - SUBLANES:]
    xe = jnp.concatenate([prev, proj], axis=0)
    acc = proj * cw_ref[CONV_K - 1:CONV_K, :]
    for tap in range(CONV_K - 1):
        shifted = pltpu.roll(xe, CONV_K - 1 - tap, axis=0)[SUBLANES:]
        acc = acc + shifted * cw_ref[tap:tap + 1, :]
    y = _silu(acc)
    normed = []
    for h in range(tn // DN_HEAD_DIM):
        yh = y[:, h * DN_HEAD_DIM:(h + 1) * DN_HEAD_DIM]
        ss = jnp.sum(yh * yh, axis=-1, keepdims=True)
        normed.append(yh * lax.rsqrt(ss + RMS_EPS))
    normed = jnp.concatenate(normed, axis=-1)
    q_scale = jnp.where(j < tiles_per_part, DN_HEAD_DIM ** -0.5, 1.0)
    qkv_ref[...] = jnp.where(j < 2 * tiles_per_part, normed * q_scale, y).astype(qkv_ref.dtype)


def _dn_qkv_proj(x, w_in, conv_w, layer, seq_len):
    m, k = x.shape
    width = DN_HEADS * DN_HEAD_DIM
    tm, tn = MM_TM, MM_TN
    conv_tiles = 3 * width // tn
    body = functools.partial(_qkv_proj_body, tiles_per_seq=seq_len // tm, tiles_per_part=width // tn)
    return pl.pallas_call(
        body,
        grid=(m // tm, conv_tiles),
        in_specs=[pl.BlockSpec((tm, k), lambda i, j: (i, 0)),
                  pl.BlockSpec((None, k, tn), lambda i, j: (layer, 0, j)),
                  pl.BlockSpec((None, CONV_K, tn), lambda i, j: (layer, 0, j))],
        out_specs=pl.BlockSpec((tm, tn), lambda i, j: (i, j)),
        out_shape=jax.ShapeDtypeStruct((m, 3 * width), BF16),
        scratch_shapes=[pltpu.VMEM((conv_tiles, SUBLANES, tn), F32)],
        compiler_params=_cparams("arbitrary", "arbitrary"),
        name="dn_qkv_proj",
    )(x, w_in, conv_w)


def _split3(x):
    hi = x.astype(jnp.bfloat16).astype(F32)
    r = x - hi
    mid = r.astype(jnp.bfloat16).astype(F32)
    return hi, mid, r - mid


def _nt(a, b):
    return lax.dot_general(a, b, (((1,), (1,)), ((), ())), preferred_element_type=F32)


def _tn(a, b):
    return lax.dot_general(a, b, (((0,), (0,)), ((), ())), preferred_element_type=F32)


def _delta_body(u_ref, w_ref, qd_ref, kd_ref, attn_ref, egl_ref, z_ref, nw_ref, o_ref, state_ref):
    c = DN_CHUNK
    d = DN_HEAD_DIM

    @pl.when(pl.program_id(0) == 0)
    def _():
        state_ref[...] = jnp.zeros_like(state_ref)

    nw = nw_ref[...]
    nb = u_ref.shape[0]
    chains = [(bi, h) for bi in range(nb) for h in range(DN_HEADS)]

    def chunk(ic, carry):
        rows = pl.ds(pl.multiple_of(ic * c, c), c)
        s_old, ws_qs = {}, {}
        for bi, h in chains:
            cols = slice(h * d, (h + 1) * d)
            s_old[bi, h] = state_ref[bi * DN_HEADS + h]
            lhs = jnp.concatenate([w_ref[bi, rows, cols], qd_ref[bi, rows, cols]], axis=0)
            ws_qs[bi, h] = jnp.dot(lhs, _mxu(s_old[bi, h]), preferred_element_type=F32)
        v_new = {}
        for bi, h in chains:
            cols = slice(h * d, (h + 1) * d)
            v_new[bi, h] = u_ref[bi, rows, cols] - ws_qs[bi, h][:c]
        for bi, h in chains:
            cols = slice(h * d, (h + 1) * d)
            acols = slice(h * c, (h + 1) * c)
            vn = _mxu(v_new[bi, h])
            o = ws_qs[bi, h][c:] + jnp.dot(attn_ref[bi, rows, acols], vn, preferred_element_type=F32)
            gl = egl_ref[bi, pl.ds(ic * DN_HEADS + h, 1), :]
            state_ref[bi * DN_HEADS + h] = s_old[bi, h] * gl + _tn(kd_ref[bi, rows, cols], vn)
            ms = jnp.mean(o * o, axis=-1, keepdims=True)
            z = z_ref[bi, rows, cols].astype(F32)
            o_ref[bi, rows, cols] = (o * lax.rsqrt(ms + RMS_EPS) * nw * _silu(z)).astype(o_ref.dtype)
        return carry

    lax.fori_loop(0, u_ref.shape[1] // c, chunk, 0)


def _dn_prep_body(q_ref, k_ref, v_ref, gb_ref, u_ref, w_ref, qd_ref, kd_ref, attn_ref, egl_ref):
    c = DN_CHUNK
    d = DN_HEAD_DIM
    ri = lax.broadcasted_iota(I32, (c, c), 0)
    ci = lax.broadcasted_iota(I32, (c, c), 1)
    causal = ri >= ci
    strict = ri > ci
    tri = causal.astype(F32)
    eye = (ri == ci).astype(F32)
    base = 4
    base_mask = strict & (ri // base == ci // base)
    pair_masks = [(ri // (2 * s) == ci // (2 * s)) & ((ri // s) % 2 == 1) & ((ci // s) % 2 == 0)
                  for s in (base << n for n in range(int(math.log2(c // base))))]
    heads = [(cc, hd) for cc in range(PREP_CHUNKS) for hd in range(DN_HEADS)]

    def setup(ic, cc, a_mat, x):
        rows = pl.ds(pl.multiple_of(ic * c, c), c)
        gbt = gb_ref[0, rows, :]
        g_hi, g_mid, g_lo = _split3(gbt)
        gcol = (jnp.dot(tri, g_hi, preferred_element_type=F32)
                + jnp.dot(tri, g_mid, preferred_element_type=F32)
                + jnp.dot(tri, g_lo, preferred_element_type=F32))
        grow = gcol.T
        g_last_col = grow[DN_HEADS:2 * DN_HEADS, c - 1:c]
        egl_ref[0, pl.ds(pl.multiple_of(ic * DN_HEADS, DN_HEADS), DN_HEADS), :] = jnp.broadcast_to(
            jnp.exp(g_last_col), (DN_HEADS, d))
        for hd in range(DN_HEADS):
            cols = slice(hd * d, (hd + 1) * d)
            qh = q_ref[0, rows, cols]
            kh = k_ref[0, rows, cols]
            qf, kf = qh.astype(F32), kh.astype(F32)
            beta_c = gbt[:, hd:hd + 1]
            gc_c = gcol[:, DN_HEADS + hd:DN_HEADS + hd + 1]
            gc_r = grow[DN_HEADS + hd:DN_HEADS + hd + 1, :]
            decay = jnp.exp(jnp.where(causal, gc_c - gc_r, -jnp.inf))
            eg_c = jnp.exp(gc_c)
            k_beta = kf * beta_c
            a_mat[cc, hd] = jnp.where(strict, _nt(_mxu(k_beta), kh) * decay, 0.0)
            attn_ref[0, rows, hd * c:(hd + 1) * c] = _mxu(jnp.where(causal, _nt(qh, kh) * decay, 0.0))
            qd_ref[0, rows, cols] = _mxu(qf * eg_c)
            kd_ref[0, rows, cols] = _mxu(kf * jnp.exp(gc_r[:, c - 1:c] - gc_c))
            x[cc, hd] = jnp.concatenate([v_ref[0, rows, cols].astype(F32) * beta_c, k_beta * eg_c], axis=-1)
        return rows

    def chunk(ip, carry):
        a_mat, x, p = {}, {}, {}
        rows_of = [setup(PREP_CHUNKS * ip + cc, cc, a_mat, x) for cc in range(PREP_CHUNKS)]
        t_inv = {}
        for h in heads:
            a0 = jnp.where(base_mask, a_mat[h], 0.0)
            p[h] = jnp.dot(a0, a0, preferred_element_type=F32)
            t_inv[h] = eye - a0 + p[h]
        for h in heads:
            a0 = jnp.where(base_mask, a_mat[h], 0.0)
            t_inv[h] = t_inv[h] - jnp.dot(a0, p[h], preferred_element_type=F32)
        for pair_mask in pair_masks:
            for h in heads:
                p[h] = jnp.dot(jnp.where(pair_mask, a_mat[h], 0.0), t_inv[h], preferred_element_type=F32)
            for h in heads:
                t_inv[h] = t_inv[h] - jnp.dot(t_inv[h], p[h], preferred_element_type=F32)
        for h in heads:
            x[h] = x[h] + jnp.dot(t_inv[h] - eye, x[h], preferred_element_type=F32)
        for cc, hd in heads:
            cols = slice(hd * d, (hd + 1) * d)
            u_ref[0, rows_of[cc], cols] = x[cc, hd][:, :d]
            w_ref[0, rows_of[cc], cols] = _mxu(x[cc, hd][:, d:])
        return carry

    lax.fori_loop(0, q_ref.shape[1] // (c * PREP_CHUNKS), chunk, 0)


def _dn_delta(qkv, z, gb, norm_w, layer):
    b, t, w = z.shape
    rs = DN_ROWS
    cps = rs // DN_CHUNK
    aw = DN_HEADS * DN_CHUNK
    spec = pl.BlockSpec((1, rs, w), lambda bi, i: (bi, i, 0))
    full = jax.ShapeDtypeStruct((b, t, w), F32)
    mxu_only = jax.ShapeDtypeStruct((b, t, w), BF16)
    u, wy, qd, kd, attn, egl = pl.pallas_call(
        _dn_prep_body,
        grid=(b, t // rs),
        in_specs=[pl.BlockSpec((1, rs, w), lambda bi, i: (bi, i, 0)),
                  pl.BlockSpec((1, rs, w), lambda bi, i: (bi, i, 1)),
                  pl.BlockSpec((1, rs, w), lambda bi, i: (bi, i, 2)),
                  pl.BlockSpec((1, rs, LANES), lambda bi, i: (bi, i, 0))],
        out_specs=[spec, spec, spec, spec,
                   pl.BlockSpec((1, rs, aw), lambda bi, i: (bi, i, 0)),
                   pl.BlockSpec((1, cps * DN_HEADS, DN_HEAD_DIM), lambda bi, i: (bi, i, 0))],
        out_shape=[full, mxu_only, mxu_only, mxu_only,
                   jax.ShapeDtypeStruct((b, t, aw), BF16),
                   jax.ShapeDtypeStruct((b, t // DN_CHUNK * DN_HEADS, DN_HEAD_DIM), F32)],
        compiler_params=_cparams("parallel", "parallel"),
        name="dn_prep",
    )(qkv, qkv, qkv, gb)
    bspec = pl.BlockSpec((b, rs, w), lambda i: (0, i, 0))
    return pl.pallas_call(
        _delta_body,
        grid=(t // rs,),
        in_specs=[bspec, bspec, bspec, bspec,
                  pl.BlockSpec((b, rs, aw), lambda i: (0, i, 0)),
                  pl.BlockSpec((b, cps * DN_HEADS, DN_HEAD_DIM), lambda i: (0, i, 0)),
                  bspec,
                  pl.BlockSpec((None, 1, DN_HEAD_DIM), lambda i: (layer, 0, 0))],
        out_specs=bspec,
        out_shape=mxu_only,
        scratch_shapes=[pltpu.VMEM((b * DN_HEADS, DN_HEAD_DIM, DN_HEAD_DIM), F32)],
        compiler_params=_cparams("arbitrary"),
        name="dn_scan",
    )(u, wy, qd, kd, attn, egl, z, norm_w)


def _proj_ln_body(a_ref, w_ref, h_ref, g_ref, b_ref, o_ref):
    y = jnp.dot(_mxu(a_ref[...]), _mxu(w_ref[...]), preferred_element_type=F32)
    o_ref[...] = _layer_norm(ALPHA * h_ref[...] + y, g_ref[...], b_ref[...])


def _proj_res_ln(a, w, h, g, b, w_layer, ln_layer, name):
    m, k = a.shape
    n = h.shape[1]
    row = pl.BlockSpec((None, 1, n), lambda i: (ln_layer, 0, 0))
    return pl.pallas_call(
        _proj_ln_body,
        grid=(m // LN_TM,),
        in_specs=[pl.BlockSpec((LN_TM, k), lambda i: (i, 0)),
                  pl.BlockSpec((None, k, n), lambda i: (w_layer, 0, 0)),
                  pl.BlockSpec((LN_TM, n), lambda i: (i, 0)), row, row],
        out_specs=pl.BlockSpec((LN_TM, n), lambda i: (i, 0)),
        out_shape=jax.ShapeDtypeStruct((m, n), F32),
        compiler_params=_cparams("parallel"),
        name=name,
    )(a, w, h, g, b)


def _mm_nt_body(w_ref, x_ref, o_ref):
    o_ref[...] = _nt(_mxu(w_ref[...]), _mxu(x_ref[...])).astype(o_ref.dtype)


def _matmul_nt(w_t, x, name, out_dtype=F32):
    n, k = w_t.shape
    m = x.shape[0]
    return pl.pallas_call(
        _mm_nt_body,
        grid=(n // MM_TN, m // MM_TM),
        in_specs=[pl.BlockSpec((MM_TN, k), lambda j, i: (j, 0)),
                  pl.BlockSpec((MM_TM, k), lambda j, i: (i, 0))],
        out_specs=pl.BlockSpec((MM_TN, MM_TM), lambda j, i: (j, i)),
        out_shape=jax.ShapeDtypeStruct((n, m), out_dtype),
        compiler_params=_cparams("parallel", "parallel"),
        name=name,
    )(w_t, x)


def _attn_body(q_ref, k_ref, vt_ref, lam_ref, sw_ref, o_ref, m_ref, l_ref, acc_ref, sta_ref, stb_ref, *, lam_init):
    i = pl.program_id(2)
    t = ATT_TK
    dv = DA_V_DIM
    heads = range(ATT_HEADS)
    maps = [(hh, s) for hh in heads for s in range(2)]
    q_maps = []
    for hh in heads:
        q = q_ref[0, :, hh * dv:(hh + 1) * dv].astype(F32) * (DA_HEAD_DIM ** -0.5 * math.log2(math.e))
        lane = lax.broadcasted_iota(I32, q.shape, 1)
        q_maps += [_mxu(jnp.where(lane < DA_HEAD_DIM, q, 0.0)), _mxu(jnp.where(lane >= DA_HEAD_DIM, q, 0.0))]
    m_ref[...] = jnp.full_like(m_ref, -jnp.inf)
    l_ref[...] = jnp.zeros_like(l_ref)
    acc_ref[...] = jnp.zeros_like(acc_ref)

    st_bufs = (sta_ref, stb_ref)

    def scores(j, buf, q0=0):
        rows = pl.ds(pl.multiple_of(j * t, t), t)
        for hh, s in maps:
            k = k_ref[0, rows, hh * dv:(hh + 1) * dv]
            st_bufs[buf][2 * hh + s, :, q0:] = _nt(k, q_maps[2 * hh + s][q0:])

    def accumulate(j, buf, q0=0, diagonal=False):
        rows = pl.ds(pl.multiple_of(j * t, t), t)
        ones = jnp.ones((BF16_SUBLANES, t), BF16)
        vt = [jnp.concatenate([vt_ref[hh * dv:(hh + 1) * dv, rows], ones], axis=0) for hh in heads]
        idx = [2 * hh + s for hh, s in maps]
        st = [st_bufs[buf][x, :, q0:] for x in idx]
        if diagonal:
            ri = lax.broadcasted_iota(I32, st[0].shape, 0)
            ci = lax.broadcasted_iota(I32, st[0].shape, 1)
            st = [jnp.where(ri <= ci, x, -jnp.inf) for x in st]
        m_old = [m_ref[x, :, q0:] for x in idx]
        m_new = [jnp.maximum(mo, jnp.max(x, axis=0, keepdims=True)) for mo, x in zip(m_old, st)]
        p = [_mxu(jnp.exp2(x - mn)) for x, mn in zip(st, m_new)]
        pv = [jnp.dot(vt[x // 2], px, preferred_element_type=F32) for x, px in zip(idx, p)]
        for n, x in enumerate(idx):
            corr = jnp.exp2(m_old[n] - m_new[n])
            l_ref[x, :, q0:] = corr * l_ref[x, :, q0:] + pv[n][dv:dv + 1]
            acc_ref[x, :, q0:] = corr * acc_ref[x, :, q0:] + pv[n][:dv]
            m_ref[x, :, q0:] = m_new[n]

    scores(0, 0)

    def two_tiles(u, carry):
        j = 2 * u
        scores(j + 1, 1)
        accumulate(j, 0)
        scores(j + 2, 0)
        accumulate(j + 1, 1)
        return carry

    lax.fori_loop(0, i, two_tiles, 0)
    scores(2 * i + 1, 1, q0=t)
    accumulate(2 * i, 0, diagonal=True)
    accumulate(2 * i + 1, 1, q0=t, diagonal=True)

    lp = lam_ref[...]
    lam = (jnp.exp(jnp.sum(lp[0:1] * lp[1:2], axis=-1, keepdims=True))
           - jnp.exp(jnp.sum(lp[2:3] * lp[3:4], axis=-1, keepdims=True)) + lam_init)
    for hh in heads:
        ot = acc_ref[2 * hh] / l_ref[2 * hh] - lam * (acc_ref[2 * hh + 1] / l_ref[2 * hh + 1])
        ms = jnp.mean(ot * ot, axis=0, keepdims=True)
        ot = ot * lax.rsqrt(ms + RMS_EPS) * sw_ref[...] * (1.0 - lam_init)
        o_ref[0, :, hh * dv:(hh + 1) * dv] = ot.T.astype(o_ref.dtype)


def _diff_attention(qp, kp, vt, lam_p, subln_w, layer, lam_init):
    b, t, w = qp.shape
    tq, tk = 2 * ATT_TK, ATT_TK
    hw = ATT_HEADS * DA_V_DIM
    return pl.pallas_call(
        functools.partial(_attn_body, lam_init=lam_init),
        grid=(b, DA_HEADS // ATT_HEADS, t // tq),
        in_specs=[pl.BlockSpec((1, tq, hw), lambda bi, h, i: (bi, i, h)),
                  pl.BlockSpec((1, t, hw), lambda bi, h, i: (bi, 0, h)),
                  pl.BlockSpec((hw, t), lambda bi, h, i: (h, bi)),
                  pl.BlockSpec((None, 4, DA_HEAD_DIM), lambda bi, h, i: (layer, 0, 0)),
                  pl.BlockSpec((None, DA_V_DIM, 1), lambda bi, h, i: (layer, 0, 0))],
        out_specs=pl.BlockSpec((1, tq, hw), lambda bi, h, i: (bi, i, h)),
        out_shape=jax.ShapeDtypeStruct((b, t, w), BF16),
        scratch_shapes=[pltpu.VMEM((2 * ATT_HEADS, 1, tq), F32), pltpu.VMEM((2 * ATT_HEADS, 1, tq), F32),
                        pltpu.VMEM((2 * ATT_HEADS, DA_V_DIM, tq), F32),
                        pltpu.VMEM((2 * ATT_HEADS, tk, tq), F32), pltpu.VMEM((2 * ATT_HEADS, tk, tq), F32)],
        compiler_params=_cparams("parallel", "parallel", "parallel"),
        name="diff_attn",
    )(qp, kp, vt, lam_p, subln_w)


def _router_body(h_ref, w_ref, info_ref, gate_ref, cnt_ref, carry_ref):
    tm = h_ref.shape[0]

    @pl.when(pl.program_id(0) == 0)
    def _():
        carry_ref[...] = jnp.zeros_like(carry_ref)

    h = h_ref[...]
    w = w_ref[...]
    h_hi = h.astype(jnp.bfloat16)
    h_lo = (h - h_hi.astype(F32)).astype(jnp.bfloat16)
    w_hi = w.astype(jnp.bfloat16)
    w_lo = (w - w_hi.astype(F32)).astype(jnp.bfloat16)
    logits = (jnp.dot(h_hi, w_hi, preferred_element_type=F32) + jnp.dot(h_lo, w_hi, preferred_element_type=F32)
              + jnp.dot(h_hi, w_lo, preferred_element_type=F32))
    lane = lax.broadcasted_iota(I32, logits.shape, 1)
    big = jnp.int32(LANES)

    def first_max(vals):
        mx = jnp.max(vals, axis=-1, keepdims=True)
        idx = jnp.min(jnp.where(vals == mx, lane, big), axis=-1, keepdims=True)
        return mx, idx

    gl = jnp.where(lane < N_GROUPS, logits, -jnp.inf)
    gmax, gidx = first_max(gl)
    p_group = 1.0 / jnp.sum(jnp.exp(gl - gmax), axis=-1, keepdims=True)
    lo = N_GROUPS + gidx * EXPERTS_PER_GROUP
    el = jnp.where((lane >= lo) & (lane < lo + EXPERTS_PER_GROUP), logits, -jnp.inf)
    m1, i1 = first_max(el)
    m2, i2 = first_max(jnp.where(lane == i1, -jnp.inf, el))
    e2 = jnp.exp(m2 - m1)
    gate1 = p_group / (1.0 + e2)
    gate2 = p_group * e2 / (1.0 + e2)

    oh = ((lane == i1) | (lane == i2)).astype(F32)
    ri = lax.broadcasted_iota(I32, (tm, tm), 0)
    ci = lax.broadcasted_iota(I32, (tm, tm), 1)
    before = jnp.dot((ci < ri).astype(F32), oh, preferred_element_type=F32) + carry_ref[0:1, :]
    rank1 = jnp.sum(jnp.where(lane == i1, before, 0.0), axis=-1, keepdims=True).astype(I32)
    rank2 = jnp.sum(jnp.where(lane == i2, before, 0.0), axis=-1, keepdims=True).astype(I32)
    total = carry_ref[0:1, :] + jnp.sum(oh, axis=0, keepdims=True)
    carry_ref[0:1, :] = total

    info = jnp.where(lane == 0, i1 - N_GROUPS,
                     jnp.where(lane == 1, i2 - N_GROUPS,
                               jnp.where(lane == 2, rank1, jnp.where(lane == 3, rank2, 0))))
    info_ref[...] = info.T[:SUBLANES]
    gate_ref[...] = jnp.where(lane == 0, gate1, jnp.where(lane == 1, gate2, 0.0))
    cnt_ref[...] = jnp.broadcast_to(total, cnt_ref.shape).astype(I32)


def _moe_router(h, w_route):
    m, k = h.shape
    tm = ROUTE_TM
    return pl.pallas_call(
        _router_body,
        grid=(m // tm,),
        in_specs=[pl.BlockSpec((tm, k), lambda i: (i, 0)),
                  pl.BlockSpec((k, LANES), lambda i: (0, 0))],
        out_specs=[pl.BlockSpec((SUBLANES, tm), lambda i: (0, i)),
                   pl.BlockSpec((tm, LANES), lambda i: (i, 0)),
                   pl.BlockSpec((SUBLANES, LANES), lambda i: (0, 0))],
        out_shape=[jax.ShapeDtypeStruct((SUBLANES, m), I32),
                   jax.ShapeDtypeStruct((m, LANES), F32),
                   jax.ShapeDtypeStruct((SUBLANES, LANES), I32)],
        scratch_shapes=[pltpu.VMEM((SUBLANES, LANES), F32)],
        compiler_params=_cparams("arbitrary"),
        name="moe_router",
    )(h, w_route)


def _pos_body(start_ref, info_ref, pos_ref):
    info = info_ref[...]
    base = jnp.zeros_like(info)
    for e in range(N_EXPERTS):
        base = jnp.where(info == e, start_ref[e], base)
    pos_ref[...] = base + pltpu.roll(info, SUBLANES - TOP_K, axis=0)


def _moe_positions(info, pad_start):
    return pl.pallas_call(
        _pos_body,
        grid_spec=pltpu.PrefetchScalarGridSpec(
            num_scalar_prefetch=1, grid=(1,),
            in_specs=[pl.BlockSpec(info.shape, lambda i, *_: (0, 0))],
            out_specs=pl.BlockSpec(info.shape, lambda i, *_: (0, 0))),
        out_shape=jax.ShapeDtypeStruct(info.shape, I32),
        compiler_params=_cparams("arbitrary"),
        name="moe_positions",
    )(pad_start, info)


ZERO_ROWS = MOE_TB // 2


def _dispatch_body(pos0_ref, pos1_ref, zlo_ref, zhi_ref, x_ref, xs_hbm, zero_ref, sem, zsem):
    i = pl.program_id(0)
    tm = x_ref.shape[0]

    def zero_pieces(act):
        def per_expert(e, carry):
            lo = zlo_ref[e]
            hi = zhi_ref[e]
            n = hi - lo
            size = ZERO_ROWS
            end = hi
            while size >= SUBLANES:
                take = (n & size) != 0

                @pl.when(take)
                def _(end=end, size=size):
                    off = pl.multiple_of(end - size, SUBLANES)
                    act(pltpu.make_async_copy(zero_ref.at[pl.ds(0, size), :], xs_hbm.at[pl.ds(off, size), :], zsem))

                end = end - jnp.where(take, size, 0)
                size //= 2
            for r in range(SUBLANES - 1):
                @pl.when(r < (n & (SUBLANES - 1)))
                def _(r=r):
                    act(pltpu.make_async_copy(zero_ref.at[pl.ds(0, 1), :], xs_hbm.at[pl.ds(lo + r, 1), :], zsem))
            return carry

        lax.fori_loop(0, N_EXPERTS, per_expert, 0)

        def trailing(p, carry):
            off = pl.multiple_of(p * ZERO_ROWS, ZERO_ROWS)
            act(pltpu.make_async_copy(zero_ref, xs_hbm.at[pl.ds(off, ZERO_ROWS), :], zsem))
            return carry

        lax.fori_loop(zlo_ref[N_EXPERTS] // ZERO_ROWS, xs_hbm.shape[0] // ZERO_ROWS, trailing, 0)

    @pl.when(i == 0)
    def _():
        zero_ref[...] = jnp.zeros_like(zero_ref)
        zero_pieces(lambda cp: cp.start())

    def issue(r, carry):
        t = i * tm + r
        src = x_ref.at[pl.ds(r, 1), :]
        pltpu.make_async_copy(src, xs_hbm.at[pl.ds(pos0_ref[t], 1), :], sem).start()
        pltpu.make_async_copy(src, xs_hbm.at[pl.ds(pos1_ref[t], 1), :], sem).start()
        return carry

    lax.fori_loop(0, tm, issue, 0, unroll=8)
    for _ in range(TOP_K):
        pltpu.make_async_copy(x_ref, xs_hbm.at[pl.ds(0, tm), :], sem).wait()

    @pl.when(i == 0)
    def _():
        zero_pieces(lambda cp: cp.wait())


def _moe_dispatch(x, pos0, pos1, zlo, zhi, n_rows):
    m, d = x.shape
    tm = DISP_TM
    return pl.pallas_call(
        _dispatch_body,
        grid_spec=pltpu.PrefetchScalarGridSpec(
            num_scalar_prefetch=4, grid=(m // tm,),
            in_specs=[pl.BlockSpec((tm, d), lambda i, *_: (i, 0))],
            out_specs=pl.BlockSpec(memory_space=pl.ANY),
            scratch_shapes=[pltpu.VMEM((ZERO_ROWS, d), F32), pltpu.SemaphoreType.DMA(()),
                            pltpu.SemaphoreType.DMA(())]),
        out_shape=jax.ShapeDtypeStruct((n_rows, d), F32),
        compiler_params=_cparams("arbitrary"),
        name="moe_dispatch",
    )(pos0, pos1, zlo, zhi, x)


def _expert_body(eid_ref, nused_ref, x_ref, w13_ref, w2_ref, o_ref):
    ff = w2_ref.shape[0]

    used = pl.program_id(0) < nused_ref[0]

    @pl.when(used)
    def _():
        hcat = jnp.dot(x_ref[...], w13_ref[...], preferred_element_type=F32)
        hid = _silu(hcat[:, :ff]) * hcat[:, ff:]
        o_ref[...] = jnp.dot(hid, w2_ref[...], preferred_element_type=F32)

    @pl.when(jnp.logical_not(used))
    def _():
        o_ref[...] = jnp.zeros_like(o_ref)


def _moe_experts(xs, block_eid, n_used, w13, w2, layer):
    n_rows, d = xs.shape
    tb = MOE_TB
    ff = w2.shape[2]

    def row_block(b, eid, nused):
        return (jnp.minimum(b, nused[0] - 1), 0)

    return pl.pallas_call(
        _expert_body,
        grid_spec=pltpu.PrefetchScalarGridSpec(
            num_scalar_prefetch=2, grid=(n_rows // tb,),
            in_specs=[pl.BlockSpec((tb, d), row_block),
                      pl.BlockSpec((None, None, d, 2 * ff), lambda b, eid, nused: (layer, eid[b], 0, 0)),
                      pl.BlockSpec((None, None, ff, d), lambda b, eid, nused: (layer, eid[b], 0, 0))],
            out_specs=pl.BlockSpec((tb, d), lambda b, eid, nused: (b, 0))),
        out_shape=jax.ShapeDtypeStruct((n_rows, d), F32),
        compiler_params=_cparams("arbitrary"),
        name="moe_experts",
    )(block_eid, n_used, xs, w13, w2)


def _combine_body(pos0_ref, pos1_ref, ys_hbm, h_ref, gate_ref, g_ref, b_ref, o_ref, ob_ref, y0_ref, y1_ref, sem0, sem1):
    i = pl.program_id(0)
    tm = h_ref.shape[0]
    slot = i % 2

    def gather(tile, dst):
        def issue(r, carry):
            t = tile * tm + r
            pltpu.make_async_copy(ys_hbm.at[pl.ds(pos0_ref[t], 1), :], y0_ref.at[dst, pl.ds(r, 1), :],
                                  sem0.at[dst]).start()
            pltpu.make_async_copy(ys_hbm.at[pl.ds(pos1_ref[t], 1), :], y1_ref.at[dst, pl.ds(r, 1), :],
                                  sem1.at[dst]).start()
            return carry

        lax.fori_loop(0, tm, issue, 0, unroll=8)

    @pl.when(i == 0)
    def _():
        gather(0, 0)

    @pl.when(i + 1 < pl.num_programs(0))
    def _():
        gather(i + 1, 1 - slot)

    pltpu.make_async_copy(ys_hbm.at[pl.ds(0, tm), :], y0_ref.at[slot], sem0.at[slot]).wait()
    pltpu.make_async_copy(ys_hbm.at[pl.ds(0, tm), :], y1_ref.at[slot], sem1.at[slot]).wait()
    gates = gate_ref[...]
    ffn = gates[:, 0:1] * y0_ref[slot] + gates[:, 1:2] * y1_ref[slot]
    out = _layer_norm(ALPHA * h_ref[...] + ffn, g_ref[...], b_ref[...])
    o_ref[...] = out
    ob_ref[...] = _mxu(out)


def _moe_combine(ys, pos0, pos1, h, gates, g, b, layer):
    m, d = h.shape
    tm = COMB_TM
    row = pl.BlockSpec((None, 1, d), lambda i, *_: (layer, 0, 0))
    return pl.pallas_call(
        _combine_body,
        grid_spec=pltpu.PrefetchScalarGridSpec(
            num_scalar_prefetch=2, grid=(m // tm,),
            in_specs=[pl.BlockSpec(memory_space=pl.ANY),
                      pl.BlockSpec((tm, d), lambda i, *_: (i, 0)),
                      pl.BlockSpec((tm, LANES), lambda i, *_: (i, 0)), row, row],
            out_specs=[pl.BlockSpec((tm, d), lambda i, *_: (i, 0)), pl.BlockSpec((tm, d), lambda i, *_: (i, 0))],
            scratch_shapes=[pltpu.VMEM((2, tm, d), F32), pltpu.VMEM((2, tm, d), F32),
                            pltpu.SemaphoreType.DMA((2,)), pltpu.SemaphoreType.DMA((2,))]),
        out_shape=[jax.ShapeDtypeStruct((m, d), F32), jax.ShapeDtypeStruct((m, d), BF16)],
        compiler_params=_cparams("arbitrary"),
        name="moe_combine",
    )(pos0, pos1, ys, h, gates, g, b)


def _moe_layer(h, w_group, w_expert, w13, w2, ln_g, ln_b, layer):
    n, d = h.shape
    tb = MOE_TB
    w_route = jnp.pad(jnp.concatenate([w_group, w_expert], axis=1), ((0, 0), (0, LANES - N_GROUPS - N_EXPERTS)))
    info, gates, cnt = _moe_router(h, w_route)
    counts = cnt[0, N_GROUPS:N_GROUPS + N_EXPERTS]
    padded = (counts + tb - 1) // tb * tb
    pad_end = jnp.cumsum(padded)
    pad_start = pad_end - padded
    pos = _moe_positions(info, pad_start)
    pos0, pos1 = pos[0], pos[1]
    n_blocks = (n * TOP_K + N_EXPERTS * (tb - 1) + tb - 1) // tb
    n_used = (pad_end[-1] // tb).astype(I32)
    blk = jnp.minimum(jnp.arange(n_blocks, dtype=I32), n_used - 1)
    block_eid = jnp.minimum(jnp.sum(pad_end[None, :] <= (blk * tb)[:, None], axis=1), N_EXPERTS - 1).astype(I32)
    zlo = jnp.concatenate([pad_start + counts, pad_end[-1:]])
    xs = _moe_dispatch(h, pos0, pos1, zlo, pad_end, n_blocks * tb)
    ys = _moe_experts(xs, block_eid, n_used.reshape(1), w13, w2, layer)
    return _moe_combine(ys, pos0, pos1, h, gates, ln_g, ln_b, layer)


def _lambda_init(layer_idx):
    return 0.8 - 0.6 * math.exp(-0.3 * layer_idx)


def kernel(x, a_w_in, a_conv_w, a_a_log, a_dt_bias, a_norm_w, a_w_out, kv_w, b_w_q, b_lambda, b_subln_w, b_w_out,
           ln_mix_g, ln_mix_b, ln_ffn_g, ln_ffn_b, moe_w_group, moe_w_expert, moe_w13, moe_w2):
    b, t, d = x.shape
    n = b * t
    dn_w = DN_HEADS * DN_HEAD_DIM
    h = x.reshape(n, d)

    def per_layer_rows(p):
        return p.reshape(p.shape[0], 1, p.shape[1])

    a_norm_w = per_layer_rows(a_norm_w)
    b_subln_w = b_subln_w.reshape(b_subln_w.shape + (1,))
    ln_mix_g, ln_mix_b = per_layer_rows(ln_mix_g), per_layer_rows(ln_mix_b)
    ln_ffn_g, ln_ffn_b = per_layer_rows(ln_ffn_g), per_layer_rows(ln_ffn_b)
    da_w = DA_HEADS * DA_V_DIM
    w_in, w_out_a = _mxu(a_w_in), _mxu(a_w_out)
    w_k, w_vt = _mxu(kv_w[:, :da_w]).reshape(1, d, da_w), _mxu(kv_w[:, da_w:].T)
    w_q, w_out_b = _mxu(b_w_q), _mxu(b_w_out)
    kp = vt = None
    hx = h
    for layer in range(DEPTH):
        if layer < N_A_LAYERS:
            qkv = _dn_qkv_proj(hx, w_in, a_conv_w, layer, t)
            z = _matmul(hx, w_in, layer, dn_w, "dn_z_proj", BF16, col0=3 * dn_w)
            gb = _dn_gates(hx, a_w_in[layer, :, 4 * dn_w:], a_a_log[layer], a_dt_bias[layer]).reshape(b, t, LANES)
            mix_in = _dn_delta(qkv.reshape(b, t, 3 * dn_w), z.reshape(b, t, dn_w), gb, a_norm_w, layer).reshape(n, dn_w)
            h = _proj_res_ln(mix_in, w_out_a, h, ln_mix_g, ln_mix_b, layer, layer, "dn_out_ln")
        else:
            j = layer - N_A_LAYERS
            if j == 0:
                kp = _matmul(hx, w_k, 0, da_w, "k_proj", BF16).reshape(b, t, da_w)
                vt = _matmul_nt(w_vt, hx, "v_proj", BF16)
            qp = _matmul(hx, w_q, j, da_w, "q_proj", BF16).reshape(b, t, da_w)
            mix_in = _diff_attention(qp, kp, vt, b_lambda, b_subln_w, j, _lambda_init(layer)).reshape(n, da_w)
            h = _proj_res_ln(mix_in, w_out_b, h, ln_mix_g, ln_mix_b, j, layer, "da_out_ln")
        h, hx = _moe_layer(h, moe_w_group[layer], moe_w_expert[layer], moe_w13, moe_w2, ln_ffn_g, ln_ffn_b, layer)
    return h.reshape(b, t, d)
```
